```python
import math
import jax, jax.numpy as jnp
from jax import lax
import numpy as np

D_MODEL = 1024
BATCH = 16
SEQ = 4096
DEPTH = 1
DEC_BATCH = 128
DEC_SEQ = 8
PAST_LEN = 8192
PAGE_SIZE = 128

GLA_HEADS = 4
GLA_DK = 64
GLA_DV = 128
GLA_GATE_RANK = 16
GLA_TAU = 16.0
GLA_CHUNK = 64
NSA_HEADS = 8
NSA_KV_HEADS = 2
NSA_GROUP = NSA_HEADS // NSA_KV_HEADS
NSA_HD = 64
CMP_LEN = 32
CMP_STRIDE = 16
SEL_BLOCK = 64
N_SEL = 8
WINDOW = 512
NSA_QBLOCK = 128
N_EXPERTS = 32
TOP_K = 4
D_EXPERT = 1024
SWIGLU_LIMIT = 7.0
SWIGLU_ALPHA = 1.702
MOE_BLOCK = 128
RMS_EPS = 1e-6
NEG_INF = -1e30
FORCE_SCORE = 1e4

GLA_QK_WIDTH = GLA_HEADS * GLA_DK
GLA_WIDTH = GLA_HEADS * GLA_DV
NSA_WIDTH = NSA_HEADS * NSA_HD
NSA_KV_WIDTH = 2 * NSA_KV_HEADS * NSA_HD
IN_SPLITS = (GLA_QK_WIDTH, GLA_QK_WIDTH, GLA_WIDTH, GLA_WIDTH, GLA_GATE_RANK,
             NSA_WIDTH, NSA_KV_WIDTH, NSA_KV_WIDTH, NSA_KV_WIDTH, 3 * NSA_HEADS)
D_IN = sum(IN_SPLITS)
MIX_WIDTH = GLA_WIDTH + NSA_WIDTH

kernel_name = "hymba_gla_nsa_moe_adaln_step"


def rmsnorm(x, g):
    xf = x.astype(jnp.float32)
    y = xf * lax.rsqrt(jnp.mean(jnp.square(xf), axis=-1, keepdims=True) + RMS_EPS)
    return (y * g.astype(jnp.float32)).astype(x.dtype)


def adaln(c, w_ada, b_ada):
    mod = jax.nn.silu(c) @ w_ada + b_ada
    return mod.reshape(c.shape[0], 6, 1, D_MODEL)


def modulate(x, g, shift, scale):
    return rmsnorm(x, g) * (1.0 + scale) + shift


def masked_softmax(s, mask, axis):
    s = jnp.where(mask, s.astype(jnp.float32), NEG_INF)
    p = jnp.where(mask, jnp.exp(s - jnp.max(s, axis=axis, keepdims=True)), 0.0)
    return p / jnp.maximum(jnp.sum(p, axis=axis, keepdims=True), 1e-30)


def gla_recurrence(q, k, v, log_a, s0):
    b_sz, seq_len = q.shape[:2]
    chunk = math.gcd(seq_len, GLA_CHUNK)
    n_chunks = seq_len // chunk

    def chunks(t):
        return t.astype(jnp.float32).reshape(b_sz, n_chunks, chunk, *t.shape[2:]).swapaxes(0, 1)

    causal = jnp.tril(jnp.ones((chunk, chunk), dtype=bool))

    def step(state, inp):
        qc, kc, vc, gc = inp
        cum = jnp.cumsum(gc, axis=1)
        last = cum[:, -1]
        q_dec = qc * jnp.exp(cum)
        k_inv = kc * jnp.exp(-cum)
        k_end = kc * jnp.exp(last[:, None] - cum)
        att = jnp.where(causal, jnp.einsum('bthd,bshd->bhts', q_dec, k_inv), 0.0)
        out = jnp.einsum('bhts,bshv->bthv', att, vc) + jnp.einsum('bthd,bhdv->bthv', q_dec, state)
        state = jnp.exp(last)[..., None] * state + jnp.einsum('bshd,bshv->bhdv', k_end, vc)
        return state, out

    state, out = lax.scan(step, s0.astype(jnp.float32),
                          (chunks(q * GLA_DK ** -0.5), chunks(k), chunks(v), chunks(log_a)))
    return out.swapaxes(0, 1).reshape(b_sz, seq_len, GLA_HEADS, GLA_DV), state


def gla_mixer(q, k, v, r, a_low, w_a2, b_a, g_norm, s0):
    b_sz, seq_len = q.shape[:2]

    def heads(t, d):
        return t.reshape(b_sz, seq_len, GLA_HEADS, d)

    log_a = jax.nn.log_sigmoid((a_low @ w_a2 + b_a).astype(jnp.float32)) / GLA_TAU
    o, state = gla_recurrence(heads(q, GLA_DK), heads(k, GLA_DK), heads(v, GLA_DV), heads(log_a, GLA_DK), s0)
    o = rmsnorm(o, g_norm) * jax.nn.silu(heads(r, GLA_DV).astype(jnp.float32))
    return o.reshape(b_sz, seq_len, GLA_WIDTH), state


def nsa_compress(kv, phi_pe, phi_w):
    n_sub = kv.shape[0] // CMP_STRIDE
    ratio = CMP_LEN // CMP_STRIDE
    n_cmp = n_sub - ratio + 1
    sub = kv.reshape(n_sub, CMP_STRIDE, 2, NSA_KV_HEADS, NSA_HD)
    blocks = jnp.concatenate([sub[i:i + n_cmp] for i in range(ratio)], axis=1)
    blocks = blocks + phi_pe.transpose(1, 0, 2)[None, :, :, None, :]
    flat = blocks.transpose(0, 2, 3, 1, 4).reshape(n_cmp, 2, NSA_KV_HEADS, CMP_LEN * NSA_HD)
    return jnp.einsum('nckf,cfd->nckd', flat, phi_w)


def cmp_to_sel(n_cmp, n_sel):
    i0 = jnp.arange(n_cmp)[:, None] * CMP_STRIDE
    j0 = jnp.arange(n_sel)[None, :] * SEL_BLOCK
    return ((i0 < j0 + SEL_BLOCK) & (i0 + CMP_LEN > j0)).astype(jnp.float32)


def nsa_attend(q, qpos, gates, kc, vc, cmp_end, sel_k, sel_v, win_k, win_v, win_pos):
    scale = NSA_HD ** -0.5
    s_c = jnp.einsum('qhgd,nhd->qhgn', q, kc) * scale
    m_c = (cmp_end[None, :] <= qpos[:, None])[:, None, None, :]
    p_c = masked_softmax(s_c, m_c, -1)
    o_c = jnp.einsum('qhgn,nhd->qhgd', p_c, vc.astype(jnp.float32))
    n_sel = sel_k.shape[1]
    imp = jnp.einsum('qhgn,nj->qhj', p_c, cmp_to_sel(kc.shape[0], n_sel))
    blk = jnp.arange(n_sel)[None, :]
    cur = (qpos // SEL_BLOCK)[:, None]
    forced = (blk == 0) | (blk == cur) | (blk == cur - 1)
    future = blk * SEL_BLOCK > qpos[:, None]
    score = jnp.where(future[:, None], NEG_INF, jnp.where(forced[:, None], FORCE_SCORE, imp))
    _, idx = lax.top_k(score, min(N_SEL, n_sel))
    h_idx = jnp.arange(NSA_KV_HEADS)[None, :, None]
    k_sel = sel_k[h_idx, idx]
    v_sel = sel_v[h_idx, idx]
    pos_sel = idx[..., None] * SEL_BLOCK + jnp.arange(SEL_BLOCK)
    s_s = jnp.einsum('qhgd,qhnld->qhgnl', q, k_sel) * scale
    m_s = (pos_sel <= qpos[:, None, None, None])[:, :, None]
    p_s = masked_softmax(s_s, m_s, (-2, -1))
    o_s = jnp.einsum('qhgnl,qhnld->qhgd', p_s, v_sel.astype(jnp.float32))
    s_w = jnp.einsum('qhgd,khd->qhgk', q, win_k) * scale
    dpos = qpos[:, None] - win_pos[None, :]
    m_w = ((dpos >= 0) & (dpos < WINDOW) & (win_pos[None, :] >= 0))[:, None, None, :]
    p_w = masked_softmax(s_w, m_w, -1)
    o_w = jnp.einsum('qhgk,khd->qhgd', p_w, win_v.astype(jnp.float32))
    return gates[..., 0:1] * o_c + gates[..., 1:2] * o_s + gates[..., 2:3] * o_w


def nsa_prompt_seq(q, gates, kv_c, kv_s, kv_w, phi_pe, phi_w):
    seq_len = q.shape[0]
    cmp = nsa_compress(kv_c, phi_pe, phi_w)
    cmp_end = jnp.arange(cmp.shape[0]) * CMP_STRIDE + CMP_LEN - 1
    sel = kv_s.reshape(seq_len // SEL_BLOCK, SEL_BLOCK, 2, NSA_KV_HEADS, NSA_HD).transpose(2, 3, 0, 1, 4)
    win = jnp.pad(kv_w, ((WINDOW, 0), (0, 0), (0, 0), (0, 0)))
    qb = math.gcd(seq_len, NSA_QBLOCK)

    def block(i):
        start = i * qb
        kw = lax.dynamic_slice_in_dim(win, start, WINDOW + qb, axis=0)
        return nsa_attend(lax.dynamic_slice_in_dim(q, start, qb, 0), start + jnp.arange(qb),
                          lax.dynamic_slice_in_dim(gates, start, qb, 0),
                          cmp[:, 0], cmp[:, 1], cmp_end, sel[0], sel[1],
                          kw[:, 0], kw[:, 1], start - WINDOW + jnp.arange(WINDOW + qb))

    out = lax.map(block, jnp.arange(seq_len // qb))
    return out.reshape(seq_len, NSA_WIDTH)


def nsa_sample_seq(q, gates, new_c, new_s, win_rows, pages, cache_c, cache_s, phi_pe, phi_w):
    n_new = q.shape[0]
    past_len = pages.shape[0] * PAGE_SIZE
    total = past_len + n_new
    padded = -(-total // SEL_BLOCK) * SEL_BLOCK

    def logical_rows(pool, new):
        past = pool[pages].reshape(past_len, 2, NSA_KV_HEADS, NSA_HD)
        rows = jnp.concatenate([past, new.astype(pool.dtype)], axis=0)
        return jnp.pad(rows, ((0, padded - total), (0, 0), (0, 0), (0, 0)))

    cmp = nsa_compress(logical_rows(cache_c, new_c), phi_pe, phi_w)
    cmp_end = jnp.arange(cmp.shape[0]) * CMP_STRIDE + CMP_LEN - 1
    sel = logical_rows(cache_s, new_s).reshape(padded // SEL_BLOCK, SEL_BLOCK, 2, NSA_KV_HEADS, NSA_HD)
    sel = sel.transpose(2, 3, 0, 1, 4)
    n_win = win_rows.shape[0]
    win_pos = total - n_win + jnp.arange(n_win)
    out = nsa_attend(q, past_len + jnp.arange(n_new), gates, cmp[:, 0], cmp[:, 1], cmp_end,
                     sel[0], sel[1], win_rows[:, 0], win_rows[:, 1], win_pos)
    return out.reshape(n_new, NSA_WIDTH)


def hybrid_mixer(h, gla_s0, nsa_fn, w_in, w_a2, b_a, g_gla, w_out):
    b_sz, seq_len = h.shape[:2]
    points = np.cumsum(IN_SPLITS)[:-1].tolist()
    gq, gk, gv, gr, ga, nq, nkc, nks, nkw, ngate = jnp.split(h @ w_in, points, axis=-1)
    gla_o, gla_s = gla_mixer(gq, gk, gv, gr, ga, w_a2, b_a, g_gla, gla_s0)
    q = nq.reshape(b_sz, seq_len, NSA_KV_HEADS, NSA_GROUP, NSA_HD)
    gates = jax.nn.sigmoid(ngate.astype(jnp.float32)).reshape(b_sz, seq_len, NSA_KV_HEADS, NSA_GROUP, 3)

    def kv(t):
        return t.reshape(b_sz, seq_len, 2, NSA_KV_HEADS, NSA_HD)

    kv_c, kv_s, kv_w = kv(nkc), kv(nks), kv(nkw)
    nsa_o = nsa_fn(q, gates, kv_c, kv_s, kv_w)
    y = jnp.concatenate([gla_o.astype(h.dtype), nsa_o.astype(h.dtype)], axis=-1) @ w_out
    return y, gla_s, kv_c, kv_s, kv_w


def moe_ffn(h, w_router, b_router, w_gate, b_gate, w_up, b_up, w_down, b_down):
    n_tok = h.shape[0]
    n_assign = n_tok * TOP_K
    logits = (h @ w_router).astype(jnp.float32) + b_router.astype(jnp.float32)
    top_logit, top_idx = lax.top_k(logits, TOP_K)
    top_w = jax.nn.softmax(top_logit, axis=-1)
    flat_e = top_idx.reshape(n_assign)
    order = jnp.argsort(flat_e)
    e_sorted = flat_e[order]
    counts = jnp.bincount(flat_e, length=N_EXPERTS)
    padded = (counts + MOE_BLOCK - 1) // MOE_BLOCK * MOE_BLOCK
    start = jnp.cumsum(counts) - counts
    pad_end = jnp.cumsum(padded)
    dest = (pad_end - padded)[e_sorted] + jnp.arange(n_assign) - start[e_sorted]
    n_blocks = (n_assign + N_EXPERTS * (MOE_BLOCK - 1) + MOE_BLOCK - 1) // MOE_BLOCK
    n_rows = n_blocks * MOE_BLOCK
    row_tok = jnp.full((n_rows,), n_tok, jnp.int32).at[dest].set((order // TOP_K).astype(jnp.int32))
    row_w = jnp.zeros((n_rows,), jnp.float32).at[dest].set(top_w.reshape(n_assign)[order])
    block_e = jnp.minimum(jnp.searchsorted(pad_end, jnp.arange(n_blocks) * MOE_BLOCK, side='right'),
                          N_EXPERTS - 1)
    h_pad = jnp.concatenate([h, jnp.zeros((1, h.shape[1]), h.dtype)], axis=0)

    def expert_block(args):
        tok, e = args
        xb = h_pad[tok]
        gate = jnp.minimum(xb @ w_gate[e] + b_gate[e], SWIGLU_LIMIT)
        up = jnp.clip(xb @ w_up[e] + b_up[e], -SWIGLU_LIMIT, SWIGLU_LIMIT)
        return ((up + 1.0) * gate * jax.nn.sigmoid(SWIGLU_ALPHA * gate)) @ w_down[e] + b_down[e]

    y = lax.map(expert_block, (row_tok.reshape(n_blocks, MOE_BLOCK), block_e))
    y = y.reshape(n_rows, -1) * row_w[:, None].astype(h.dtype)
    return jnp.zeros_like(h_pad).at[row_tok].add(y)[:n_tok]


def setup_inputs(seed: int = 0) -> dict:
    key = jax.random.key(seed)
    ks = iter(jax.random.split(key, 29))
    n_pages = PAST_LEN // PAGE_SIZE
    n_pool = (DEC_BATCH * n_pages * 5) // 4
    win_buf = min(WINDOW, PAST_LEN)

    def nrm(shape, scale=1.0):
        return jax.random.normal(next(ks), shape, jnp.float32) * scale

    def gain(shape):
        return 1.0 + nrm(shape, 0.02)

    kv_row = (2, NSA_KV_HEADS, NSA_HD)
    x_prompt = nrm((BATCH, SEQ, D_MODEL))
    x_sample = nrm((DEC_BATCH, DEC_SEQ, D_MODEL))
    c_prompt = nrm((BATCH, D_MODEL))
    c_sample = nrm((DEC_BATCH, D_MODEL))
    cache_cmp = nrm((DEPTH, n_pool, PAGE_SIZE) + kv_row)
    cache_sel = nrm((DEPTH, n_pool, PAGE_SIZE) + kv_row)
    state_win = nrm((DEPTH, DEC_BATCH, win_buf) + kv_row)
    state_gla = nrm((DEPTH, DEC_BATCH, GLA_HEADS, GLA_DK, GLA_DV))
    page_table = jax.random.permutation(next(ks), n_pool)[:DEC_BATCH * n_pages]
    page_table = page_table.reshape(DEC_BATCH, n_pages).astype(jnp.int32)
    return {
        'x_prompt': x_prompt, 'x_sample': x_sample, 'c_prompt': c_prompt, 'c_sample': c_sample,
        'cache_cmp': cache_cmp, 'cache_sel': cache_sel, 'state_win': state_win, 'state_gla': state_gla,
        'page_table': page_table,
        'w_ada': nrm((DEPTH, D_MODEL, 6 * D_MODEL), 0.5 * D_MODEL ** -0.5),
        'b_ada': nrm((DEPTH, 6 * D_MODEL), 0.02),
        'g_mix': gain((DEPTH, D_MODEL)),
        'g_ffn': gain((DEPTH, D_MODEL)),
        'w_in': nrm((DEPTH, D_MODEL, D_IN), D_MODEL ** -0.5),
        'w_a2': nrm((DEPTH, GLA_GATE_RANK, GLA_QK_WIDTH), GLA_GATE_RANK ** -0.5),
        'b_a': nrm((DEPTH, GLA_QK_WIDTH), 0.1),
        'g_gla': gain((DEPTH, GLA_DV)),
        'phi_pe': nrm((DEPTH, 2, CMP_LEN, NSA_HD), 0.02),
        'phi_w': nrm((DEPTH, 2, CMP_LEN * NSA_HD, NSA_HD), (CMP_LEN * NSA_HD) ** -0.5),
        'w_out': nrm((DEPTH, MIX_WIDTH, D_MODEL), MIX_WIDTH ** -0.5),
        'w_router': nrm((DEPTH, D_MODEL, N_EXPERTS), D_MODEL ** -0.5),
        'b_router': nrm((DEPTH, N_EXPERTS), 0.01),
        'w_gate': nrm((DEPTH, N_EXPERTS, D_MODEL, D_EXPERT), D_MODEL ** -0.5),
        'b_gate': nrm((DEPTH, N_EXPERTS, D_EXPERT), 0.01),
        'w_up': nrm((DEPTH, N_EXPERTS, D_MODEL, D_EXPERT), D_MODEL ** -0.5),
        'b_up': nrm((DEPTH, N_EXPERTS, D_EXPERT), 0.01),
        'w_down': nrm((DEPTH, N_EXPERTS, D_EXPERT, D_MODEL), D_EXPERT ** -0.5),
        'b_down': nrm((DEPTH, N_EXPERTS, D_MODEL), 0.01),
        'g_final': gain((D_MODEL,)),
    }


def reference(x_prompt, x_sample, c_prompt, c_sample, cache_cmp, cache_sel, state_win, state_gla, page_table,
              w_ada, b_ada, g_mix, g_ffn, w_in, w_a2, b_a, g_gla, phi_pe, phi_w, w_out,
              w_router, b_router, w_gate, b_gate, w_up, b_up, w_down, b_down, g_final):
    xp, xs = x_prompt, x_sample
    bp, sp = xp.shape[:2]
    bs, ts = xs.shape[:2]
    win_buf = state_win.shape[2]
    cmp_p, sel_p, win_p, gla_p = [], [], [], []
    cmp_s, sel_s, win_s, gla_s = [], [], [], []
    for l in range(DEPTH):
        mod_p = adaln(c_prompt, w_ada[l], b_ada[l])
        mod_s = adaln(c_sample, w_ada[l], b_ada[l])

        def nsa_prompt(q, gates, kc, ks_, kw, l=l):
            return lax.map(lambda a: nsa_prompt_seq(*a, phi_pe[l], phi_w[l]), (q, gates, kc, ks_, kw))

        def nsa_sample(q, gates, kc, ks_, kw, l=l):
            win_rows = jnp.concatenate([state_win[l], kw.astype(state_win.dtype)], axis=1)
            return lax.map(lambda a: nsa_sample_seq(*a, cache_cmp[l], cache_sel[l], phi_pe[l], phi_w[l]),
                           (q, gates, kc, ks_, win_rows, page_table))

        hp = modulate(xp, g_mix[l], mod_p[:, 0], mod_p[:, 1])
        yp, gp, kc_p, ks_p, kw_p = hybrid_mixer(hp, jnp.zeros((bp, GLA_HEADS, GLA_DK, GLA_DV), jnp.float32),
                                                nsa_prompt, w_in[l], w_a2[l], b_a[l], g_gla[l], w_out[l])
        xp = xp + mod_p[:, 2] * yp
        hs = modulate(xs, g_mix[l], mod_s[:, 0], mod_s[:, 1])
        ys, gs, kc_s, ks_s, kw_s = hybrid_mixer(hs, state_gla[l], nsa_sample,
                                                w_in[l], w_a2[l], b_a[l], g_gla[l], w_out[l])
        xs = xs + mod_s[:, 2] * ys
        cmp_p.append(kc_p)
        sel_p.append(ks_p)
        win_p.append(jnp.pad(kw_p, ((0, 0), (win_buf, 0), (0, 0), (0, 0), (0, 0)))[:, -win_buf:])
        gla_p.append(gp)
        cmp_s.append(kc_s)
        sel_s.append(ks_s)
        win_s.append(jnp.concatenate([state_win[l], kw_s.astype(state_win.dtype)], axis=1)[:, -win_buf:])
        gla_s.append(gs)

        hp2 = modulate(xp, g_ffn[l], mod_p[:, 3], mod_p[:, 4])
        hs2 = modulate(xs, g_ffn[l], mod_s[:, 3], mod_s[:, 4])
        f = moe_ffn(jnp.concatenate([hp2.reshape(-1, D_MODEL), hs2.reshape(-1, D_MODEL)], axis=0),
                    w_router[l], b_router[l], w_gate[l], b_gate[l], w_up[l], b_up[l], w_down[l], b_down[l])
        xp = xp + mod_p[:, 5] * f[:bp * sp].reshape(bp, sp, D_MODEL)
        xs = xs + mod_s[:, 5] * f[bp * sp:].reshape(bs, ts, D_MODEL)

    y_prompt = rmsnorm(xp, g_final)
    y_sample = rmsnorm(xs, g_final)
    return (y_prompt, y_sample,
            jnp.stack(cmp_p), jnp.stack(sel_p), jnp.stack(win_p), jnp.stack(gla_p),
            jnp.stack(cmp_s), jnp.stack(sel_s), jnp.stack(win_s), jnp.stack(gla_s))
```

```python
import functools
import math

import numpy as np
import jax
import jax.numpy as jnp
from jax import lax
from jax.experimental import pallas as pl
from jax.experimental.pallas import tpu as pltpu

f32, bf16, i32 = jnp.float32, jnp.bfloat16, jnp.int32
SDS = jax.ShapeDtypeStruct

D_MODEL = 1024
GLA_HEADS, GLA_DK, GLA_DV, GLA_GATE_RANK, GLA_TAU, GLA_CHUNK = 4, 64, 128, 16, 16.0, 64
NSA_HEADS, NSA_KV_HEADS, NSA_HD = 8, 2, 64
NSA_GROUP = NSA_HEADS // NSA_KV_HEADS
CMP_LEN, CMP_STRIDE, SEL_BLOCK, N_SEL, WINDOW, NSA_QBLOCK = 32, 16, 64, 8, 512, 128
N_EXPERTS, TOP_K, D_EXPERT = 32, 4, 1024
SWIGLU_LIMIT, SWIGLU_ALPHA = 7.0, 1.702
RMS_EPS, NEG_INF, FORCE_SCORE = 1e-6, -1e30, 1e4
PAGE_SIZE = 128

GLA_QK_WIDTH = GLA_HEADS * GLA_DK
GLA_WIDTH = GLA_HEADS * GLA_DV
NSA_WIDTH = NSA_HEADS * NSA_HD
NSA_KV_WIDTH = 2 * NSA_KV_HEADS * NSA_HD
IN_SPLITS = (GLA_QK_WIDTH, GLA_QK_WIDTH, GLA_WIDTH, GLA_WIDTH, GLA_GATE_RANK,
             NSA_WIDTH, NSA_KV_WIDTH, NSA_KV_WIDTH, NSA_KV_WIDTH, 3 * NSA_HEADS)
MIX_WIDTH = GLA_WIDTH + NSA_WIDTH

LANES = 128
SUBLANES = 8
VMEM_LIMIT_BYTES = 48 * 1024 * 1024

_IN_COLS = (("gq", 0, 256, f32), ("gk", 256, 256, f32), ("gv", 512, 512, bf16), ("gr", 1024, 512, f32),
            ("nq", 1536, 512, bf16), ("kvc", 2048, 256, f32), ("kvs", 2304, 256, f32), ("kvw", 2560, 256, f32),
            ("ga", 2816, LANES, f32), ("ng", 2944, LANES, f32))
_IN_WIDTH = 3072
FEAT = NSA_KV_HEADS * NSA_HD
SUBROW = CMP_STRIDE * NSA_KV_WIDTH
MOE_TM = 512


def _cp(*sem):
    return pltpu.CompilerParams(dimension_semantics=sem, vmem_limit_bytes=VMEM_LIMIT_BYTES)


def _dot(a, b):
    return jnp.dot(a, b, preferred_element_type=f32)


def _dot_nt(a, b):
    return lax.dot_general(a, b, (((1,), (1,)), ((), ())), preferred_element_type=f32)


def _dot_tn(a, b):
    return lax.dot_general(a, b, (((0,), (0,)), ((), ())), preferred_element_type=f32)


def _rms(x, g):
    return x * lax.rsqrt(jnp.mean(x * x, axis=-1, keepdims=True) + RMS_EPS) * g


def _silu(x):
    return x * jax.nn.sigmoid(x)


def _adaln_kernel(c_ref, w_ref, b_ref, o_ref):
    s = _silu(c_ref[...]).astype(bf16)
    o_ref[...] = _dot(s, w_ref[...].astype(bf16)) + b_ref[...]


def _adaln(c_all, w_ada, b_ada):
    nb, n = c_all.shape[0], w_ada.shape[1]
    tn = 1024
    return pl.pallas_call(
        _adaln_kernel, grid=(n // tn,),
        in_specs=[pl.BlockSpec((nb, D_MODEL), lambda j: (0, 0)),
                  pl.BlockSpec((D_MODEL, tn), lambda j: (0, j)),
                  pl.BlockSpec((1, tn), lambda j: (0, j))],
        out_specs=pl.BlockSpec((nb, tn), lambda j: (0, j)),
        out_shape=SDS((nb, n), f32), compiler_params=_cp("arbitrary"), name="adaln",
    )(c_all, w_ada, b_ada.reshape(1, n))


class _Mod:
    def __init__(self, arr, per_token, tm, tiles_per_seq):
        self.arr, self.per_token, self.tm, self.tps = arr, per_token, tm, tiles_per_seq

    def spec(self, k):
        if self.per_token:
            return pl.BlockSpec((None, self.tm, D_MODEL), lambda i: (k, i, 0))
        tps = self.tps
        return pl.BlockSpec((None, None, 1, D_MODEL), lambda i: (i // tps, k, 0, 0))


def _inproj_kernel(x_ref, sh_ref, sc_ref, g_ref, w_ref, *outs):
    y = _rms(x_ref[...], g_ref[...])
    h = (y * (1.0 + sc_ref[...]) + sh_ref[...]).astype(bf16)
    for o_ref, (_, c0, w, _) in zip(outs, _IN_COLS):
        o_ref[...] = _dot(h, w_ref[:, c0:c0 + w]).astype(o_ref.dtype)


def _inproj(x2d, mod, g_mix, w_in_p):
    t, tm = x2d.shape[0], mod.tm
    row = lambda i: (i, 0)
    return pl.pallas_call(
        _inproj_kernel, grid=(t // tm,),
        in_specs=[pl.BlockSpec((tm, D_MODEL), row), mod.spec(0), mod.spec(1),
                  pl.BlockSpec((1, D_MODEL), lambda i: (0, 0)),
                  pl.BlockSpec((D_MODEL, _IN_WIDTH), lambda i: (0, 0))],
        out_specs=[pl.BlockSpec((tm, w), row) for (_, _, w, _) in _IN_COLS],
        out_shape=[SDS((t, w), dt) for (_, _, w, dt) in _IN_COLS],
        compiler_params=_cp("arbitrary"), name="inproj",
    )(x2d, mod.arr, mod.arr, g_mix.reshape(1, D_MODEL), w_in_p)


def _gla_kernel(q_ref, k_ref, v_ref, r_ref, a_ref, s0_ref, wa2_ref, ba_ref, gg_ref, o_ref, st_ref, *, chunk, n_chunks):
    @pl.when(pl.program_id(1) == 0)
    def _init():
        st_ref[...] = s0_ref[...]

    ri = lax.broadcasted_iota(i32, (chunk, chunk), 0)
    ci = lax.broadcasted_iota(i32, (chunk, chunk), 1)
    causal = ri >= ci
    tril = jnp.where(causal, 1.0, 0.0).astype(bf16)

    def body(c, carry):
        rows = pl.ds(pl.multiple_of(c * chunk, chunk), chunk)
        a_low = a_ref[rows, :][:, :GLA_GATE_RANK].astype(bf16)
        z = _dot(a_low, wa2_ref[...]) + ba_ref[...]
        log_a = (jnp.minimum(z, 0.0) - jnp.log1p(jnp.exp(-jnp.abs(z)))) * (1.0 / GLA_TAU)
        hi = log_a.astype(bf16)
        lo = (log_a - hi.astype(f32)).astype(bf16)
        cum = _dot(tril, hi) + _dot(tril, lo)
        last = cum[chunk - 1:chunk, :]
        q = q_ref[rows, :] * (GLA_DK ** -0.5)
        k = k_ref[rows, :]
        qd = (q * jnp.exp(cum)).astype(bf16)
        ki = (k * jnp.exp(-cum)).astype(bf16)
        ke = (k * jnp.exp(last - cum)).astype(bf16)
        dec = jnp.exp(last)
        v = v_ref[rows, :]
        r = r_ref[rows, :]
        for h in range(GLA_HEADS):
            sk = slice(h * GLA_DK, (h + 1) * GLA_DK)
            sv = slice(h * GLA_DV, (h + 1) * GLA_DV)
            qh, kih, keh, vh = qd[:, sk], ki[:, sk], ke[:, sk], v[:, sv]
            att = jnp.where(causal, _dot_nt(qh, kih), 0.0)
            st = st_ref[h]
            o = _dot(att.astype(bf16), vh) + _dot_nt(qh, st.astype(bf16))
            st_ref[h] = st * dec[:, sk] + _dot_tn(vh, keh)
            o_ref[rows, sv] = (_rms(o, gg_ref[...]) * _silu(r[:, sv])).astype(o_ref.dtype)
        return carry

    lax.fori_loop(0, n_chunks, body, 0)


def _gla(gq, gk, gv, gr, ga, s0t, w_a2, b_a, g_gla, b_sz, seq_len):
    chunk = math.gcd(seq_len, GLA_CHUNK)
    tb = min(seq_len, 512)
    n_chunks = tb // chunk
    r3 = lambda a: a.reshape(b_sz, seq_len, a.shape[-1])
    tok = lambda w: pl.BlockSpec((None, tb, w), lambda b, j: (b, j, 0))
    st_spec = pl.BlockSpec((None, GLA_HEADS, GLA_DV, GLA_DK), lambda b, j: (b, 0, 0, 0))
    const = lambda s: pl.BlockSpec(s, lambda b, j: (0, 0))
    return pl.pallas_call(
        functools.partial(_gla_kernel, chunk=chunk, n_chunks=n_chunks),
        grid=(b_sz, seq_len // tb),
        in_specs=[tok(GLA_QK_WIDTH), tok(GLA_QK_WIDTH), tok(GLA_WIDTH), tok(GLA_WIDTH), tok(LANES), st_spec,
                  const((GLA_GATE_RANK, GLA_QK_WIDTH)), const((1, GLA_QK_WIDTH)), const((1, GLA_DV))],
        out_specs=[tok(GLA_WIDTH), st_spec],
        out_shape=[SDS((b_sz, seq_len, GLA_WIDTH), bf16), SDS((b_sz, GLA_HEADS, GLA_DV, GLA_DK), f32)],
        compiler_params=_cp("arbitrary", "arbitrary"), name="gla",
    )(r3(gq), r3(gk), r3(gv), r3(gr), r3(ga), s0t, w_a2.astype(bf16), b_a.reshape(1, -1), g_gla.reshape(1, -1))


def _np_psel():
    p = np.zeros((NSA_HEADS, FEAT, NSA_WIDTH), np.float32)
    for h in range(NSA_KV_HEADS):
        for g in range(NSA_GROUP):
            for d in range(NSA_HD):
                p[h * NSA_GROUP + g, h * NSA_HD + d, (h * NSA_GROUP + g) * NSA_HD + d] = 1.0
    return p


def _np_cmp_to_sel_t(n_cmp, n_sel, rows, cols):
    i0 = np.arange(n_cmp)[None, :] * CMP_STRIDE
    j0 = np.arange(n_sel)[:, None] * SEL_BLOCK
    m = np.zeros((rows, cols), np.float32)
    m[:n_sel, :n_cmp] = ((i0 < j0 + SEL_BLOCK) & (i0 + CMP_LEN > j0)).astype(np.float32)
    return m


def _prep_compress(phi_pe, phi_w):
    w = phi_w.reshape(2, CMP_LEN, NSA_HD, NSA_HD)
    wb = jnp.einsum('cldo,ce,hk->lchdeko', w, jnp.eye(2, dtype=f32), jnp.eye(NSA_KV_HEADS, dtype=f32))
    wb = wb.reshape(CMP_LEN, NSA_KV_WIDTH, NSA_KV_WIDTH)
    w_top = wb[:CMP_STRIDE].reshape(SUBROW, NSA_KV_WIDTH).astype(bf16)
    w_bot = wb[CMP_STRIDE:].reshape(SUBROW, NSA_KV_WIDTH).astype(bf16)
    pe = jnp.broadcast_to(phi_pe.transpose(1, 0, 2)[:, :, None, :], (CMP_LEN, 2, NSA_KV_HEADS, NSA_HD))
    pe = pe.reshape(CMP_LEN, NSA_KV_WIDTH)
    return pe[:CMP_STRIDE].reshape(1, SUBROW), pe[CMP_STRIDE:].reshape(1, SUBROW), w_top, w_bot


def _compress(x, pet, peb, wt, wb):
    a = _dot((x + pet).astype(bf16), wt)
    b = _dot((x + peb).astype(bf16), wb)
    return a + pltpu.roll(b, x.shape[0] - 1, 0)


def _col_softmax(s, valid):
    s = jnp.where(valid, s, NEG_INF)
    m = jnp.max(s, axis=0, keepdims=True)
    p = jnp.where(valid, jnp.exp(s - m), 0.0)
    l = jnp.sum(p, axis=0, keepdims=True)
    return (p * (1.0 / jnp.maximum(l, 1e-30))).astype(bf16)


def _select_blocks(imp, t_col, n_sel):
    blk = lax.broadcasted_iota(i32, imp.shape, 0)
    cur = lax.shift_right_logical(t_col, int(math.log2(SEL_BLOCK)))
    forced = (blk == 0) | (blk == cur) | (blk == cur - 1)
    future = blk * SEL_BLOCK > t_col
    score = jnp.where(future, NEG_INF, jnp.where(forced, FORCE_SCORE, imp))
    score = jnp.where(blk < n_sel, score, -jnp.inf)
    sel = jnp.zeros(imp.shape, f32)
    for _ in range(min(N_SEL, n_sel)):
        mx = jnp.max(score, axis=0, keepdims=True)
        idx = jnp.min(jnp.where(score == mx, blk, 2 ** 30), axis=0, keepdims=True)
        pick = blk == idx
        sel = jnp.where(pick, 1.0, sel)
        score = jnp.where(pick, -jnp.inf, score)
    return sel


def _flash_step(s, valid, vt, m_ref, l_ref, acc_ref):
    s = jnp.where(valid, s, NEG_INF)
    m_old = m_ref[...]
    m_new = jnp.maximum(m_old, jnp.max(s, axis=0, keepdims=True))
    alpha = jnp.exp(m_old - m_new)
    p = jnp.where(valid, jnp.exp(s - m_new), 0.0)
    l_ref[...] = alpha * l_ref[...] + jnp.sum(p, axis=0, keepdims=True)
    acc_ref[...] = alpha * acc_ref[...] + _dot(vt, p.astype(bf16))
    m_ref[...] = m_new


def _cmp_prompt_kernel(x_ref, pet_ref, peb_ref, wt_ref, wb_ref, kc_ref, vct_ref):
    cmp = _compress(x_ref[...], pet_ref[...], peb_ref[...], wt_ref[...], wb_ref[...])
    kc_ref[...] = cmp[:, :FEAT].astype(bf16)
    vct_ref[...] = cmp[:, FEAT:].T.astype(bf16)


def _cmp_prompt(kvc3, cw):
    b_sz, seq_len, _ = kvc3.shape
    r = seq_len // CMP_STRIDE
    x = kvc3.reshape(b_sz, r, SUBROW)
    const = lambda s: pl.BlockSpec(s, lambda b: (0, 0))
    return pl.pallas_call(
        _cmp_prompt_kernel, grid=(b_sz,),
        in_specs=[pl.BlockSpec((None, r, SUBROW), lambda b: (b, 0, 0)), const((1, SUBROW)), const((1, SUBROW)),
                  const((SUBROW, NSA_KV_WIDTH)), const((SUBROW, NSA_KV_WIDTH))],
        out_specs=[pl.BlockSpec((None, r, FEAT), lambda b: (b, 0, 0)), pl.BlockSpec((None, FEAT, r), lambda b: (b, 0, 0))],
        out_shape=[SDS((b_sz, r, FEAT), bf16), SDS((b_sz, FEAT, r), bf16)],
        compiler_params=_cp("arbitrary"), name="cmp_prompt",
    )(x, *cw)


def _nsa_prompt_kernel(q_ref, gate_ref, kc_ref, vct_ref, ksk_ref, ksvt_ref, kwk_ref, kwvt_ref, psel_ref, mt_ref,
                       o_ref, smt_ref, m_ref, l_ref, acc_ref, *, qb, seq_len, tk, wk):
    i = pl.program_id(1)
    start = i * qb
    nc = NSA_HEADS * qb
    n_sel = seq_len // SEL_BLOCK
    n_cmp = seq_len // CMP_STRIDE - CMP_LEN // CMP_STRIDE + 1
    q = q_ref[...]
    qzt = jnp.concatenate([_dot_nt(psel_ref[hg], q) for hg in range(NSA_HEADS)], axis=1)
    qzt = (qzt * (NSA_HD ** -0.5)).astype(bf16)
    t_col = start + lax.broadcasted_iota(i32, (1, nc), 1) % qb

    s = _dot(kc_ref[...], qzt)
    n_io = lax.broadcasted_iota(i32, s.shape, 0)
    valid = (n_io * CMP_STRIDE + (CMP_LEN - 1) <= t_col) & (n_io < n_cmp)
    pc = _col_softmax(s, valid)
    o_c = _dot(vct_ref[...], pc)
    imp_all = _dot(mt_ref[...], pc)
    imp = []
    for h in range(NSA_KV_HEADS):
        parts = [imp_all[:, (h * NSA_GROUP + g) * qb:(h * NSA_GROUP + g + 1) * qb] for g in range(NSA_GROUP)]
        tot = parts[0]
        for part in parts[1:]:
            tot = tot + part
        imp.extend([tot] * NSA_GROUP)
    smt_ref[...] = _select_blocks(jnp.concatenate(imp, axis=1), t_col, n_sel)

    m_ref[...] = jnp.full(m_ref.shape, NEG_INF, f32)
    l_ref[...] = jnp.zeros(l_ref.shape, f32)
    acc_ref[...] = jnp.zeros(acc_ref.shape, f32)
    bpt = tk // SEL_BLOCK

    def body(jt, carry):
        k0 = pl.multiple_of(jt * tk, tk)
        s = _dot(ksk_ref[pl.ds(k0, tk), :], qzt)
        pos = k0 + lax.broadcasted_iota(i32, s.shape, 0)
        selm = jnp.concatenate(
            [jnp.broadcast_to(smt_ref[pl.ds(jt * bpt + b, 1), :], (SEL_BLOCK, nc)) for b in range(bpt)], axis=0)
        valid = (selm > 0.5) & (pos <= t_col)
        _flash_step(s, valid, ksvt_ref[:, pl.ds(k0, tk)], m_ref, l_ref, acc_ref)
        return carry

    lax.fori_loop(0, (start + qb + tk - 1) // tk, body, 0)
    o_s = acc_ref[...] * (1.0 / jnp.maximum(l_ref[...], 1e-30))

    ks = pl.multiple_of(jnp.clip(start - WINDOW, 0, seq_len - wk), qb)
    s = _dot(kwk_ref[pl.ds(ks, wk), :], qzt)
    dpos = t_col - (ks + lax.broadcasted_iota(i32, s.shape, 0))
    pw = _col_softmax(s, (dpos >= 0) & (dpos < WINDOW))
    o_w = _dot(kwvt_ref[:, pl.ds(ks, wk)], pw)

    gt = jax.nn.sigmoid(gate_ref[...]).T
    out = jnp.zeros((qb, NSA_WIDTH), f32)
    for hg in range(NSA_HEADS):
        cs = slice(hg * qb, (hg + 1) * qb)
        mix = gt[3 * hg:3 * hg + 1] * o_c[:, cs] + gt[3 * hg + 1:3 * hg + 2] * o_s[:, cs] + gt[3 * hg + 2:3 * hg + 3] * o_w[:, cs]
        out = out + _dot(mix.T.astype(bf16), psel_ref[hg])
    o_ref[...] = out.astype(o_ref.dtype)


def _nsa_prompt(nq3, ng3, kc, vct, ksk, ksvt, kwk, kwvt):
    b_sz, seq_len, _ = nq3.shape
    qb = math.gcd(seq_len, NSA_QBLOCK)
    tk = 256 if seq_len % 256 == 0 else SEL_BLOCK
    wk = min(WINDOW + qb, seq_len)
    r = kc.shape[1]
    n_sel = seq_len // SEL_BLOCK
    n_cmp = seq_len // CMP_STRIDE - CMP_LEN // CMP_STRIDE + 1
    psel = jnp.asarray(_np_psel(), bf16)
    mt = jnp.asarray(_np_cmp_to_sel_t(n_cmp, n_sel, n_sel, r), bf16)
    nc = NSA_HEADS * qb
    per_seq = lambda s: pl.BlockSpec((None,) + s, lambda b, i: (b, 0, 0))
    return pl.pallas_call(
        functools.partial(_nsa_prompt_kernel, qb=qb, seq_len=seq_len, tk=tk, wk=wk),
        grid=(b_sz, seq_len // qb),
        in_specs=[pl.BlockSpec((None, qb, NSA_WIDTH), lambda b, i: (b, i, 0)),
                  pl.BlockSpec((None, qb, LANES), lambda b, i: (b, i, 0)),
                  per_seq((r, FEAT)), per_seq((FEAT, r)),
                  per_seq((seq_len, FEAT)), per_seq((FEAT, seq_len)),
                  per_seq((seq_len, FEAT)), per_seq((FEAT, seq_len)),
                  pl.BlockSpec((NSA_HEADS, FEAT, NSA_WIDTH), lambda b, i: (0, 0, 0)),
                  pl.BlockSpec((n_sel, r), lambda b, i: (0, 0))],
        out_specs=pl.BlockSpec((None, qb, NSA_WIDTH), lambda b, i: (b, i, 0)),
        out_shape=SDS((b_sz, seq_len, NSA_WIDTH), bf16),
        scratch_shapes=[pltpu.VMEM((n_sel, nc), f32), pltpu.VMEM((1, nc), f32), pltpu.VMEM((1, nc), f32),
                        pltpu.VMEM((FEAT, nc), f32)],
        compiler_params=_cp("arbitrary", "arbitrary"), name="nsa_prompt",
    )(nq3, ng3, kc, vct, ksk, ksvt, kwk, kwvt, psel, mt)


NCS = LANES


def _nsa_s1_kernel(pt_ref, *refs, n_pages, pps, ts, win_buf):
    pages = refs[:pps]
    (newc_ref, q_ref, gate_ref, swin_ref, kwn_ref, pet_ref, peb_ref, wt_ref, wb_ref, psel_ref, mt_ref, gsum_ref,
     qzt_ref, part_ref, smt_ref, x_ref, kwin_ref) = refs[pps:]
    j = pl.program_id(1)
    sub_per_page = PAGE_SIZE // CMP_STRIDE
    rx = x_ref.shape[0]
    past = n_pages * PAGE_SIZE

    @pl.when(j == 0)
    def _fill_tail():
        x_ref[pl.ds(n_pages * sub_per_page, rx - n_pages * sub_per_page), :] = jnp.zeros(
            (rx - n_pages * sub_per_page, SUBROW), f32)
        x_ref[pl.ds(n_pages * sub_per_page, 1), :] = newc_ref[...]

    for k in range(pps):
        r0 = pl.multiple_of((j * pps + k) * sub_per_page, sub_per_page)
        x_ref[pl.ds(r0, sub_per_page), :] = pages[k][...]

    @pl.when(j == pl.num_programs(1) - 1)
    def _finish():
        n_rows = NSA_HEADS * ts
        qz = jnp.concatenate([_dot_nt(q_ref[...], psel_ref[hg]) for hg in range(NSA_HEADS)]
                             + [jnp.zeros((NCS - n_rows, FEAT), f32)], axis=0)
        qzt = (qz.T * (NSA_HD ** -0.5)).astype(bf16)
        qzt_ref[...] = qzt
        t_col = past + lax.broadcasted_iota(i32, (1, NCS), 1) % ts
        total = past + ts
        n_sel = -(-total // SEL_BLOCK)
        n_cmp = n_sel * SEL_BLOCK // CMP_STRIDE - CMP_LEN // CMP_STRIDE + 1

        cmp = _compress(x_ref[...], pet_ref[...], peb_ref[...], wt_ref[...], wb_ref[...])
        s = _dot(cmp[:, :FEAT].astype(bf16), qzt)
        n_io = lax.broadcasted_iota(i32, s.shape, 0)
        valid = (n_io * CMP_STRIDE + (CMP_LEN - 1) <= t_col) & (n_io < n_cmp)
        pc = _col_softmax(s, valid)
        o_c = _dot(cmp[:, FEAT:].T.astype(bf16), pc)
        imp_all = _dot(mt_ref[...], pc)
        hi = imp_all.astype(bf16)
        r1 = imp_all - hi.astype(f32)
        mid = r1.astype(bf16)
        lo = (r1 - mid.astype(f32)).astype(bf16)
        gs = gsum_ref[...]
        imp = _dot(hi, gs) + _dot(mid, gs) + _dot(lo, gs)
        smt_ref[...] = _select_blocks(imp, t_col, n_sel)

        kwin_ref[pl.ds(0, win_buf), :] = swin_ref[...]
        kwin_ref[pl.ds(win_buf, ts), :] = kwn_ref[...]
        kw = kwin_ref[...]
        s = _dot(kw[:, :FEAT].astype(bf16), qzt)
        w_pos = total - (win_buf + ts) + lax.broadcasted_iota(i32, s.shape, 0)
        dpos = t_col - w_pos
        pw = _col_softmax(s, (dpos >= 0) & (dpos < WINDOW) & (w_pos >= 0))
        o_w = _dot(kw[:, FEAT:].T.astype(bf16), pw)

        sig = jax.nn.sigmoid(gate_ref[...])
        oc_r, ow_r = o_c.T, o_w.T
        part_ref[...] = jnp.concatenate(
            [sig[:, 3 * hg:3 * hg + 1] * oc_r[hg * ts:(hg + 1) * ts] + sig[:, 3 * hg + 2:3 * hg + 3] * ow_r[hg * ts:(hg + 1) * ts]
             for hg in range(NSA_HEADS)], axis=0)


def _nsa_s2_kernel(pt_ref, *refs, n_pages, pps, ts):
    pages = refs[:pps]
    news_ref, qzt_ref, smt_ref, part_ref, gate_ref, psel_ref, o_ref, m_ref, l_ref, acc_ref = refs[pps:]
    j = pl.program_id(1)
    past = n_pages * PAGE_SIZE
    qzt = qzt_ref[...]
    t_col = past + lax.broadcasted_iota(i32, (1, NCS), 1) % ts
    bpp = PAGE_SIZE // SEL_BLOCK

    @pl.when(j == 0)
    def _init():
        m_ref[...] = jnp.full(m_ref.shape, NEG_INF, f32)
        l_ref[...] = jnp.zeros(l_ref.shape, f32)
        acc_ref[...] = jnp.zeros(acc_ref.shape, f32)

    def tile(kv, first_block, n_blocks, pos0):
        s = _dot(kv[:, :FEAT].astype(bf16), qzt)
        pos = pos0 + lax.broadcasted_iota(i32, s.shape, 0)
        selm = jnp.concatenate(
            [jnp.broadcast_to(smt_ref[pl.ds(first_block + b, 1), :], (SEL_BLOCK, NCS)) for b in range(n_blocks)], axis=0)
        valid = (selm > 0.5) & (pos <= t_col)
        _flash_step(s, valid, kv[:, FEAT:].T.astype(bf16), m_ref, l_ref, acc_ref)

    for k in range(pps):
        page = j * pps + k
        tile(pages[k][...], page * bpp, bpp, page * PAGE_SIZE)

    @pl.when(j == pl.num_programs(1) - 1)
    def _finish():
        new = jnp.concatenate([news_ref[...], jnp.zeros((SEL_BLOCK - ts, NSA_KV_WIDTH), f32)], axis=0)
        tile(new, n_pages * bpp, 1, past)
        o_s = (acc_ref[...] * (1.0 / jnp.maximum(l_ref[...], 1e-30))).T
        sig = jax.nn.sigmoid(gate_ref[...])
        part = part_ref[...]
        out = jnp.zeros((ts, NSA_WIDTH), f32)
        for hg in range(NSA_HEADS):
            rs = slice(hg * ts, (hg + 1) * ts)
            mix = part[rs] + sig[:, 3 * hg + 1:3 * hg + 2] * o_s[rs]
            out = out + _dot(mix.astype(bf16), psel_ref[hg])
        o_ref[...] = out.astype(o_ref.dtype)


def _nsa_sample(nq3, ng3, kvc3, kvs3, kvw3, cache_c, cache_s, state_win_l, page_table, cw):
    bs, ts, _ = nq3.shape
    n_pages = page_table.shape[1]
    n_pool = cache_c.shape[0]
    win_buf = state_win_l.shape[1]
    assert ts == SUBLANES and PAGE_SIZE % SEL_BLOCK == 0 and NSA_HEADS * ts <= NCS
    pps = math.gcd(n_pages, 8)
    steps = n_pages // pps
    sub_per_page = PAGE_SIZE // CMP_STRIDE
    rx = n_pages * sub_per_page + SUBLANES
    total = n_pages * PAGE_SIZE + ts
    n_sel = -(-total // SEL_BLOCK)
    n_cmp = n_sel * SEL_BLOCK // CMP_STRIDE - CMP_LEN // CMP_STRIDE + 1
    rb = -(-n_sel // SUBLANES) * SUBLANES
    psel = jnp.asarray(_np_psel(), bf16)
    mt = jnp.asarray(_np_cmp_to_sel_t(n_cmp, n_sel, rb, rx), bf16)
    col = np.arange(NCS)
    gsum = ((col[:, None] // (NSA_GROUP * ts) == col[None, :] // (NSA_GROUP * ts)) & (col[:, None] % ts == col[None, :] % ts))
    gsum = jnp.asarray(gsum.astype(np.float32), bf16)
    pt_flat = page_table.reshape(-1).astype(i32)
    newc = jnp.pad(kvc3.reshape(bs, 1, ts * NSA_KV_WIDTH), ((0, 0), (0, 0), (0, SUBROW - ts * NSA_KV_WIDTH)))
    pet, peb, wt, wb = cw

    def page_spec(shape, k):
        return pl.BlockSpec((None,) + shape, lambda b, j, pt, k=k: (pt[b * n_pages + j * pps + k], 0, 0))

    per_seq = lambda s: pl.BlockSpec((None,) + s, lambda b, j, pt: (b, 0, 0))
    const = lambda s: pl.BlockSpec(s, lambda b, j, pt: (0,) * len(s))

    qzt, part, smt = pl.pallas_call(
        functools.partial(_nsa_s1_kernel, n_pages=n_pages, pps=pps, ts=ts, win_buf=win_buf),
        grid_spec=pltpu.PrefetchScalarGridSpec(
            num_scalar_prefetch=1, grid=(bs, steps),
            in_specs=[page_spec((sub_per_page, SUBROW), k) for k in range(pps)]
            + [per_seq((1, SUBROW)), per_seq((ts, NSA_WIDTH)), per_seq((ts, LANES)), per_seq((win_buf, NSA_KV_WIDTH)),
               per_seq((ts, NSA_KV_WIDTH)), const((1, SUBROW)), const((1, SUBROW)), const((SUBROW, NSA_KV_WIDTH)),
               const((SUBROW, NSA_KV_WIDTH)), const((NSA_HEADS, FEAT, NSA_WIDTH)), const((rb, rx)), const((NCS, NCS))],
            out_specs=[per_seq((FEAT, NCS)), per_seq((NSA_HEADS * ts, FEAT)), per_seq((rb, NCS))],
            scratch_shapes=[pltpu.VMEM((rx, SUBROW), f32), pltpu.VMEM((win_buf + ts, NSA_KV_WIDTH), f32)]),
        out_shape=[SDS((bs, FEAT, NCS), bf16), SDS((bs, NSA_HEADS * ts, FEAT), f32), SDS((bs, rb, NCS), f32)],
        compiler_params=_cp("arbitrary", "arbitrary"), name="nsa_sample_cmp",
    )(pt_flat, *([cache_c.reshape(n_pool, sub_per_page, SUBROW)] * pps), newc, nq3, ng3, state_win_l, kvw3,
      pet, peb, wt, wb, psel, mt, gsum)

    return pl.pallas_call(
        functools.partial(_nsa_s2_kernel, n_pages=n_pages, pps=pps, ts=ts),
        grid_spec=pltpu.PrefetchScalarGridSpec(
            num_scalar_prefetch=1, grid=(bs, steps),
            in_specs=[page_spec((PAGE_SIZE, NSA_KV_WIDTH), k) for k in range(pps)]
            + [per_seq((ts, NSA_KV_WIDTH)), per_seq((FEAT, NCS)), per_seq((rb, NCS)), per_seq((NSA_HEADS * ts, FEAT)),
               per_seq((ts, LANES)), const((NSA_HEADS, FEAT, NSA_WIDTH))],
            out_specs=per_seq((ts, NSA_WIDTH)),
            scratch_shapes=[pltpu.VMEM((1, NCS), f32), pltpu.VMEM((1, NCS), f32), pltpu.VMEM((FEAT, NCS), f32)]),
        out_shape=SDS((bs, ts, NSA_WIDTH), bf16),
        compiler_params=_cp("arbitrary", "arbitrary"), name="nsa_sample_sel",
    )(pt_flat, *([cache_s.reshape(n_pool, PAGE_SIZE, NSA_KV_WIDTH)] * pps), kvs3, qzt, smt, part, ng3, psel)


def _outproj_kernel(gla_ref, nsa_ref, x_ref, gm_ref, sh_ref, sc_ref, gf_ref, wo_ref, wrt_ref, br_ref,
                    x1_ref, h2_ref, idx_ref, tw_ref):
    y = _dot(gla_ref[...], wo_ref[:GLA_WIDTH, :]) + _dot(nsa_ref[...], wo_ref[GLA_WIDTH:, :])
    x1 = x_ref[...] + gm_ref[...] * y
    x1_ref[...] = x1
    h2 = (_rms(x1, gf_ref[...]) * (1.0 + sc_ref[...]) + sh_ref[...]).astype(bf16)
    h2_ref[...] = h2
    s = _dot_nt(wrt_ref[...], h2) + br_ref[...]
    e_io = lax.broadcasted_iota(i32, s.shape, 0)
    vals, idxs = [], []
    for _ in range(TOP_K):
        mx = jnp.max(s, axis=0, keepdims=True)
        ix = jnp.min(jnp.where(s == mx, e_io, N_EXPERTS), axis=0, keepdims=True)
        vals.append(mx)
        idxs.append(ix)
        s = jnp.where(e_io == ix, -jnp.inf, s)
    ex = [jnp.exp(v - vals[0]) for v in vals]
    den = ex[0] + ex[1] + ex[2] + ex[3]
    idx_ref[...] = jnp.concatenate(idxs, axis=0)
    tw_ref[...] = jnp.concatenate([e / den for e in ex], axis=0)


def _outproj(gla_o, nsa_o, x2d, mod, g_ffn, w_out_b, w_router_t, b_router):
    t, tm = x2d.shape[0], mod.tm
    row = lambda i: (i, 0)
    const = lambda s: pl.BlockSpec(s, lambda i: (0, 0))
    return pl.pallas_call(
        _outproj_kernel, grid=(t // tm,),
        in_specs=[pl.BlockSpec((tm, GLA_WIDTH), row), pl.BlockSpec((tm, NSA_WIDTH), row), pl.BlockSpec((tm, D_MODEL), row),
                  mod.spec(2), mod.spec(3), mod.spec(4), const((1, D_MODEL)), const((MIX_WIDTH, D_MODEL)),
                  const((N_EXPERTS, D_MODEL)), const((N_EXPERTS, 1))],
        out_specs=[pl.BlockSpec((tm, D_MODEL), row), pl.BlockSpec((tm, D_MODEL), row),
                   pl.BlockSpec((TOP_K, tm), lambda i: (0, i)), pl.BlockSpec((TOP_K, tm), lambda i: (0, i))],
        out_shape=[SDS((t, D_MODEL), f32), SDS((t, D_MODEL), bf16), SDS((TOP_K, t), i32), SDS((TOP_K, t), f32)],
        compiler_params=_cp("arbitrary"), name="outproj_router",
    )(gla_o, nsa_o, x2d, mod.arr, mod.arr, mod.arr, g_ffn.reshape(1, D_MODEL), w_out_b, w_router_t,
      b_router.reshape(N_EXPERTS, 1))


def _moe_kernel(be_ref, nv_ref, x_ref, rw_ref, wg_ref, bg_ref, wu_ref, bu_ref, wd_ref, bd_ref, y_ref):
    @pl.when(pl.program_id(0) < nv_ref[0])
    def _run():
        x = x_ref[...]
        gate = jnp.minimum(_dot(x, wg_ref[...]) + bg_ref[...], SWIGLU_LIMIT)
        up = jnp.clip(_dot(x, wu_ref[...]) + bu_ref[...], -SWIGLU_LIMIT, SWIGLU_LIMIT)
        act = ((up + 1.0) * gate * jax.nn.sigmoid(SWIGLU_ALPHA * gate)).astype(bf16)
        y_ref[...] = (_dot(act, wd_ref[...]) + bd_ref[...]) * rw_ref[...]


def _moe_experts(xs, row_w, block_e, n_valid, wg, bg, wu, bu, wd, bd):
    n_rows = xs.shape[0]
    tm = MOE_TM
    row = lambda i, be, nv: (i, 0)
    wspec = lambda a, b: pl.BlockSpec((None, a, b), lambda i, be, nv: (be[i], 0, 0))
    return pl.pallas_call(
        _moe_kernel,
        grid_spec=pltpu.PrefetchScalarGridSpec(
            num_scalar_prefetch=2, grid=(n_rows // tm,),
            in_specs=[pl.BlockSpec((tm, D_MODEL), row), pl.BlockSpec((tm, 1), row),
                      wspec(D_MODEL, D_EXPERT), wspec(1, D_EXPERT), wspec(D_MODEL, D_EXPERT), wspec(1, D_EXPERT),
                      wspec(D_EXPERT, D_MODEL), wspec(1, D_MODEL)],
            out_specs=pl.BlockSpec((tm, D_MODEL), row)),
        out_shape=SDS((n_rows, D_MODEL), f32),
        compiler_params=_cp("arbitrary"), name="moe_experts",
    )(block_e, n_valid, xs, row_w.reshape(n_rows, 1), wg, bg.reshape(N_EXPERTS, 1, D_EXPERT), wu,
      bu.reshape(N_EXPERTS, 1, D_EXPERT), wd, bd.reshape(N_EXPERTS, 1, D_MODEL))


def _moe(h2_all, top_idx, top_w, wg, bg, wu, bu, wd, bd):
    n_tok = h2_all.shape[0]
    n_assign = n_tok * TOP_K
    tm = MOE_TM
    flat_e = top_idx.reshape(n_assign)
    order = jnp.argsort(flat_e)
    e_sorted = flat_e[order]
    counts = jnp.bincount(flat_e, length=N_EXPERTS)
    padded = (counts + tm - 1) // tm * tm
    start = jnp.cumsum(counts) - counts
    pad_end = jnp.cumsum(padded)
    dest = ((pad_end - padded)[e_sorted] + jnp.arange(n_assign) - start[e_sorted]).astype(i32)
    n_blocks = (n_assign + N_EXPERTS * (tm - 1) + tm - 1) // tm
    n_rows = n_blocks * tm
    row_tok = jnp.zeros((n_rows,), i32).at[dest].set((order // TOP_K).astype(i32))
    row_w = jnp.zeros((n_rows,), f32).at[dest].set(top_w.reshape(n_assign)[order])
    block_e = jnp.minimum(jnp.searchsorted(pad_end, jnp.arange(n_blocks) * tm, side='right'), N_EXPERTS - 1).astype(i32)
    n_valid = (pad_end[-1] // tm).astype(i32).reshape(1)
    inv = jnp.zeros((n_assign,), i32).at[order].set(dest)
    xs = jnp.take(h2_all, row_tok, axis=0)
    y = _moe_experts(xs, row_w, block_e, n_valid, wg, bg, wu, bu, wd, bd)
    return jnp.take(y, inv, axis=0).reshape(n_tok, TOP_K, D_MODEL).sum(axis=1)


def _final_kernel(x1_ref, f_ref, gate_ref, g_ref, o_ref):
    o_ref[...] = _rms(x1_ref[...] + gate_ref[...] * f_ref[...], g_ref[...])


def _final(x1, f, mod, g_final):
    t, tm = x1.shape[0], mod.tm
    row = lambda i: (i, 0)
    return pl.pallas_call(
        _final_kernel, grid=(t // tm,),
        in_specs=[pl.BlockSpec((tm, D_MODEL), row), pl.BlockSpec((tm, D_MODEL), row), mod.spec(5),
                  pl.BlockSpec((1, D_MODEL), lambda i: (0, 0))],
        out_specs=pl.BlockSpec((tm, D_MODEL), row), out_shape=SDS((t, D_MODEL), f32),
        compiler_params=_cp("arbitrary"), name="final_norm",
    )(x1, f, mod.arr, g_final.reshape(1, D_MODEL))


def _prep_w_in(w_in):
    gq, gk, gv, gr, ga, nq, nkc, nks, nkw, ng = jnp.split(w_in, np.cumsum(IN_SPLITS)[:-1].tolist(), axis=1)
    pad = lambda a: jnp.pad(a, ((0, 0), (0, LANES - a.shape[1])))
    return jnp.concatenate([gq, gk, gv, gr, nq, nkc, nks, nkw, pad(ga), pad(ng)], axis=1).astype(bf16)


def _key_layouts(kv3):
    b = kv3.astype(bf16)
    return b[:, :, :FEAT], jnp.swapaxes(b[:, :, FEAT:], 1, 2)


def kernel(x_prompt, x_sample, c_prompt, c_sample, cache_cmp, cache_sel, state_win, state_gla, page_table, w_ada, b_ada, g_mix, g_ffn, w_in, w_a2, b_a, g_gla, phi_pe, phi_w, w_out, w_router, b_router, w_gate, b_gate, w_up, b_up, w_down, b_down, g_final):
    bp, sp = x_prompt.shape[:2]
    bs, ts = x_sample.shape[:2]
    tp, tsn = bp * sp, bs * ts
    win_buf = state_win.shape[2]
    l = 0

    mod = _adaln(jnp.concatenate([c_prompt, c_sample], axis=0), w_ada[l], b_ada[l])
    tm_p = math.gcd(sp, 512)
    tm_s = math.gcd(tsn, 512)
    mod_p = _Mod(mod[:bp].reshape(bp, 6, 1, D_MODEL), False, tm_p, sp // tm_p)
    mod_s = _Mod(jnp.repeat(mod[bp:].reshape(bs, 6, D_MODEL).transpose(1, 0, 2), ts, axis=1), True, tm_s, None)

    w_in_p = _prep_w_in(w_in[l])
    cw = _prep_compress(phi_pe[l], phi_w[l])
    w_out_b = w_out[l].astype(bf16)
    w_router_t = w_router[l].T.astype(bf16)
    xp2, xs2 = x_prompt.reshape(tp, D_MODEL), x_sample.reshape(tsn, D_MODEL)

    gq, gk, gv, gr, nq, kvc, kvs, kvw, ga, ng = _inproj(xp2, mod_p, g_mix[l], w_in_p)
    gla_o, st_p = _gla(gq, gk, gv, gr, ga, jnp.zeros((bp, GLA_HEADS, GLA_DV, GLA_DK), f32), w_a2[l], b_a[l], g_gla[l], bp, sp)
    kvc3, kvs3, kvw3 = (a.reshape(bp, sp, NSA_KV_WIDTH) for a in (kvc, kvs, kvw))
    kc, vct = _cmp_prompt(kvc3, cw)
    nsa_o = _nsa_prompt(nq.reshape(bp, sp, NSA_WIDTH), ng.reshape(bp, sp, LANES), kc, vct,
                        *_key_layouts(kvs3), *_key_layouts(kvw3))
    x1_p, h2_p, idx_p, tw_p = _outproj(gla_o.reshape(tp, GLA_WIDTH), nsa_o.reshape(tp, NSA_WIDTH), xp2, mod_p,
                                       g_ffn[l], w_out_b, w_router_t, b_router[l])
    kv_row = (2, NSA_KV_HEADS, NSA_HD)
    new_cmp_p = kvc.reshape((1, bp, sp) + kv_row)
    new_sel_p = kvs.reshape((1, bp, sp) + kv_row)
    new_win_p = jnp.pad(kvw3, ((0, 0), (win_buf, 0), (0, 0)))[:, -win_buf:].reshape((1, bp, win_buf) + kv_row)
    new_gla_p = jnp.swapaxes(st_p, 2, 3)[None]

    gq, gk, gv, gr, nq, kvc, kvs, kvw, ga, ng = _inproj(xs2, mod_s, g_mix[l], w_in_p)
    gla_os, st_s = _gla(gq, gk, gv, gr, ga, jnp.swapaxes(state_gla[l], 2, 3), w_a2[l], b_a[l], g_gla[l], bs, ts)
    kvc3, kvs3, kvw3 = (a.reshape(bs, ts, NSA_KV_WIDTH) for a in (kvc, kvs, kvw))
    swin = state_win[l].reshape(bs, win_buf, NSA_KV_WIDTH)
    n_pool = cache_cmp.shape[1]
    nsa_os = _nsa_sample(nq.reshape(bs, ts, NSA_WIDTH), ng.reshape(bs, ts, LANES), kvc3, kvs3, kvw3,
                         cache_cmp[l].reshape(n_pool, PAGE_SIZE, NSA_KV_WIDTH),
                         cache_sel[l].reshape(n_pool, PAGE_SIZE, NSA_KV_WIDTH), swin, page_table, cw)
    x1_s, h2_s, idx_s, tw_s = _outproj(gla_os.reshape(tsn, GLA_WIDTH), nsa_os.reshape(tsn, NSA_WIDTH), xs2, mod_s,
                                       g_ffn[l], w_out_b, w_router_t, b_router[l])
    new_cmp_s = kvc.reshape((1, bs, ts) + kv_row)
    new_sel_s = kvs.reshape((1, bs, ts) + kv_row)
    new_win_s = jnp.concatenate([swin, kvw3], axis=1)[:, -win_buf:].reshape((1, bs, win_buf) + kv_row)
    new_gla_s = jnp.swapaxes(st_s, 2, 3)[None]

    f = _moe(jnp.concatenate([h2_p, h2_s], axis=0),
             jnp.concatenate([idx_p, idx_s], axis=1).T, jnp.concatenate([tw_p, tw_s], axis=1).T,
             w_gate[l].astype(bf16), b_gate[l], w_up[l].astype(bf16), b_up[l], w_down[l].astype(bf16), b_down[l])
    y_p = _final(x1_p, f[:tp], mod_p, g_final).reshape(bp, sp, D_MODEL)
    y_s = _final(x1_s, f[tp:], mod_s, g_final).reshape(bs, ts, D_MODEL)
    return (y_p, y_s, new_cmp_p, new_sel_p, new_win_p, new_gla_p, new_cmp_s, new_sel_s, new_win_s, new_gla_s)
```

```python
import functools
import math

import numpy as np
import jax
import jax.numpy as jnp
from jax import lax
from jax.experimental import pallas as pl
from jax.experimental.pallas import tpu as pltpu

f32, bf16, i32 = jnp.float32, jnp.bfloat16, jnp.int32
SDS = jax.ShapeDtypeStruct

D_MODEL = 1024
GLA_HEADS, GLA_DK, GLA_DV, GLA_GATE_RANK, GLA_TAU, GLA_CHUNK = 4, 64, 128, 16, 16.0, 64
NSA_HEADS, NSA_KV_HEADS, NSA_HD = 8, 2, 64
NSA_GROUP = NSA_HEADS // NSA_KV_HEADS
CMP_LEN, CMP_STRIDE, SEL_BLOCK, N_SEL, WINDOW, NSA_QBLOCK = 32, 16, 64, 8, 512, 128
N_EXPERTS, TOP_K, D_EXPERT = 32, 4, 1024
SWIGLU_LIMIT, SWIGLU_ALPHA = 7.0, 1.702
RMS_EPS, NEG_INF, FORCE_SCORE = 1e-6, -1e30, 1e4
PAGE_SIZE = 128

GLA_QK_WIDTH = GLA_HEADS * GLA_DK
GLA_WIDTH = GLA_HEADS * GLA_DV
NSA_WIDTH = NSA_HEADS * NSA_HD
NSA_KV_WIDTH = 2 * NSA_KV_HEADS * NSA_HD
IN_SPLITS = (GLA_QK_WIDTH, GLA_QK_WIDTH, GLA_WIDTH, GLA_WIDTH, GLA_GATE_RANK,
             NSA_WIDTH, NSA_KV_WIDTH, NSA_KV_WIDTH, NSA_KV_WIDTH, 3 * NSA_HEADS)
MIX_WIDTH = GLA_WIDTH + NSA_WIDTH

LANES = 128
SUBLANES = 8
VMEM_LIMIT_BYTES = 48 * 1024 * 1024
MOE_VMEM_LIMIT_BYTES = 56 * 1024 * 1024
D_CHUNKS = D_MODEL // LANES

FEAT = NSA_KV_HEADS * NSA_HD
_IN_COLS = (("gq", 0, 256, f32), ("gk", 256, 256, f32), ("gv", 512, 512, bf16), ("gr", 1024, 512, f32),
            ("nq", 1536, 512, bf16), ("kc", 2048, FEAT, f32), ("vc", 2176, FEAT, f32), ("ks", 2304, FEAT, f32),
            ("vs", 2432, FEAT, f32), ("kw", 2560, FEAT, f32), ("vw", 2688, FEAT, f32),
            ("ga", 2816, LANES, f32), ("ng", 2944, LANES, f32))
_IN_WIDTH = 3072
_KV0 = 2048
MOE_TM = 512


def _cp(*sem):
    return pltpu.CompilerParams(dimension_semantics=sem, vmem_limit_bytes=VMEM_LIMIT_BYTES)


def _dot(a, b):
    return jnp.dot(a, b, preferred_element_type=f32)


def _dot_nt(a, b):
    return lax.dot_general(a, b, (((1,), (1,)), ((), ())), preferred_element_type=f32)


def _dot_tn(a, b):
    return lax.dot_general(a, b, (((0,), (0,)), ((), ())), preferred_element_type=f32)


def _rms(x, g):
    return x * lax.rsqrt(jnp.mean(x * x, axis=-1, keepdims=True) + RMS_EPS) * g


def _silu(x):
    return x * jax.nn.sigmoid(x)


def _adaln_kernel(c_ref, w_ref, b_ref, o_ref):
    s = _silu(c_ref[...]).astype(bf16)
    o_ref[...] = _dot(s, w_ref[...].astype(bf16)) + b_ref[...]


def _adaln(c_all, w_ada, b_ada):
    nb, n = c_all.shape[0], w_ada.shape[1]
    tn = 1024
    return pl.pallas_call(
        _adaln_kernel, grid=(n // tn,),
        in_specs=[pl.BlockSpec((nb, D_MODEL), lambda j: (0, 0)),
                  pl.BlockSpec((D_MODEL, tn), lambda j: (0, j)),
                  pl.BlockSpec((1, tn), lambda j: (0, j))],
        out_specs=pl.BlockSpec((nb, tn), lambda j: (0, j)),
        out_shape=SDS((nb, n), f32), compiler_params=_cp("arbitrary"), name="adaln",
    )(c_all, w_ada, b_ada.reshape(1, n))


class _Mod:
    def __init__(self, arr, per_token, tm, tiles_per_seq):
        self.arr, self.per_token, self.tm, self.tps = arr, per_token, tm, tiles_per_seq

    def spec(self, k):
        if self.per_token:
            return pl.BlockSpec((None, self.tm, D_MODEL), lambda i: (k, i, 0))
        tps = self.tps
        return pl.BlockSpec((None, None, 1, D_MODEL), lambda i: (i // tps, k, 0, 0))


def _inproj_kernel(x_ref, sh_ref, sc_ref, g_ref, w_ref, *rest, feature_major):
    y = _rms(x_ref[...], g_ref[...])
    h = (y * (1.0 + sc_ref[...]) + sh_ref[...]).astype(bf16)
    outs = rest[1:] if feature_major else rest
    for o_ref, (_, c0, w, _) in zip(outs, _IN_COLS):
        o_ref[...] = _dot(h, w_ref[:, c0:c0 + w]).astype(o_ref.dtype)
    if feature_major:
        wt_ref = rest[0]
        kvt_refs, (ksk_ref, kwk_ref, ksvt_ref, kwvt_ref) = outs[len(_IN_COLS):-4], outs[-4:]
        kvt = _dot_nt(wt_ref[...], h)
        for a, kvt_ref in enumerate(kvt_refs):
            for ch in range(2 * NSA_KV_HEADS):
                r0 = a * NSA_KV_WIDTH + ch * NSA_HD
                kvt_ref[ch // NSA_KV_HEADS, ch % NSA_KV_HEADS] = kvt[r0:r0 + NSA_HD]
        ksvt_ref[...] = kvt[NSA_KV_WIDTH + FEAT:2 * NSA_KV_WIDTH].astype(bf16)
        kwvt_ref[...] = kvt[2 * NSA_KV_WIDTH + FEAT:].astype(bf16)
        ksk_ref[...] = _dot(h, w_ref[:, _KV0 + NSA_KV_WIDTH:_KV0 + NSA_KV_WIDTH + FEAT]).astype(bf16)
        kwk_ref[...] = _dot(h, w_ref[:, _KV0 + 2 * NSA_KV_WIDTH:_KV0 + 2 * NSA_KV_WIDTH + FEAT]).astype(bf16)


def _inproj(x2d, mod, g_mix, w_in_p, w_kv_t=None, seq_len=None):
    t, tm = x2d.shape[0], mod.tm
    row = lambda i: (i, 0)
    in_specs = [pl.BlockSpec((tm, D_MODEL), row), mod.spec(0), mod.spec(1),
                pl.BlockSpec((1, D_MODEL), lambda i: (0, 0)),
                pl.BlockSpec((D_MODEL, _IN_WIDTH), lambda i: (0, 0))]
    out_specs = [pl.BlockSpec((tm, w), row) for (_, _, w, _) in _IN_COLS]
    out_shape = [SDS((t, w), dt) for (_, _, w, dt) in _IN_COLS]
    args = [x2d, mod.arr, mod.arr, g_mix.reshape(1, D_MODEL), w_in_p]
    if w_kv_t is not None:
        b_sz, tps = t // seq_len, seq_len // tm
        in_specs.append(pl.BlockSpec((3 * NSA_KV_WIDTH, D_MODEL), lambda i: (0, 0)))
        args.append(w_kv_t)
        fm = lambda w: pl.BlockSpec((None, w, tm), lambda i: (i // tps, 0, i % tps))
        out_specs += [pl.BlockSpec((None, 2, NSA_KV_HEADS, NSA_HD, tm), lambda i: (i // tps, 0, 0, 0, i % tps))] * 3
        out_specs += [pl.BlockSpec((tm, FEAT), row), pl.BlockSpec((tm, FEAT), row), fm(FEAT), fm(FEAT)]
        out_shape += [SDS((b_sz, 2, NSA_KV_HEADS, NSA_HD, seq_len), f32)] * 3
        out_shape += [SDS((t, FEAT), bf16), SDS((t, FEAT), bf16),
                      SDS((b_sz, FEAT, seq_len), bf16), SDS((b_sz, FEAT, seq_len), bf16)]
    return pl.pallas_call(
        functools.partial(_inproj_kernel, feature_major=w_kv_t is not None), grid=(t // tm,),
        in_specs=in_specs, out_specs=out_specs, out_shape=out_shape,
        compiler_params=_cp("arbitrary"), name="inproj",
    )(*args)


def _gla_kernel(q_ref, k_ref, v_ref, r_ref, a_ref, s0_ref, wa2_ref, ba_ref, gg_ref, o_ref, st_ref, *, chunk, n_chunks):
    @pl.when(pl.program_id(1) == 0)
    def _init():
        st_ref[...] = s0_ref[...]

    ri = lax.broadcasted_iota(i32, (chunk, chunk), 0)
    ci = lax.broadcasted_iota(i32, (chunk, chunk), 1)
    causal = ri >= ci
    tril = jnp.where(causal, 1.0, 0.0).astype(bf16)

    def body(c, carry):
        rows = pl.ds(pl.multiple_of(c * chunk, chunk), chunk)
        a_low = a_ref[rows, :][:, :GLA_GATE_RANK].astype(bf16)
        z = _dot(a_low, wa2_ref[...]) + ba_ref[...]
        log_a = (jnp.minimum(z, 0.0) - jnp.log1p(jnp.exp(-jnp.abs(z)))) * (1.0 / GLA_TAU)
        hi = log_a.astype(bf16)
        lo = (log_a - hi.astype(f32)).astype(bf16)
        cum = _dot(tril, hi) + _dot(tril, lo)
        last = cum[chunk - 1:chunk, :]
        q = q_ref[rows, :] * (GLA_DK ** -0.5)
        k = k_ref[rows, :]
        qd = (q * jnp.exp(cum)).astype(bf16)
        ki = (k * jnp.exp(-cum)).astype(bf16)
        ke = (k * jnp.exp(last - cum)).astype(bf16)
        dec = jnp.exp(last)
        v = v_ref[rows, :]
        r = r_ref[rows, :]
        for h in range(GLA_HEADS):
            sk = slice(h * GLA_DK, (h + 1) * GLA_DK)
            sv = slice(h * GLA_DV, (h + 1) * GLA_DV)
            qh, kih, keh, vh = qd[:, sk], ki[:, sk], ke[:, sk], v[:, sv]
            att = jnp.where(causal, _dot_nt(qh, kih), 0.0)
            st = st_ref[h]
            o = _dot(att.astype(bf16), vh) + _dot_nt(qh, st.astype(bf16))
            st_ref[h] = st * dec[:, sk] + _dot_tn(vh, keh)
            o_ref[rows, sv] = (_rms(o, gg_ref[...]) * _silu(r[:, sv])).astype(o_ref.dtype)
        return carry

    lax.fori_loop(0, n_chunks, body, 0)


def _gla(gq, gk, gv, gr, ga, s0t, w_a2, b_a, g_gla, b_sz, seq_len):
    chunk = math.gcd(seq_len, GLA_CHUNK)
    tb = min(seq_len, 512)
    n_chunks = tb // chunk
    r3 = lambda a: a.reshape(b_sz, seq_len, a.shape[-1])
    tok = lambda w: pl.BlockSpec((None, tb, w), lambda b, j: (b, j, 0))
    st_spec = pl.BlockSpec((None, GLA_HEADS, GLA_DV, GLA_DK), lambda b, j: (b, 0, 0, 0))
    const = lambda s: pl.BlockSpec(s, lambda b, j: (0, 0))
    return pl.pallas_call(
        functools.partial(_gla_kernel, chunk=chunk, n_chunks=n_chunks),
        grid=(b_sz, seq_len // tb),
        in_specs=[tok(GLA_QK_WIDTH), tok(GLA_QK_WIDTH), tok(GLA_WIDTH), tok(GLA_WIDTH), tok(LANES), st_spec,
                  const((GLA_GATE_RANK, GLA_QK_WIDTH)), const((1, GLA_QK_WIDTH)), const((1, GLA_DV))],
        out_specs=[tok(GLA_WIDTH), st_spec],
        out_shape=[SDS((b_sz, seq_len, GLA_WIDTH), bf16), SDS((b_sz, GLA_HEADS, GLA_DV, GLA_DK), f32)],
        compiler_params=_cp("arbitrary", "arbitrary"), name="gla",
    )(r3(gq), r3(gk), r3(gv), r3(gr), r3(ga), s0t, w_a2.astype(bf16), b_a.reshape(1, -1), g_gla.reshape(1, -1))


def _np_psel():
    p = np.zeros((NSA_HEADS, FEAT, NSA_WIDTH), np.float32)
    for h in range(NSA_KV_HEADS):
        for g in range(NSA_GROUP):
            for d in range(NSA_HD):
                p[h * NSA_GROUP + g, h * NSA_HD + d, (h * NSA_GROUP + g) * NSA_HD + d] = 1.0
    return p


def _np_cmp_to_sel_t(n_cmp, n_sel, rows, cols):
    i0 = np.arange(n_cmp)[None, :] * CMP_STRIDE
    j0 = np.arange(n_sel)[:, None] * SEL_BLOCK
    m = np.zeros((rows, cols), np.float32)
    m[:n_sel, :n_cmp] = ((i0 < j0 + SEL_BLOCK) & (i0 + CMP_LEN > j0)).astype(np.float32)
    return m


def _prep_compress(phi_pe, phi_w):
    w = phi_w.reshape(2, CMP_LEN, NSA_HD, NSA_HD)
    wb = jnp.einsum('cldo,hk->clhdko', w, jnp.eye(NSA_KV_HEADS, dtype=f32)).reshape(2, CMP_LEN, FEAT, FEAT)
    pe = jnp.broadcast_to(phi_pe[:, :, None, :], (2, CMP_LEN, NSA_KV_HEADS, NSA_HD)).reshape(2, CMP_LEN, 1, FEAT)
    return pe, wb.astype(bf16)


def _compress_half(x_ref, pe_ref, w_ref, c, rows):
    top = jnp.zeros((rows, FEAT), f32)
    bot = jnp.zeros((rows, FEAT), f32)
    for l in range(CMP_STRIDE):
        x = x_ref[pl.ds(l, rows, stride=CMP_STRIDE), :]
        top = top + _dot((x + pe_ref[c, l]).astype(bf16), w_ref[c, l])
        bot = bot + _dot((x + pe_ref[c, CMP_STRIDE + l]).astype(bf16), w_ref[c, CMP_STRIDE + l])
    return top + pltpu.roll(bot, rows - 1, 0)


def _col_softmax(s, valid):
    s = jnp.where(valid, s, NEG_INF)
    m = jnp.max(s, axis=0, keepdims=True)
    p = jnp.where(valid, jnp.exp(s - m), 0.0)
    l = jnp.sum(p, axis=0, keepdims=True)
    return (p * (1.0 / jnp.maximum(l, 1e-30))).astype(bf16)


def _select_bias(imp, t_col, n_sel):
    blk = lax.broadcasted_iota(i32, imp.shape, 0)
    cur = lax.shift_right_logical(t_col, int(math.log2(SEL_BLOCK)))
    forced = (blk == 0) | (blk == cur) | (blk == cur - 1)
    future = blk * SEL_BLOCK > t_col
    score = jnp.where(future, NEG_INF, jnp.where(forced, FORCE_SCORE, imp))
    score = jnp.where(blk < n_sel, score, -jnp.inf)
    bias = jnp.full(imp.shape, NEG_INF, f32)
    for _ in range(min(N_SEL, n_sel)):
        mx = jnp.max(score, axis=0, keepdims=True)
        idx = jnp.min(jnp.where(score == mx, blk, 2 ** 30), axis=0, keepdims=True)
        pick = blk == idx
        bias = jnp.where(pick, 0.0, bias)
        score = jnp.where(pick, -jnp.inf, score)
    return bias


def _block_bias(bias_ref, first_block, n_blocks, cols):
    return jnp.concatenate(
        [jnp.broadcast_to(bias_ref[pl.ds(first_block + b, 1), :], (SEL_BLOCK, cols)) for b in range(n_blocks)], axis=0)


def _flash_step(s, vt, m_ref, l_ref, acc_ref):
    m_old = m_ref[...]
    m_new = jnp.maximum(m_old, jnp.max(s, axis=0, keepdims=True))
    alpha = jnp.exp(m_old - m_new)
    p = jnp.exp(s - m_new)
    l_ref[...] = alpha * l_ref[...] + jnp.sum(p, axis=0, keepdims=True)
    acc_ref[...] = alpha * acc_ref[...] + _dot(vt, p.astype(bf16))
    m_ref[...] = m_new


def _flash_init(m_ref, l_ref, acc_ref):
    m_ref[...] = jnp.full(m_ref.shape, NEG_INF, f32)
    l_ref[...] = jnp.zeros(l_ref.shape, f32)
    acc_ref[...] = jnp.zeros(acc_ref.shape, f32)


def _cmp_prompt_kernel(k_ref, v_ref, pe_ref, w_ref, kc_ref, vct_ref):
    rows = kc_ref.shape[0]
    kc_ref[...] = _compress_half(k_ref, pe_ref, w_ref, 0, rows).astype(bf16)
    vct_ref[...] = _compress_half(v_ref, pe_ref, w_ref, 1, rows).T.astype(bf16)


def _cmp_prompt(kc3, vc3, cw):
    b_sz, seq_len, _ = kc3.shape
    r = seq_len // CMP_STRIDE
    pe, w = cw
    return pl.pallas_call(
        _cmp_prompt_kernel, grid=(b_sz,),
        in_specs=[pl.BlockSpec((None, seq_len, FEAT), lambda b: (b, 0, 0)), pl.BlockSpec((None, seq_len, FEAT), lambda b: (b, 0, 0)),
                  pl.BlockSpec(pe.shape, lambda b: (0, 0, 0, 0)), pl.BlockSpec(w.shape, lambda b: (0, 0, 0, 0))],
        out_specs=[pl.BlockSpec((None, r, FEAT), lambda b: (b, 0, 0)), pl.BlockSpec((None, FEAT, r), lambda b: (b, 0, 0))],
        out_shape=[SDS((b_sz, r, FEAT), bf16), SDS((b_sz, FEAT, r), bf16)],
        compiler_params=_cp("arbitrary"), name="cmp_prompt",
    )(kc3, vc3, pe, w)


def _nsa_prompt_kernel(q_ref, gate_ref, kc_ref, vct_ref, ksk_ref, ksvt_ref, kwk_ref, kwvt_ref, psel_ref, mt_ref,
                       o_ref, bias_ref, m_ref, l_ref, acc_ref, *, qb, seq_len, tk, wk):
    i = pl.program_id(1)
    start = i * qb
    nc = NSA_HEADS * qb
    n_sel = seq_len // SEL_BLOCK
    n_cmp = seq_len // CMP_STRIDE - CMP_LEN // CMP_STRIDE + 1
    q = q_ref[...]
    qzt = jnp.concatenate([_dot_nt(psel_ref[hg], q) for hg in range(NSA_HEADS)], axis=1)
    qzt = (qzt * (NSA_HD ** -0.5)).astype(bf16)
    t_col = start + lax.broadcasted_iota(i32, (1, nc), 1) % qb

    s = _dot(kc_ref[...], qzt)
    n_io = lax.broadcasted_iota(i32, s.shape, 0)
    valid = (n_io * CMP_STRIDE + (CMP_LEN - 1) <= t_col) & (n_io < n_cmp)
    pc = _col_softmax(s, valid)
    o_c = _dot(vct_ref[...], pc)
    imp_all = _dot(mt_ref[...], pc)
    imp = []
    for h in range(NSA_KV_HEADS):
        tot = imp_all[:, h * NSA_GROUP * qb:(h * NSA_GROUP + 1) * qb]
        for g in range(1, NSA_GROUP):
            tot = tot + imp_all[:, (h * NSA_GROUP + g) * qb:(h * NSA_GROUP + g + 1) * qb]
        imp.append(tot)
    bias = _select_bias(jnp.concatenate(imp, axis=1), t_col[:, :NSA_KV_HEADS * qb], n_sel)
    bias_ref[...] = jnp.concatenate([bias[:, h * qb:(h + 1) * qb] for h in range(NSA_KV_HEADS) for _ in range(NSA_GROUP)], axis=1)

    _flash_init(m_ref, l_ref, acc_ref)
    bpt = tk // SEL_BLOCK

    def scores(jt):
        k0 = pl.multiple_of(jt * tk, tk)
        s = _dot(ksk_ref[pl.ds(k0, tk), :], qzt) + _block_bias(bias_ref, jt * bpt, bpt, nc)
        return k0, s

    def body(jt, carry):
        k0, s = scores(jt)
        _flash_step(s, ksvt_ref[:, pl.ds(k0, tk)], m_ref, l_ref, acc_ref)
        return carry

    n_past = start // tk
    lax.fori_loop(0, n_past, body, 0)
    k0, s = scores(n_past)
    pos = k0 + lax.broadcasted_iota(i32, s.shape, 0)
    _flash_step(jnp.where(pos <= t_col, s, NEG_INF), ksvt_ref[:, pl.ds(k0, tk)], m_ref, l_ref, acc_ref)
    o_s = acc_ref[...] * (1.0 / jnp.maximum(l_ref[...], 1e-30))

    ks = pl.multiple_of(jnp.clip(start - WINDOW, 0, seq_len - wk), qb)
    s = _dot(kwk_ref[pl.ds(ks, wk), :], qzt)
    dpos = t_col - (ks + lax.broadcasted_iota(i32, s.shape, 0))
    pw = _col_softmax(s, (dpos >= 0) & (dpos < WINDOW))
    o_w = _dot(kwvt_ref[:, pl.ds(ks, wk)], pw)

    gt = jax.nn.sigmoid(gate_ref[...]).T
    out = jnp.zeros((qb, NSA_WIDTH), f32)
    for hg in range(NSA_HEADS):
        cs = slice(hg * qb, (hg + 1) * qb)
        mix = gt[3 * hg:3 * hg + 1] * o_c[:, cs] + gt[3 * hg + 1:3 * hg + 2] * o_s[:, cs] + gt[3 * hg + 2:3 * hg + 3] * o_w[:, cs]
        out = out + _dot(mix.T.astype(bf16), psel_ref[hg])
    o_ref[...] = out.astype(o_ref.dtype)


def _nsa_prompt(nq3, ng3, kc, vct, ksk, ksvt, kwk, kwvt):
    b_sz, seq_len, _ = nq3.shape
    qb = math.gcd(seq_len, NSA_QBLOCK)
    tk = 2 * qb
    assert seq_len % tk == 0 and tk % SEL_BLOCK == 0
    wk = min(WINDOW + qb, seq_len)
    r = kc.shape[1]
    n_sel = seq_len // SEL_BLOCK
    n_cmp = seq_len // CMP_STRIDE - CMP_LEN // CMP_STRIDE + 1
    psel = jnp.asarray(_np_psel(), bf16)
    mt = jnp.asarray(_np_cmp_to_sel_t(n_cmp, n_sel, n_sel, r), bf16)
    nc = NSA_HEADS * qb
    per_seq = lambda s: pl.BlockSpec((None,) + s, lambda b, i: (b, 0, 0))
    return pl.pallas_call(
        functools.partial(_nsa_prompt_kernel, qb=qb, seq_len=seq_len, tk=tk, wk=wk),
        grid=(b_sz, seq_len // qb),
        in_specs=[pl.BlockSpec((None, qb, NSA_WIDTH), lambda b, i: (b, i, 0)),
                  pl.BlockSpec((None, qb, LANES), lambda b, i: (b, i, 0)),
                  per_seq((r, FEAT)), per_seq((FEAT, r)),
                  per_seq((seq_len, FEAT)), per_seq((FEAT, seq_len)),
                  per_seq((seq_len, FEAT)), per_seq((FEAT, seq_len)),
                  pl.BlockSpec((NSA_HEADS, FEAT, NSA_WIDTH), lambda b, i: (0, 0, 0)),
                  pl.BlockSpec((n_sel, r), lambda b, i: (0, 0))],
        out_specs=pl.BlockSpec((None, qb, NSA_WIDTH), lambda b, i: (b, i, 0)),
        out_shape=SDS((b_sz, seq_len, NSA_WIDTH), bf16),
        scratch_shapes=[pltpu.VMEM((n_sel, nc), f32), pltpu.VMEM((1, nc), f32), pltpu.VMEM((1, nc), f32),
                        pltpu.VMEM((FEAT, nc), f32)],
        compiler_params=_cp("arbitrary", "arbitrary"), name="nsa_prompt",
    )(nq3, ng3, kc, vct, ksk, ksvt, kwk, kwvt, psel, mt)


NCS = LANES


def _page_halves(page_ref):
    k = jnp.concatenate([page_ref[0, h] for h in range(NSA_KV_HEADS)], axis=0)
    v = jnp.concatenate([page_ref[1, h] for h in range(NSA_KV_HEADS)], axis=0)
    return k, v


def _nsa_s1_kernel(pt_ref, *refs, n_pages, pps, ts, win_buf, rows):
    pages = refs[:pps]
    (newk_ref, newv_ref, q_ref, gate_ref, swin_ref, kwn_ref, pe_ref, w_ref, psel_ref, mt_ref, gsum_ref,
     qzt_ref, part_ref, bias_ref, xk_ref, xv_ref) = refs[pps:]
    j = pl.program_id(1)
    past = n_pages * PAGE_SIZE
    n_tail = xk_ref.shape[0] - past

    @pl.when(j == 0)
    def _fill_tail():
        for x_ref, new_ref in ((xk_ref, newk_ref), (xv_ref, newv_ref)):
            x_ref[pl.ds(past, n_tail), :] = jnp.zeros((n_tail, FEAT), f32)
            x_ref[pl.ds(past, ts), :] = new_ref[...]

    for k in range(pps):
        r0 = pl.multiple_of((j * pps + k) * PAGE_SIZE, PAGE_SIZE)
        kt, vt = _page_halves(pages[k])
        xk_ref[pl.ds(r0, PAGE_SIZE), :] = kt.T
        xv_ref[pl.ds(r0, PAGE_SIZE), :] = vt.T

    @pl.when(j == pl.num_programs(1) - 1)
    def _finish():
        n_rows = NSA_HEADS * ts
        qz = jnp.concatenate([_dot_nt(q_ref[...], psel_ref[hg]) for hg in range(NSA_HEADS)]
                             + [jnp.zeros((NCS - n_rows, FEAT), f32)], axis=0)
        qzt = (qz.T * (NSA_HD ** -0.5)).astype(bf16)
        qzt_ref[...] = qzt
        t_col = past + lax.broadcasted_iota(i32, (1, NCS), 1) % ts
        total = past + ts
        n_sel = -(-total // SEL_BLOCK)
        n_cmp = n_sel * SEL_BLOCK // CMP_STRIDE - CMP_LEN // CMP_STRIDE + 1

        s = _dot(_compress_half(xk_ref, pe_ref, w_ref, 0, rows).astype(bf16), qzt)
        n_io = lax.broadcasted_iota(i32, s.shape, 0)
        valid = (n_io * CMP_STRIDE + (CMP_LEN - 1) <= t_col) & (n_io < n_cmp)
        pc = _col_softmax(s, valid)
        o_c = _dot(_compress_half(xv_ref, pe_ref, w_ref, 1, rows).T.astype(bf16), pc)
        imp_all = _dot(mt_ref[...], pc)
        hi = imp_all.astype(bf16)
        r1 = imp_all - hi.astype(f32)
        mid = r1.astype(bf16)
        lo = (r1 - mid.astype(f32)).astype(bf16)
        gs = gsum_ref[...]
        imp = _dot(hi, gs) + _dot(mid, gs) + _dot(lo, gs)
        bias_ref[...] = _select_bias(imp, t_col, n_sel)

        wkt, wvt = _page_halves(swin_ref)
        kwn = jnp.concatenate([kwn_ref[...], jnp.zeros((ts, NSA_KV_WIDTH), f32)], axis=0).astype(bf16)
        s = jnp.concatenate([_dot_tn(wkt.astype(bf16), qzt), _dot(kwn[:, :FEAT], qzt)], axis=0)
        w_row = lax.broadcasted_iota(i32, s.shape, 0)
        w_pos = total - (win_buf + ts) + w_row
        dpos = t_col - w_pos
        pw = _col_softmax(s, (dpos >= 0) & (dpos < WINDOW) & (w_pos >= 0) & (w_row < win_buf + ts))
        o_w = _dot(wvt.astype(bf16), pw[:win_buf]) + _dot_tn(kwn[:, FEAT:], pw[win_buf:])

        sig = jax.nn.sigmoid(gate_ref[...])
        oc_r, ow_r = o_c.T, o_w.T
        part_ref[...] = jnp.concatenate(
            [sig[:, 3 * hg:3 * hg + 1] * oc_r[hg * ts:(hg + 1) * ts] + sig[:, 3 * hg + 2:3 * hg + 3] * ow_r[hg * ts:(hg + 1) * ts]
             for hg in range(NSA_HEADS)], axis=0)


def _nsa_s2_kernel(pt_ref, *refs, n_pages, pps, ts):
    pages = refs[:pps]
    news_ref, qzt_ref, bias_ref, part_ref, gate_ref, psel_ref, o_ref, m_ref, l_ref, acc_ref = refs[pps:]
    j = pl.program_id(1)
    past = n_pages * PAGE_SIZE
    qzt = qzt_ref[...]
    bpp = PAGE_SIZE // SEL_BLOCK

    @pl.when(j == 0)
    def _init():
        _flash_init(m_ref, l_ref, acc_ref)

    for k in range(pps):
        kt, vt = _page_halves(pages[k])
        s = _dot_tn(kt.astype(bf16), qzt) + _block_bias(bias_ref, (j * pps + k) * bpp, bpp, NCS)
        _flash_step(s, vt.astype(bf16), m_ref, l_ref, acc_ref)

    @pl.when(j == pl.num_programs(1) - 1)
    def _finish():
        t_col = past + lax.broadcasted_iota(i32, (1, NCS), 1) % ts
        new = jnp.concatenate([news_ref[...], jnp.zeros((PAGE_SIZE - ts, NSA_KV_WIDTH), f32)], axis=0)
        s = _dot(new[:, :FEAT].astype(bf16), qzt) + _block_bias(bias_ref, n_pages * bpp, bpp, NCS)
        pos = past + lax.broadcasted_iota(i32, s.shape, 0)
        _flash_step(jnp.where(pos <= t_col, s, NEG_INF), new[:, FEAT:].T.astype(bf16), m_ref, l_ref, acc_ref)
        o_s = (acc_ref[...] * (1.0 / jnp.maximum(l_ref[...], 1e-30))).T
        sig = jax.nn.sigmoid(gate_ref[...])
        part = part_ref[...]
        out = jnp.zeros((ts, NSA_WIDTH), f32)
        for hg in range(NSA_HEADS):
            rs = slice(hg * ts, (hg + 1) * ts)
            mix = part[rs] + sig[:, 3 * hg + 1:3 * hg + 2] * o_s[rs]
            out = out + _dot(mix.astype(bf16), psel_ref[hg])
        o_ref[...] = out.astype(o_ref.dtype)


def _nsa_sample(nq3, ng3, kc3, vc3, kvs3, kvw3, cache_c_t, cache_s_t, swin_t, page_table, cw):
    bs, ts, _ = nq3.shape
    n_pages = page_table.shape[1]
    win_buf = swin_t.shape[-1]
    assert ts == SUBLANES and PAGE_SIZE % SEL_BLOCK == 0 and NSA_HEADS * ts <= NCS
    pps = math.gcd(n_pages, 8)
    steps = n_pages // pps
    past = n_pages * PAGE_SIZE
    total = past + ts
    n_sel = -(-total // SEL_BLOCK)
    n_cmp = n_sel * SEL_BLOCK // CMP_STRIDE - CMP_LEN // CMP_STRIDE + 1
    rows = -(-(n_cmp + 1) // SUBLANES) * SUBLANES
    x_rows = rows * CMP_STRIDE
    rb = -(-(n_sel + 1) // SUBLANES) * SUBLANES
    psel = jnp.asarray(_np_psel(), bf16)
    mt = jnp.asarray(_np_cmp_to_sel_t(n_cmp, n_sel, rb, rows), bf16)
    col = np.arange(NCS)
    gsum = ((col[:, None] // (NSA_GROUP * ts) == col[None, :] // (NSA_GROUP * ts)) & (col[:, None] % ts == col[None, :] % ts))
    gsum = jnp.asarray(gsum.astype(np.float32), bf16)
    pt_flat = page_table.reshape(-1).astype(i32)
    pe, w = cw
    page_shape = (2, NSA_KV_HEADS, NSA_HD, PAGE_SIZE)

    def page_spec(k):
        return pl.BlockSpec((None,) + page_shape, lambda b, j, pt, k=k: (pt[b * n_pages + j * pps + k], 0, 0, 0, 0))

    per_seq = lambda s: pl.BlockSpec((None,) + s, lambda b, j, pt: (b,) + (0,) * len(s))
    const = lambda s: pl.BlockSpec(s, lambda b, j, pt: (0,) * len(s))

    qzt, part, bias = pl.pallas_call(
        functools.partial(_nsa_s1_kernel, n_pages=n_pages, pps=pps, ts=ts, win_buf=win_buf, rows=rows),
        grid_spec=pltpu.PrefetchScalarGridSpec(
            num_scalar_prefetch=1, grid=(bs, steps),
            in_specs=[page_spec(k) for k in range(pps)]
            + [per_seq((ts, FEAT)), per_seq((ts, FEAT)), per_seq((ts, NSA_WIDTH)), per_seq((ts, LANES)),
               per_seq((2, NSA_KV_HEADS, NSA_HD, win_buf)), per_seq((ts, NSA_KV_WIDTH)), const(pe.shape), const(w.shape),
               const((NSA_HEADS, FEAT, NSA_WIDTH)), const((rb, rows)), const((NCS, NCS))],
            out_specs=[per_seq((FEAT, NCS)), per_seq((NSA_HEADS * ts, FEAT)), per_seq((rb, NCS))],
            scratch_shapes=[pltpu.VMEM((x_rows, FEAT), f32), pltpu.VMEM((x_rows, FEAT), f32)]),
        out_shape=[SDS((bs, FEAT, NCS), bf16), SDS((bs, NSA_HEADS * ts, FEAT), f32), SDS((bs, rb, NCS), f32)],
        compiler_params=_cp("arbitrary", "arbitrary"), name="nsa_sample_cmp",
    )(pt_flat, *([cache_c_t] * pps), kc3, vc3, nq3, ng3, swin_t, kvw3, pe, w, psel, mt, gsum)

    return pl.pallas_call(
        functools.partial(_nsa_s2_kernel, n_pages=n_pages, pps=pps, ts=ts),
        grid_spec=pltpu.PrefetchScalarGridSpec(
            num_scalar_prefetch=1, grid=(bs, steps),
            in_specs=[page_spec(k) for k in range(pps)]
            + [per_seq((ts, NSA_KV_WIDTH)), per_seq((FEAT, NCS)), per_seq((rb, NCS)), per_seq((NSA_HEADS * ts, FEAT)),
               per_seq((ts, LANES)), const((NSA_HEADS, FEAT, NSA_WIDTH))],
            out_specs=per_seq((ts, NSA_WIDTH)),
            scratch_shapes=[pltpu.VMEM((1, NCS), f32), pltpu.VMEM((1, NCS), f32), pltpu.VMEM((FEAT, NCS), f32)]),
        out_shape=SDS((bs, ts, NSA_WIDTH), bf16),
        compiler_params=_cp("arbitrary", "arbitrary"), name="nsa_sample_sel",
    )(pt_flat, *([cache_s_t] * pps), kvs3, qzt, bias, part, ng3, psel)


def _outproj_kernel(gla_ref, nsa_ref, x_ref, gm_ref, sh_ref, sc_ref, gf_ref, wo_ref, wrt_ref, br_ref,
                    x1_ref, h2_ref, idx_ref, tw_ref):
    y = _dot(gla_ref[...], wo_ref[:GLA_WIDTH, :]) + _dot(nsa_ref[...], wo_ref[GLA_WIDTH:, :])
    x1 = x_ref[...] + gm_ref[...] * y
    x1_ref[...] = x1
    h2 = _rms(x1, gf_ref[...]) * (1.0 + sc_ref[...]) + sh_ref[...]
    for c in range(D_CHUNKS):
        h2_ref[:, c, :] = h2[:, c * LANES:(c + 1) * LANES]
    s = _dot_nt(wrt_ref[...], h2.astype(bf16)) + br_ref[...]
    e_io = lax.broadcasted_iota(i32, s.shape, 0)
    vals, idxs = [], []
    for _ in range(TOP_K):
        mx = jnp.max(s, axis=0, keepdims=True)
        ix = jnp.min(jnp.where(s == mx, e_io, N_EXPERTS), axis=0, keepdims=True)
        vals.append(mx)
        idxs.append(ix)
        s = jnp.where(e_io == ix, -jnp.inf, s)
    ex = [jnp.exp(v - vals[0]) for v in vals]
    den = ex[0] + ex[1] + ex[2] + ex[3]
    idx_ref[...] = jnp.concatenate(idxs, axis=0)
    tw_ref[...] = jnp.concatenate([e / den for e in ex], axis=0)


def _outproj(gla_o, nsa_o, x2d, mod, g_ffn, w_out_b, w_router_t, b_router):
    t, tm = x2d.shape[0], mod.tm
    row = lambda i: (i, 0)
    const = lambda s: pl.BlockSpec(s, lambda i: (0, 0))
    return pl.pallas_call(
        _outproj_kernel, grid=(t // tm,),
        in_specs=[pl.BlockSpec((tm, GLA_WIDTH), row), pl.BlockSpec((tm, NSA_WIDTH), row), pl.BlockSpec((tm, D_MODEL), row),
                  mod.spec(2), mod.spec(3), mod.spec(4), const((1, D_MODEL)), const((MIX_WIDTH, D_MODEL)),
                  const((N_EXPERTS, D_MODEL)), const((N_EXPERTS, 1))],
        out_specs=[pl.BlockSpec((tm, D_MODEL), row), pl.BlockSpec((tm, D_CHUNKS, LANES), lambda i: (i, 0, 0)),
                   pl.BlockSpec((TOP_K, tm), lambda i: (0, i)), pl.BlockSpec((TOP_K, tm), lambda i: (0, i))],
        out_shape=[SDS((t, D_MODEL), f32), SDS((t, D_CHUNKS, LANES), f32), SDS((TOP_K, t), i32), SDS((TOP_K, t), f32)],
        compiler_params=_cp("arbitrary"), name="outproj_router",
    )(gla_o, nsa_o, x2d, mod.arr, mod.arr, mod.arr, g_ffn.reshape(1, D_MODEL), w_out_b, w_router_t,
      b_router.reshape(N_EXPERTS, 1))


def _moe_kernel(be_ref, nv_ref, meta_hbm, h2_hbm, wg_ref, bg_ref, wu_ref, bu_ref, wd_ref, bd_ref, out_hbm,
                msm, xbuf, ybuf, wbuf, msem, gsem, ssem):
    i = pl.program_id(0)
    nv = nv_ref[0]
    tm = xbuf.shape[1]
    n_real = out_hbm.shape[0] - tm

    def meta_copy(b):
        return pltpu.make_async_copy(meta_hbm.at[b], msm.at[b % 3], msem.at[b % 3])

    def row_copies(b, start):
        def body(r, carry):
            if start == "gather":
                pltpu.make_async_copy(h2_hbm.at[msm[b % 3, 0, r]], xbuf.at[b % 2, r], gsem.at[b % 2]).start()
            else:
                pltpu.make_async_copy(ybuf.at[b % 2, r], out_hbm.at[msm[b % 3, 1, r]], ssem.at[b % 2]).start()
            return carry
        lax.fori_loop(0, tm, body, 0, unroll=8)

    def wait_gather(b):
        pltpu.make_async_copy(h2_hbm.at[pl.ds(0, tm)], xbuf.at[b % 2], gsem.at[b % 2]).wait()

    def wait_scatter(b):
        pltpu.make_async_copy(ybuf.at[b % 2], out_hbm.at[pl.ds(0, tm)], ssem.at[b % 2]).wait()

    @pl.when(i == 0)
    def _prologue():
        meta_copy(0).start()
        ybuf[1] = jnp.zeros(ybuf.shape[1:], f32)
        zero_fill = pltpu.make_async_copy(ybuf.at[1], out_hbm.at[pl.ds(n_real, tm)], ssem.at[1])
        zero_fill.start()
        meta_copy(0).wait()
        row_copies(0, "gather")
        zero_fill.wait()

        @pl.when(nv > 1)
        def _():
            meta_copy(1).start()

    @pl.when(i < nv)
    def _run():
        @pl.when(i + 1 < nv)
        def _next_gather():
            meta_copy(i + 1).wait()
            row_copies(i + 1, "gather")

        @pl.when(i + 2 < nv)
        def _next_meta():
            meta_copy(i + 2).start()

        @pl.when((i == 0) | (be_ref[i] != be_ref[jnp.maximum(i - 1, 0)]))
        def _new_expert():
            wbuf[0] = wg_ref[...].astype(bf16)
            wbuf[1] = wu_ref[...].astype(bf16)
            wbuf[2] = wd_ref[...].astype(bf16)

        wait_gather(i)
        x = jnp.concatenate([xbuf[i % 2, :, c, :].astype(bf16) for c in range(D_CHUNKS)], axis=1)
        gate = jnp.minimum(_dot(x, wbuf[0]) + bg_ref[...], SWIGLU_LIMIT)
        up = jnp.clip(_dot(x, wbuf[1]) + bu_ref[...], -SWIGLU_LIMIT, SWIGLU_LIMIT)
        act = ((up + 1.0) * gate * jax.nn.sigmoid(SWIGLU_ALPHA * gate)).astype(bf16)
        y = _dot(act, wbuf[2]) + bd_ref[...]

        @pl.when(i >= 2)
        def _free_ybuf():
            wait_scatter(i - 2)

        for c in range(D_CHUNKS):
            ybuf[i % 2, :, c, :] = y[:, c * LANES:(c + 1) * LANES]
        row_copies(i, "scatter")

        @pl.when(i == nv - 1)
        def _drain():
            wait_scatter(i)

            @pl.when(i >= 1)
            def _():
                wait_scatter(i - 1)


def _moe_experts(h2_all, meta, block_e, n_valid, n_slots, wg, bg, wu, bu, wd, bd):
    n_blocks, _, tm = meta.shape
    assert D_MODEL == D_EXPERT
    any_spec = pl.BlockSpec(memory_space=pl.ANY)
    wspec = lambda a, b: pl.BlockSpec((None, a, b), lambda i, be, nv: (be[i], 0, 0))
    return pl.pallas_call(
        _moe_kernel,
        grid_spec=pltpu.PrefetchScalarGridSpec(
            num_scalar_prefetch=2, grid=(n_blocks,),
            in_specs=[any_spec, any_spec,
                      wspec(D_MODEL, D_EXPERT), wspec(1, D_EXPERT), wspec(D_MODEL, D_EXPERT), wspec(1, D_EXPERT),
                      wspec(D_EXPERT, D_MODEL), wspec(1, D_MODEL)],
            out_specs=any_spec,
            scratch_shapes=[pltpu.SMEM((3, 2, tm), i32), pltpu.VMEM((2, tm, D_CHUNKS, LANES), f32),
                            pltpu.VMEM((2, tm, D_CHUNKS, LANES), f32), pltpu.VMEM((3, D_MODEL, D_EXPERT), bf16),
                            pltpu.SemaphoreType.DMA((3,)), pltpu.SemaphoreType.DMA((2,)), pltpu.SemaphoreType.DMA((2,))]),
        out_shape=SDS((n_slots, D_CHUNKS, LANES), f32),
        compiler_params=pltpu.CompilerParams(dimension_semantics=("arbitrary",), vmem_limit_bytes=MOE_VMEM_LIMIT_BYTES),
        name="moe_experts",
    )(block_e, n_valid, meta, h2_all, wg, bg.reshape(N_EXPERTS, 1, D_EXPERT), wu,
      bu.reshape(N_EXPERTS, 1, D_EXPERT), wd, bd.reshape(N_EXPERTS, 1, D_MODEL))


def _moe_plan(idx_t, tm):
    n_tok = idx_t.shape[1]
    n_assign = n_tok * TOP_K
    flat_e = idx_t.reshape(n_assign)
    order = jnp.argsort(flat_e).astype(i32)
    counts = jnp.sum((flat_e[:, None] == jnp.arange(N_EXPERTS, dtype=i32)[None, :]).astype(i32), axis=0)
    padded = (counts + tm - 1) // tm * tm
    start = jnp.cumsum(counts) - counts
    pad_end = jnp.cumsum(padded)
    pad_start = pad_end - padded
    n_blocks = (n_assign + N_EXPERTS * (tm - 1) + tm - 1) // tm
    blk0 = jnp.arange(n_blocks, dtype=i32) * tm
    block_e = jnp.minimum(jnp.sum((pad_end[None, :] <= blk0[:, None]).astype(i32), axis=1), N_EXPERTS - 1).astype(i32)
    n_valid = (pad_end[-1] // tm).astype(i32).reshape(1)
    r_in = jnp.arange(tm, dtype=i32)[None, :]
    j = blk0[:, None] + r_in - pad_start[block_e][:, None]
    valid = (j < counts[block_e][:, None]) & (blk0[:, None] < pad_end[-1])
    a = order[jnp.clip(start[block_e][:, None] + j, 0, n_assign - 1)]
    tok = a % n_tok
    slot = tok * TOP_K + a // n_tok
    meta = jnp.stack([jnp.where(valid, tok, 0), jnp.where(valid, slot, n_assign + r_in)], axis=1)
    return meta.astype(i32), block_e, n_valid


def _final_kernel(x1_ref, y4_ref, tw_ref, gate_ref, g_ref, o_ref):
    tw = tw_ref[...]
    zs, ss = [], 0.0
    for c in range(D_CHUNKS):
        cs = slice(c * LANES, (c + 1) * LANES)
        f = tw[:, 0:1] * y4_ref[:, 0, c, :]
        for k in range(1, TOP_K):
            f = f + tw[:, k:k + 1] * y4_ref[:, k, c, :]
        z = x1_ref[:, cs] + gate_ref[:, cs] * f
        zs.append(z)
        ss = ss + jnp.sum(z * z, axis=-1, keepdims=True)
    inv = lax.rsqrt(ss * (1.0 / D_MODEL) + RMS_EPS)
    for c in range(D_CHUNKS):
        cs = slice(c * LANES, (c + 1) * LANES)
        o_ref[:, cs] = zs[c] * inv * g_ref[:, cs]


def _final(x1, y4, tw, mod, g_final, row0):
    t, tm = x1.shape[0], mod.tm
    assert row0 % tm == 0
    b0 = row0 // tm
    row = lambda i: (i, 0)
    return pl.pallas_call(
        _final_kernel, grid=(t // tm,),
        in_specs=[pl.BlockSpec((tm, D_MODEL), row),
                  pl.BlockSpec((tm, TOP_K, D_CHUNKS, LANES), lambda i: (i + b0, 0, 0, 0)),
                  pl.BlockSpec((tm, TOP_K), lambda i: (i + b0, 0)), mod.spec(5),
                  pl.BlockSpec((1, D_MODEL), lambda i: (0, 0))],
        out_specs=pl.BlockSpec((tm, D_MODEL), row), out_shape=SDS((t, D_MODEL), f32),
        compiler_params=_cp("arbitrary"), name="combine_final_norm",
    )(x1, y4, tw, mod.arr, g_final.reshape(1, D_MODEL))


def _prep_w_in(w_in):
    gq, gk, gv, gr, ga, nq, nkc, nks, nkw, ng = jnp.split(w_in, np.cumsum(IN_SPLITS)[:-1].tolist(), axis=1)
    pad = lambda a: jnp.pad(a, ((0, 0), (0, LANES - a.shape[1])))
    w_p = jnp.concatenate([gq, gk, gv, gr, nq, nkc, nks, nkw, pad(ga), pad(ng)], axis=1).astype(bf16)
    w_kv_t = jnp.concatenate([nkc, nks, nkw], axis=1).T.astype(bf16)
    return w_p, w_kv_t


def _feature_major(a, rows_axis):
    return jnp.moveaxis(a, rows_axis, -1)


def kernel(x_prompt, x_sample, c_prompt, c_sample, cache_cmp, cache_sel, state_win, state_gla, page_table, w_ada, b_ada, g_mix, g_ffn, w_in, w_a2, b_a, g_gla, phi_pe, phi_w, w_out, w_router, b_router, w_gate, b_gate, w_up, b_up, w_down, b_down, g_final):
    bp, sp = x_prompt.shape[:2]
    bs, ts = x_sample.shape[:2]
    tp, tsn = bp * sp, bs * ts
    win_buf = state_win.shape[2]
    l = 0

    mod = _adaln(jnp.concatenate([c_prompt, c_sample], axis=0), w_ada[l], b_ada[l])
    tm_p = math.gcd(sp, 512)
    tm_s = math.gcd(tsn, 512)
    mod_p = _Mod(mod[:bp].reshape(bp, 6, 1, D_MODEL), False, tm_p, sp // tm_p)
    mod_s = _Mod(jnp.repeat(mod[bp:].reshape(bs, 6, D_MODEL).transpose(1, 0, 2), ts, axis=1), True, tm_s, None)

    w_in_p, w_kv_t = _prep_w_in(w_in[l])
    cw = _prep_compress(phi_pe[l], phi_w[l])
    w_out_b = w_out[l].astype(bf16)
    w_router_t = w_router[l].T.astype(bf16)
    xp2, xs2 = x_prompt.reshape(tp, D_MODEL), x_sample.reshape(tsn, D_MODEL)

    (gq, gk, gv, gr, nq, kc, vc, _, _, _, _, ga, ng, kvc_t, kvs_t, kvw_t, ksk, kwk, ksvt, kwvt) = _inproj(
        xp2, mod_p, g_mix[l], w_in_p, w_kv_t, sp)
    gla_o, st_p = _gla(gq, gk, gv, gr, ga, jnp.zeros((bp, GLA_HEADS, GLA_DV, GLA_DK), f32), w_a2[l], b_a[l], g_gla[l], bp, sp)
    kcm, vct = _cmp_prompt(kc.reshape(bp, sp, FEAT), vc.reshape(bp, sp, FEAT), cw)
    nsa_o = _nsa_prompt(nq.reshape(bp, sp, NSA_WIDTH), ng.reshape(bp, sp, LANES), kcm, vct,
                        ksk.reshape(bp, sp, FEAT), ksvt, kwk.reshape(bp, sp, FEAT), kwvt)
    x1_p, h2_p, idx_p, tw_p = _outproj(gla_o.reshape(tp, GLA_WIDTH), nsa_o.reshape(tp, NSA_WIDTH), xp2, mod_p,
                                       g_ffn[l], w_out_b, w_router_t, b_router[l])
    token_major = lambda a: jnp.moveaxis(a, -1, 1)[None]
    new_cmp_p = token_major(kvc_t)
    new_sel_p = token_major(kvs_t)
    new_win_p = token_major(jnp.pad(kvw_t, ((0, 0),) * 4 + ((win_buf, 0),))[..., -win_buf:])
    new_gla_p = jnp.swapaxes(st_p, 2, 3)[None]

    gq, gk, gv, gr, nq, kc, vc, ks, vs, kw, vw, ga, ng = _inproj(xs2, mod_s, g_mix[l], w_in_p)
    gla_os, st_s = _gla(gq, gk, gv, gr, ga, jnp.swapaxes(state_gla[l], 2, 3), w_a2[l], b_a[l], g_gla[l], bs, ts)
    r3 = lambda a: a.reshape(bs, ts, a.shape[-1])
    kvc3, kvs3, kvw3 = (jnp.concatenate([r3(k_), r3(v_)], axis=-1) for k_, v_ in ((kc, vc), (ks, vs), (kw, vw)))
    swin_t = _feature_major(state_win[l], 1)
    nsa_os = _nsa_sample(r3(nq), r3(ng), r3(kc), r3(vc), kvs3, kvw3,
                         _feature_major(cache_cmp[l], 1), _feature_major(cache_sel[l], 1), swin_t, page_table, cw)
    x1_s, h2_s, idx_s, tw_s = _outproj(gla_os.reshape(tsn, GLA_WIDTH), nsa_os.reshape(tsn, NSA_WIDTH), xs2, mod_s,
                                       g_ffn[l], w_out_b, w_router_t, b_router[l])
    kv_row = (2, NSA_KV_HEADS, NSA_HD)
    new_cmp_s = kvc3.reshape((1, bs, ts) + kv_row)
    new_sel_s = kvs3.reshape((1, bs, ts) + kv_row)
    kw_t = _feature_major(kvw3.reshape((bs, ts) + kv_row), 1)
    new_win_s = token_major(jnp.concatenate([swin_t, kw_t], axis=-1)[..., -win_buf:])
    new_gla_s = jnp.swapaxes(st_s, 2, 3)[None]

    n_tok = tp + tsn
    meta, block_e, n_valid = _moe_plan(jnp.concatenate([idx_p, idx_s], axis=1), MOE_TM)
    n_slots = n_tok * TOP_K + -(-MOE_TM // TOP_K) * TOP_K
    y4 = _moe_experts(jnp.concatenate([h2_p, h2_s], axis=0), meta, block_e, n_valid, n_slots,
                      w_gate[l], b_gate[l], w_up[l], b_up[l], w_down[l], b_down[l])
    y4 = y4.reshape(n_slots // TOP_K, TOP_K, D_CHUNKS, LANES)
    tw_all = jnp.concatenate([tw_p, tw_s], axis=1).T
    y_p = _final(x1_p, y4, tw_all, mod_p, g_final, 0).reshape(bp, sp, D_MODEL)
    y_s = _final(x1_s, y4, tw_all, mod_s, g_final, tp).reshape(bs, ts, D_MODEL)
    return (y_p, y_s, new_cmp_p, new_sel_p, new_win_p, new_gla_p, new_cmp_s, new_sel_s, new_win_s, new_gla_s)
```

```python
import functools
import math

import numpy as np
import jax
import jax.numpy as jnp
from jax import lax
from jax.experimental import pallas as pl
from jax.experimental.pallas import tpu as pltpu

f32, bf16, i32 = jnp.float32, jnp.bfloat16, jnp.int32
SDS = jax.ShapeDtypeStruct

D_MODEL = 1024
GLA_HEADS, GLA_DK, GLA_DV, GLA_GATE_RANK, GLA_TAU, GLA_CHUNK = 4, 64, 128, 16, 16.0, 64
NSA_HEADS, NSA_KV_HEADS, NSA_HD = 8, 2, 64
NSA_GROUP = NSA_HEADS // NSA_KV_HEADS
CMP_LEN, CMP_STRIDE, SEL_BLOCK, N_SEL, WINDOW, NSA_QBLOCK = 32, 16, 64, 8, 512, 128
N_EXPERTS, TOP_K, D_EXPERT = 32, 4, 1024
SWIGLU_LIMIT, SWIGLU_ALPHA = 7.0, 1.702
RMS_EPS, NEG_INF, FORCE_SCORE = 1e-6, -1e30, 1e4
PAGE_SIZE = 128

GLA_QK_WIDTH = GLA_HEADS * GLA_DK
GLA_WIDTH = GLA_HEADS * GLA_DV
NSA_WIDTH = NSA_HEADS * NSA_HD
NSA_KV_WIDTH = 2 * NSA_KV_HEADS * NSA_HD
IN_SPLITS = (GLA_QK_WIDTH, GLA_QK_WIDTH, GLA_WIDTH, GLA_WIDTH, GLA_GATE_RANK,
             NSA_WIDTH, NSA_KV_WIDTH, NSA_KV_WIDTH, NSA_KV_WIDTH, 3 * NSA_HEADS)
MIX_WIDTH = GLA_WIDTH + NSA_WIDTH

LANES = 128
SUBLANES = 8
VMEM_LIMIT_BYTES = 48 * 1024 * 1024
MOE_VMEM_LIMIT_BYTES = 56 * 1024 * 1024
D_CHUNKS = D_MODEL // LANES

FEAT = NSA_KV_HEADS * NSA_HD
_IN_COLS = (("gq", 0, 256, f32), ("gk", 256, 256, f32), ("gv", 512, 512, bf16), ("gr", 1024, 512, f32),
            ("nq", 1536, 512, bf16), ("kc", 2048, FEAT, f32), ("vc", 2176, FEAT, f32), ("ks", 2304, FEAT, f32),
            ("vs", 2432, FEAT, f32), ("kw", 2560, FEAT, f32), ("vw", 2688, FEAT, f32),
            ("ga", 2816, LANES, f32), ("ng", 2944, LANES, f32))
_IN_WIDTH = 3072
_KV0 = 2048
MOE_TM = 512


def _cp(*sem):
    return pltpu.CompilerParams(dimension_semantics=sem, vmem_limit_bytes=VMEM_LIMIT_BYTES)


def _dot(a, b):
    return jnp.dot(a, b, preferred_element_type=f32)


def _dot_nt(a, b):
    return lax.dot_general(a, b, (((1,), (1,)), ((), ())), preferred_element_type=f32)


def _dot_tn(a, b):
    return lax.dot_general(a, b, (((0,), (0,)), ((), ())), preferred_element_type=f32)


def _rms(x, g):
    return x * lax.rsqrt(jnp.mean(x * x, axis=-1, keepdims=True) + RMS_EPS) * g


def _silu(x):
    return x * jax.nn.sigmoid(x)


def _adaln_kernel(c_ref, w_ref, b_ref, o_ref):
    s = _silu(c_ref[...]).astype(bf16)
    o_ref[...] = _dot(s, w_ref[...].astype(bf16)) + b_ref[...]


def _adaln(c_all, w_ada, b_ada):
    nb, n = c_all.shape[0], w_ada.shape[1]
    tn = 1024
    return pl.pallas_call(
        _adaln_kernel, grid=(n // tn,),
        in_specs=[pl.BlockSpec((nb, D_MODEL), lambda j: (0, 0)),
                  pl.BlockSpec((D_MODEL, tn), lambda j: (0, j)),
                  pl.BlockSpec((1, tn), lambda j: (0, j))],
        out_specs=pl.BlockSpec((nb, tn), lambda j: (0, j)),
        out_shape=SDS((nb, n), f32), compiler_params=_cp("arbitrary"), name="adaln",
    )(c_all, w_ada, b_ada.reshape(1, n))


class _Mod:
    def __init__(self, arr, per_token, tm, tiles_per_seq):
        self.arr, self.per_token, self.tm, self.tps = arr, per_token, tm, tiles_per_seq

    def spec(self, k):
        if self.per_token:
            return pl.BlockSpec((None, self.tm, D_MODEL), lambda i: (k, i, 0))
        tps = self.tps
        return pl.BlockSpec((None, None, 1, D_MODEL), lambda i: (i // tps, k, 0, 0))


def _inproj_kernel(x_ref, sh_ref, sc_ref, g_ref, w_ref, *rest, feature_major):
    y = _rms(x_ref[...], g_ref[...])
    h = (y * (1.0 + sc_ref[...]) + sh_ref[...]).astype(bf16)
    outs = rest[1:] if feature_major else rest
    for o_ref, (_, c0, w, _) in zip(outs, _IN_COLS):
        o_ref[...] = _dot(h, w_ref[:, c0:c0 + w]).astype(o_ref.dtype)
    if feature_major:
        wt_ref = rest[0]
        kvt_refs, (ksk_ref, kwk_ref, ksvt_ref, kwvt_ref) = outs[len(_IN_COLS):-4], outs[-4:]
        kvt = _dot_nt(wt_ref[...], h)
        for a, kvt_ref in enumerate(kvt_refs):
            for ch in range(2 * NSA_KV_HEADS):
                r0 = a * NSA_KV_WIDTH + ch * NSA_HD
                kvt_ref[ch // NSA_KV_HEADS, ch % NSA_KV_HEADS] = kvt[r0:r0 + NSA_HD]
        ksvt_ref[...] = kvt[NSA_KV_WIDTH + FEAT:2 * NSA_KV_WIDTH].astype(bf16)
        kwvt_ref[...] = kvt[2 * NSA_KV_WIDTH + FEAT:].astype(bf16)
        ksk_ref[...] = _dot(h, w_ref[:, _KV0 + NSA_KV_WIDTH:_KV0 + NSA_KV_WIDTH + FEAT]).astype(bf16)
        kwk_ref[...] = _dot(h, w_ref[:, _KV0 + 2 * NSA_KV_WIDTH:_KV0 + 2 * NSA_KV_WIDTH + FEAT]).astype(bf16)


def _inproj(x2d, mod, g_mix, w_in_p, w_kv_t=None, seq_len=None):
    t, tm = x2d.shape[0], mod.tm
    row = lambda i: (i, 0)
    in_specs = [pl.BlockSpec((tm, D_MODEL), row), mod.spec(0), mod.spec(1),
                pl.BlockSpec((1, D_MODEL), lambda i: (0, 0)),
                pl.BlockSpec((D_MODEL, _IN_WIDTH), lambda i: (0, 0))]
    out_specs = [pl.BlockSpec((tm, w), row) for (_, _, w, _) in _IN_COLS]
    out_shape = [SDS((t, w), dt) for (_, _, w, dt) in _IN_COLS]
    args = [x2d, mod.arr, mod.arr, g_mix.reshape(1, D_MODEL), w_in_p]
    if w_kv_t is not None:
        b_sz, tps = t // seq_len, seq_len // tm
        in_specs.append(pl.BlockSpec((3 * NSA_KV_WIDTH, D_MODEL), lambda i: (0, 0)))
        args.append(w_kv_t)
        fm = lambda w: pl.BlockSpec((None, w, tm), lambda i: (i // tps, 0, i % tps))
        out_specs += [pl.BlockSpec((None, 2, NSA_KV_HEADS, NSA_HD, tm), lambda i: (i // tps, 0, 0, 0, i % tps))] * 3
        out_specs += [pl.BlockSpec((tm, FEAT), row), pl.BlockSpec((tm, FEAT), row), fm(FEAT), fm(FEAT)]
        out_shape += [SDS((b_sz, 2, NSA_KV_HEADS, NSA_HD, seq_len), f32)] * 3
        out_shape += [SDS((t, FEAT), bf16), SDS((t, FEAT), bf16),
                      SDS((b_sz, FEAT, seq_len), bf16), SDS((b_sz, FEAT, seq_len), bf16)]
    return pl.pallas_call(
        functools.partial(_inproj_kernel, feature_major=w_kv_t is not None), grid=(t // tm,),
        in_specs=in_specs, out_specs=out_specs, out_shape=out_shape,
        compiler_params=_cp("arbitrary"), name="inproj",
    )(*args)


def _gla_kernel(q_ref, k_ref, v_ref, r_ref, a_ref, s0_ref, wa2_ref, ba_ref, gg_ref, o_ref, st_ref, *, chunk, n_chunks):
    @pl.when(pl.program_id(1) == 0)
    def _init():
        st_ref[...] = s0_ref[...]

    ri = lax.broadcasted_iota(i32, (chunk, chunk), 0)
    ci = lax.broadcasted_iota(i32, (chunk, chunk), 1)
    causal = ri >= ci
    tril = jnp.where(causal, 1.0, 0.0).astype(bf16)

    def body(c, carry):
        rows = pl.ds(pl.multiple_of(c * chunk, chunk), chunk)
        a_low = a_ref[rows, :][:, :GLA_GATE_RANK].astype(bf16)
        z = _dot(a_low, wa2_ref[...]) + ba_ref[...]
        log_a = (jnp.minimum(z, 0.0) - jnp.log1p(jnp.exp(-jnp.abs(z)))) * (1.0 / GLA_TAU)
        hi = log_a.astype(bf16)
        lo = (log_a - hi.astype(f32)).astype(bf16)
        cum = _dot(tril, hi) + _dot(tril, lo)
        last = cum[chunk - 1:chunk, :]
        q = q_ref[rows, :] * (GLA_DK ** -0.5)
        k = k_ref[rows, :]
        qd = (q * jnp.exp(cum)).astype(bf16)
        ki = (k * jnp.exp(-cum)).astype(bf16)
        ke = (k * jnp.exp(last - cum)).astype(bf16)
        dec = jnp.exp(last)
        v = v_ref[rows, :]
        r = r_ref[rows, :]
        for h in range(GLA_HEADS):
            sk = slice(h * GLA_DK, (h + 1) * GLA_DK)
            sv = slice(h * GLA_DV, (h + 1) * GLA_DV)
            qh, kih, keh, vh = qd[:, sk], ki[:, sk], ke[:, sk], v[:, sv]
            att = jnp.where(causal, _dot_nt(qh, kih), 0.0)
            st = st_ref[h]
            o = _dot(att.astype(bf16), vh) + _dot_nt(qh, st.astype(bf16))
            st_ref[h] = st * dec[:, sk] + _dot_tn(vh, keh)
            o_ref[rows, sv] = (_rms(o, gg_ref[...]) * _silu(r[:, sv])).astype(o_ref.dtype)
        return carry

    lax.fori_loop(0, n_chunks, body, 0)


def _gla(gq, gk, gv, gr, ga, s0t, w_a2, b_a, g_gla, b_sz, seq_len):
    chunk = math.gcd(seq_len, GLA_CHUNK)
    tb = min(seq_len, 512)
    n_chunks = tb // chunk
    r3 = lambda a: a.reshape(b_sz, seq_len, a.shape[-1])
    tok = lambda w: pl.BlockSpec((None, tb, w), lambda b, j: (b, j, 0))
    st_spec = pl.BlockSpec((None, GLA_HEADS, GLA_DV, GLA_DK), lambda b, j: (b, 0, 0, 0))
    const = lambda s: pl.BlockSpec(s, lambda b, j: (0, 0))
    return pl.pallas_call(
        functools.partial(_gla_kernel, chunk=chunk, n_chunks=n_chunks),
        grid=(b_sz, seq_len // tb),
        in_specs=[tok(GLA_QK_WIDTH), tok(GLA_QK_WIDTH), tok(GLA_WIDTH), tok(GLA_WIDTH), tok(LANES), st_spec,
                  const((GLA_GATE_RANK, GLA_QK_WIDTH)), const((1, GLA_QK_WIDTH)), const((1, GLA_DV))],
        out_specs=[tok(GLA_WIDTH), st_spec],
        out_shape=[SDS((b_sz, seq_len, GLA_WIDTH), bf16), SDS((b_sz, GLA_HEADS, GLA_DV, GLA_DK), f32)],
        compiler_params=_cp("arbitrary", "arbitrary"), name="gla",
    )(r3(gq), r3(gk), r3(gv), r3(gr), r3(ga), s0t, w_a2.astype(bf16), b_a.reshape(1, -1), g_gla.reshape(1, -1))


def _np_psel():
    p = np.zeros((NSA_HEADS, FEAT, NSA_WIDTH), np.float32)
    for h in range(NSA_KV_HEADS):
        for g in range(NSA_GROUP):
            for d in range(NSA_HD):
                p[h * NSA_GROUP + g, h * NSA_HD + d, (h * NSA_GROUP + g) * NSA_HD + d] = 1.0
    return p


def _np_cmp_to_sel_t(n_cmp, n_sel, rows, cols):
    i0 = np.arange(n_cmp)[None, :] * CMP_STRIDE
    j0 = np.arange(n_sel)[:, None] * SEL_BLOCK
    m = np.zeros((rows, cols), np.float32)
    m[:n_sel, :n_cmp] = ((i0 < j0 + SEL_BLOCK) & (i0 + CMP_LEN > j0)).astype(np.float32)
    return m


def _prep_compress(phi_pe, phi_w):
    w = phi_w.reshape(2, CMP_LEN, NSA_HD, NSA_HD)
    wb = jnp.einsum('cldo,hk->clhdko', w, jnp.eye(NSA_KV_HEADS, dtype=f32)).reshape(2, CMP_LEN, FEAT, FEAT)
    pe = jnp.broadcast_to(phi_pe[:, :, None, :], (2, CMP_LEN, NSA_KV_HEADS, NSA_HD)).reshape(2, CMP_LEN, 1, FEAT)
    return pe, wb.astype(bf16)


def _compress_half(x_ref, pe_ref, w_ref, c, rows):
    top = jnp.zeros((rows, FEAT), f32)
    bot = jnp.zeros((rows, FEAT), f32)
    for l in range(0, CMP_STRIDE, 2):
        xa = x_ref[pl.ds(l, rows, stride=CMP_STRIDE), :]
        xb = x_ref[pl.ds(l + 1, rows, stride=CMP_STRIDE), :]
        for acc_is_top, l0 in ((True, l), (False, CMP_STRIDE + l)):
            x2 = jnp.concatenate([xa + pe_ref[c, l0], xb + pe_ref[c, l0 + 1]], axis=1).astype(bf16)
            w2 = w_ref[c, pl.ds(l0, 2)].reshape(2 * FEAT, FEAT)
            if acc_is_top:
                top = top + _dot(x2, w2)
            else:
                bot = bot + _dot(x2, w2)
    return top + pltpu.roll(bot, rows - 1, 0)


def _col_softmax(s, valid, every_column_valid=False):
    s = jnp.where(valid, s, NEG_INF)
    m = jnp.max(s, axis=0, keepdims=True)
    p = jnp.exp(s - m)
    if not every_column_valid:
        p = jnp.where(valid, p, 0.0)
    l = jnp.sum(p, axis=0, keepdims=True)
    return p.astype(bf16), 1.0 / jnp.maximum(l, 1e-30)


def _select_bias(imp, t_col, n_sel):
    blk = lax.broadcasted_iota(i32, imp.shape, 0)
    cur = lax.shift_right_logical(t_col, int(math.log2(SEL_BLOCK)))
    forced = (blk == 0) | (blk == cur) | (blk == cur - 1)
    future = blk * SEL_BLOCK > t_col
    score = jnp.where(future, NEG_INF, jnp.where(forced, FORCE_SCORE, imp))
    score = jnp.where(blk < n_sel, score, -jnp.inf)
    bias = jnp.full(imp.shape, NEG_INF, f32)
    for _ in range(min(N_SEL, n_sel)):
        mx = jnp.max(score, axis=0, keepdims=True)
        idx = jnp.min(jnp.where(score == mx, blk, 2 ** 30), axis=0, keepdims=True)
        pick = blk == idx
        bias = jnp.where(pick, 0.0, bias)
        score = jnp.where(pick, -jnp.inf, score)
    return bias


def _block_bias(bias_ref, first_block, n_blocks, cols):
    return jnp.concatenate(
        [jnp.broadcast_to(bias_ref[pl.ds(first_block + b, 1), :], (SEL_BLOCK, cols)) for b in range(n_blocks)], axis=0)


def _flash_step(s, vt, m_ref, l_ref, acc_ref):
    m_old = m_ref[...]
    m_new = jnp.maximum(m_old, jnp.max(s, axis=0, keepdims=True))
    alpha = jnp.exp(m_old - m_new)
    p = jnp.exp(s - m_new)
    l_ref[...] = alpha * l_ref[...] + jnp.sum(p, axis=0, keepdims=True)
    acc_ref[...] = alpha * acc_ref[...] + _dot(vt, p.astype(bf16))
    m_ref[...] = m_new


def _flash_init(m_ref, l_ref, acc_ref):
    m_ref[...] = jnp.full(m_ref.shape, NEG_INF, f32)
    l_ref[...] = jnp.zeros(l_ref.shape, f32)
    acc_ref[...] = jnp.zeros(acc_ref.shape, f32)


def _cmp_prompt_kernel(k_ref, v_ref, pe_ref, w_ref, kc_ref, vct_ref):
    rows = kc_ref.shape[0]
    kc_ref[...] = _compress_half(k_ref, pe_ref, w_ref, 0, rows).astype(bf16)
    vct_ref[...] = _compress_half(v_ref, pe_ref, w_ref, 1, rows).T.astype(bf16)


def _cmp_prompt(kc3, vc3, cw):
    b_sz, seq_len, _ = kc3.shape
    r = seq_len // CMP_STRIDE
    pe, w = cw
    return pl.pallas_call(
        _cmp_prompt_kernel, grid=(b_sz,),
        in_specs=[pl.BlockSpec((None, seq_len, FEAT), lambda b: (b, 0, 0)), pl.BlockSpec((None, seq_len, FEAT), lambda b: (b, 0, 0)),
                  pl.BlockSpec(pe.shape, lambda b: (0, 0, 0, 0)), pl.BlockSpec(w.shape, lambda b: (0, 0, 0, 0))],
        out_specs=[pl.BlockSpec((None, r, FEAT), lambda b: (b, 0, 0)), pl.BlockSpec((None, FEAT, r), lambda b: (b, 0, 0))],
        out_shape=[SDS((b_sz, r, FEAT), bf16), SDS((b_sz, FEAT, r), bf16)],
        compiler_params=_cp("arbitrary"), name="cmp_prompt",
    )(kc3, vc3, pe, w)


def _nsa_prompt_kernel(q_ref, gate_ref, kc_ref, vct_ref, ksk_ref, ksvt_ref, kwk_ref, kwvt_ref, psel_ref, mt_ref,
                       o_ref, bias_ref, m_ref, l_ref, acc_ref, *, qb, seq_len, tk, tkp, wk):
    i = pl.program_id(1)
    start = i * qb
    nc = NSA_HEADS * qb
    n_sel = seq_len // SEL_BLOCK
    n_cmp = seq_len // CMP_STRIDE - CMP_LEN // CMP_STRIDE + 1
    q = q_ref[...]
    qzt = jnp.concatenate([_dot_nt(psel_ref[hg], q) for hg in range(NSA_HEADS)], axis=1)
    qzt = (qzt * (NSA_HD ** -0.5)).astype(bf16)
    t_col = start + lax.broadcasted_iota(i32, (1, nc), 1) % qb

    s = _dot(kc_ref[...], qzt)
    n_io = lax.broadcasted_iota(i32, s.shape, 0)
    valid = (n_io * CMP_STRIDE + (CMP_LEN - 1) <= t_col) & (n_io < n_cmp)
    pc, inv_c = _col_softmax(s, valid)
    o_c = _dot(vct_ref[...], pc) * inv_c
    imp_all = _dot(mt_ref[...], pc) * inv_c
    imp = []
    for h in range(NSA_KV_HEADS):
        tot = imp_all[:, h * NSA_GROUP * qb:(h * NSA_GROUP + 1) * qb]
        for g in range(1, NSA_GROUP):
            tot = tot + imp_all[:, (h * NSA_GROUP + g) * qb:(h * NSA_GROUP + g + 1) * qb]
        imp.append(tot)
    bias = _select_bias(jnp.concatenate(imp, axis=1), t_col[:, :NSA_KV_HEADS * qb], n_sel)
    bias_ref[...] = jnp.concatenate([bias[:, h * qb:(h + 1) * qb] for h in range(NSA_KV_HEADS) for _ in range(NSA_GROUP)], axis=1)

    _flash_init(m_ref, l_ref, acc_ref)

    def scores(jt, width):
        k0 = pl.multiple_of(jt * width, width)
        bias = _block_bias(bias_ref, jt * (width // SEL_BLOCK), width // SEL_BLOCK, nc)
        return k0, _dot(ksk_ref[pl.ds(k0, width), :], qzt) + bias

    def past_tile(jt, carry):
        k0, s = scores(jt, tkp)
        _flash_step(s, ksvt_ref[:, pl.ds(k0, tkp)], m_ref, l_ref, acc_ref)
        return carry

    def causal_tile(jt, carry):
        k0, s = scores(jt, tk)
        pos = k0 + lax.broadcasted_iota(i32, s.shape, 0)
        _flash_step(jnp.where(pos <= t_col, s, NEG_INF), ksvt_ref[:, pl.ds(k0, tk)], m_ref, l_ref, acc_ref)
        return carry

    n_past = start // tkp
    lax.fori_loop(0, n_past, past_tile, 0)
    lax.fori_loop(n_past * (tkp // tk), (start + qb + tk - 1) // tk, causal_tile, 0)
    o_s = acc_ref[...] * (1.0 / jnp.maximum(l_ref[...], 1e-30))

    ks = pl.multiple_of(jnp.clip(start - WINDOW, 0, seq_len - wk), qb)
    s = _dot(kwk_ref[pl.ds(ks, wk), :], qzt)
    dpos = t_col - (ks + lax.broadcasted_iota(i32, s.shape, 0))
    pw, inv_w = _col_softmax(s, lax.bitcast_convert_type(dpos, jnp.uint32) < WINDOW, every_column_valid=True)
    o_w = _dot(kwvt_ref[:, pl.ds(ks, wk)], pw) * inv_w

    gt = jax.nn.sigmoid(gate_ref[...]).T
    out = jnp.zeros((qb, NSA_WIDTH), f32)
    for hg in range(NSA_HEADS):
        cs = slice(hg * qb, (hg + 1) * qb)
        mix = gt[3 * hg:3 * hg + 1] * o_c[:, cs] + gt[3 * hg + 1:3 * hg + 2] * o_s[:, cs] + gt[3 * hg + 2:3 * hg + 3] * o_w[:, cs]
        out = out + _dot(mix.T.astype(bf16), psel_ref[hg])
    o_ref[...] = out.astype(o_ref.dtype)


def _nsa_prompt(nq3, ng3, kc, vct, ksk, ksvt, kwk, kwvt):
    b_sz, seq_len, _ = nq3.shape
    qb = math.gcd(seq_len, NSA_QBLOCK)
    tk = 2 * qb
    tkp = 2 * tk if seq_len % (2 * tk) == 0 else tk
    assert seq_len % tk == 0 and tk % SEL_BLOCK == 0
    wk = min(WINDOW + qb, seq_len)
    r = kc.shape[1]
    n_sel = seq_len // SEL_BLOCK
    n_cmp = seq_len // CMP_STRIDE - CMP_LEN // CMP_STRIDE + 1
    psel = jnp.asarray(_np_psel(), bf16)
    mt = jnp.asarray(_np_cmp_to_sel_t(n_cmp, n_sel, n_sel, r), bf16)
    nc = NSA_HEADS * qb
    per_seq = lambda s: pl.BlockSpec((None,) + s, lambda b, i: (b, 0, 0))
    return pl.pallas_call(
        functools.partial(_nsa_prompt_kernel, qb=qb, seq_len=seq_len, tk=tk, tkp=tkp, wk=wk),
        grid=(b_sz, seq_len // qb),
        in_specs=[pl.BlockSpec((None, qb, NSA_WIDTH), lambda b, i: (b, i, 0)),
                  pl.BlockSpec((None, qb, LANES), lambda b, i: (b, i, 0)),
                  per_seq((r, FEAT)), per_seq((FEAT, r)),
                  per_seq((seq_len, FEAT)), per_seq((FEAT, seq_len)),
                  per_seq((seq_len, FEAT)), per_seq((FEAT, seq_len)),
                  pl.BlockSpec((NSA_HEADS, FEAT, NSA_WIDTH), lambda b, i: (0, 0, 0)),
                  pl.BlockSpec((n_sel, r), lambda b, i: (0, 0))],
        out_specs=pl.BlockSpec((None, qb, NSA_WIDTH), lambda b, i: (b, i, 0)),
        out_shape=SDS((b_sz, seq_len, NSA_WIDTH), bf16),
        scratch_shapes=[pltpu.VMEM((n_sel, nc), f32), pltpu.VMEM((1, nc), f32), pltpu.VMEM((1, nc), f32),
                        pltpu.VMEM((FEAT, nc), f32)],
        compiler_params=_cp("arbitrary", "arbitrary"), name="nsa_prompt",
    )(nq3, ng3, kc, vct, ksk, ksvt, kwk, kwvt, psel, mt)


NCS = LANES


def _page_halves(page_ref):
    k = jnp.concatenate([page_ref[0, h] for h in range(NSA_KV_HEADS)], axis=0)
    v = jnp.concatenate([page_ref[1, h] for h in range(NSA_KV_HEADS)], axis=0)
    return k, v


def _nsa_s1_kernel(pt_ref, *refs, n_pages, pps, ts, win_buf, rows):
    pages = refs[:pps]
    (newk_ref, newv_ref, q_ref, gate_ref, swin_ref, kwn_ref, pe_ref, w_ref, psel_ref, mt_ref, gsum_ref,
     qzt_ref, part_ref, bias_ref, xk_ref, xv_ref) = refs[pps:]
    j = pl.program_id(1)
    past = n_pages * PAGE_SIZE
    n_tail = xk_ref.shape[0] - past

    @pl.when(j == 0)
    def _fill_tail():
        for x_ref, new_ref in ((xk_ref, newk_ref), (xv_ref, newv_ref)):
            x_ref[pl.ds(past, n_tail), :] = jnp.zeros((n_tail, FEAT), f32)
            x_ref[pl.ds(past, ts), :] = new_ref[...]

    for k in range(pps):
        r0 = pl.multiple_of((j * pps + k) * PAGE_SIZE, PAGE_SIZE)
        kt, vt = _page_halves(pages[k])
        xk_ref[pl.ds(r0, PAGE_SIZE), :] = kt.T
        xv_ref[pl.ds(r0, PAGE_SIZE), :] = vt.T

    @pl.when(j == pl.num_programs(1) - 1)
    def _finish():
        n_rows = NSA_HEADS * ts
        qz = jnp.concatenate([_dot_nt(q_ref[...], psel_ref[hg]) for hg in range(NSA_HEADS)]
                             + [jnp.zeros((NCS - n_rows, FEAT), f32)], axis=0)
        qzt = (qz.T * (NSA_HD ** -0.5)).astype(bf16)
        qzt_ref[...] = qzt
        t_col = past + lax.broadcasted_iota(i32, (1, NCS), 1) % ts
        total = past + ts
        n_sel = -(-total // SEL_BLOCK)
        n_cmp = n_sel * SEL_BLOCK // CMP_STRIDE - CMP_LEN // CMP_STRIDE + 1

        s = _dot(_compress_half(xk_ref, pe_ref, w_ref, 0, rows).astype(bf16), qzt)
        n_io = lax.broadcasted_iota(i32, s.shape, 0)
        valid = (n_io * CMP_STRIDE + (CMP_LEN - 1) <= t_col) & (n_io < n_cmp)
        pc, inv_c = _col_softmax(s, valid)
        o_c = _dot(_compress_half(xv_ref, pe_ref, w_ref, 1, rows).T.astype(bf16), pc) * inv_c
        imp_all = _dot(mt_ref[...], pc) * inv_c
        hi = imp_all.astype(bf16)
        r1 = imp_all - hi.astype(f32)
        mid = r1.astype(bf16)
        lo = (r1 - mid.astype(f32)).astype(bf16)
        gs = gsum_ref[...]
        imp = _dot(hi, gs) + _dot(mid, gs) + _dot(lo, gs)
        bias_ref[...] = _select_bias(imp, t_col, n_sel)

        wkt, wvt = _page_halves(swin_ref)
        kwn = jnp.concatenate([kwn_ref[...], jnp.zeros((ts, NSA_KV_WIDTH), f32)], axis=0).astype(bf16)
        s = jnp.concatenate([_dot_tn(wkt.astype(bf16), qzt), _dot(kwn[:, :FEAT], qzt)], axis=0)
        w_row = lax.broadcasted_iota(i32, s.shape, 0)
        w_pos = total - (win_buf + ts) + w_row
        dpos = t_col - w_pos
        pw, inv_w = _col_softmax(s, (dpos >= 0) & (dpos < WINDOW) & (w_pos >= 0) & (w_row < win_buf + ts))
        o_w = (_dot(wvt.astype(bf16), pw[:win_buf]) + _dot_tn(kwn[:, FEAT:], pw[win_buf:])) * inv_w

        sig = jax.nn.sigmoid(gate_ref[...])
        oc_r, ow_r = o_c.T, o_w.T
        part_ref[...] = jnp.concatenate(
            [sig[:, 3 * hg:3 * hg + 1] * oc_r[hg * ts:(hg + 1) * ts] + sig[:, 3 * hg + 2:3 * hg + 3] * ow_r[hg * ts:(hg + 1) * ts]
             for hg in range(NSA_HEADS)], axis=0)


def _nsa_s2_kernel(pt_ref, *refs, n_pages, pps, ts):
    pages = refs[:pps]
    news_ref, qzt_ref, bias_ref, part_ref, gate_ref, psel_ref, o_ref, m_ref, l_ref, acc_ref = refs[pps:]
    j = pl.program_id(1)
    past = n_pages * PAGE_SIZE
    qzt = qzt_ref[...]
    bpp = PAGE_SIZE // SEL_BLOCK

    @pl.when(j == 0)
    def _init():
        _flash_init(m_ref, l_ref, acc_ref)

    halves = [_page_halves(pages[k]) for k in range(pps)]
    kt = jnp.concatenate([h[0] for h in halves], axis=1).astype(bf16)
    vt = jnp.concatenate([h[1] for h in halves], axis=1).astype(bf16)
    _flash_step(_dot_tn(kt, qzt) + _block_bias(bias_ref, j * pps * bpp, pps * bpp, NCS), vt, m_ref, l_ref, acc_ref)

    @pl.when(j == pl.num_programs(1) - 1)
    def _finish():
        t_col = past + lax.broadcasted_iota(i32, (1, NCS), 1) % ts
        new = jnp.concatenate([news_ref[...], jnp.zeros((PAGE_SIZE - ts, NSA_KV_WIDTH), f32)], axis=0)
        s = _dot(new[:, :FEAT].astype(bf16), qzt) + _block_bias(bias_ref, n_pages * bpp, bpp, NCS)
        pos = past + lax.broadcasted_iota(i32, s.shape, 0)
        _flash_step(jnp.where(pos <= t_col, s, NEG_INF), new[:, FEAT:].T.astype(bf16), m_ref, l_ref, acc_ref)
        o_s = (acc_ref[...] * (1.0 / jnp.maximum(l_ref[...], 1e-30))).T
        sig = jax.nn.sigmoid(gate_ref[...])
        part = part_ref[...]
        out = jnp.zeros((ts, NSA_WIDTH), f32)
        for hg in range(NSA_HEADS):
            rs = slice(hg * ts, (hg + 1) * ts)
            mix = part[rs] + sig[:, 3 * hg + 1:3 * hg + 2] * o_s[rs]
            out = out + _dot(mix.astype(bf16), psel_ref[hg])
        o_ref[...] = out.astype(o_ref.dtype)


def _nsa_sample(nq3, ng3, kc3, vc3, kvs3, kvw3, cache_c_t, cache_s_t, swin_t, page_table, cw):
    bs, ts, _ = nq3.shape
    n_pages = page_table.shape[1]
    win_buf = swin_t.shape[-1]
    assert ts == SUBLANES and PAGE_SIZE % SEL_BLOCK == 0 and NSA_HEADS * ts <= NCS
    pps = math.gcd(n_pages, 8)
    steps = n_pages // pps
    past = n_pages * PAGE_SIZE
    total = past + ts
    n_sel = -(-total // SEL_BLOCK)
    n_cmp = n_sel * SEL_BLOCK // CMP_STRIDE - CMP_LEN // CMP_STRIDE + 1
    rows = -(-(n_cmp + 1) // SUBLANES) * SUBLANES
    x_rows = rows * CMP_STRIDE
    rb = -(-(n_sel + 1) // SUBLANES) * SUBLANES
    psel = jnp.asarray(_np_psel(), bf16)
    mt = jnp.asarray(_np_cmp_to_sel_t(n_cmp, n_sel, rb, rows), bf16)
    col = np.arange(NCS)
    gsum = ((col[:, None] // (NSA_GROUP * ts) == col[None, :] // (NSA_GROUP * ts)) & (col[:, None] % ts == col[None, :] % ts))
    gsum = jnp.asarray(gsum.astype(np.float32), bf16)
    pt_flat = page_table.reshape(-1).astype(i32)
    pe, w = cw
    page_shape = (2, NSA_KV_HEADS, NSA_HD, PAGE_SIZE)

    def page_spec(k):
        return pl.BlockSpec((None,) + page_shape, lambda b, j, pt, k=k: (pt[b * n_pages + j * pps + k], 0, 0, 0, 0))

    per_seq = lambda s: pl.BlockSpec((None,) + s, lambda b, j, pt: (b,) + (0,) * len(s))
    const = lambda s: pl.BlockSpec(s, lambda b, j, pt: (0,) * len(s))

    qzt, part, bias = pl.pallas_call(
        functools.partial(_nsa_s1_kernel, n_pages=n_pages, pps=pps, ts=ts, win_buf=win_buf, rows=rows),
        grid_spec=pltpu.PrefetchScalarGridSpec(
            num_scalar_prefetch=1, grid=(bs, steps),
            in_specs=[page_spec(k) for k in range(pps)]
            + [per_seq((ts, FEAT)), per_seq((ts, FEAT)), per_seq((ts, NSA_WIDTH)), per_seq((ts, LANES)),
               per_seq((2, NSA_KV_HEADS, NSA_HD, win_buf)), per_seq((ts, NSA_KV_WIDTH)), const(pe.shape), const(w.shape),
               const((NSA_HEADS, FEAT, NSA_WIDTH)), const((rb, rows)), const((NCS, NCS))],
            out_specs=[per_seq((FEAT, NCS)), per_seq((NSA_HEADS * ts, FEAT)), per_seq((rb, NCS))],
            scratch_shapes=[pltpu.VMEM((x_rows, FEAT), f32), pltpu.VMEM((x_rows, FEAT), f32)]),
        out_shape=[SDS((bs, FEAT, NCS), bf16), SDS((bs, NSA_HEADS * ts, FEAT), f32), SDS((bs, rb, NCS), f32)],
        compiler_params=_cp("arbitrary", "arbitrary"), name="nsa_sample_cmp",
    )(pt_flat, *([cache_c_t] * pps), kc3, vc3, nq3, ng3, swin_t, kvw3, pe, w, psel, mt, gsum)

    return pl.pallas_call(
        functools.partial(_nsa_s2_kernel, n_pages=n_pages, pps=pps, ts=ts),
        grid_spec=pltpu.PrefetchScalarGridSpec(
            num_scalar_prefetch=1, grid=(bs, steps),
            in_specs=[page_spec(k) for k in range(pps)]
            + [per_seq((ts, NSA_KV_WIDTH)), per_seq((FEAT, NCS)), per_seq((rb, NCS)), per_seq((NSA_HEADS * ts, FEAT)),
               per_seq((ts, LANES)), const((NSA_HEADS, FEAT, NSA_WIDTH))],
            out_specs=per_seq((ts, NSA_WIDTH)),
            scratch_shapes=[pltpu.VMEM((1, NCS), f32), pltpu.VMEM((1, NCS), f32), pltpu.VMEM((FEAT, NCS), f32)]),
        out_shape=SDS((bs, ts, NSA_WIDTH), bf16),
        compiler_params=_cp("arbitrary", "arbitrary"), name="nsa_sample_sel",
    )(pt_flat, *([cache_s_t] * pps), kvs3, qzt, bias, part, ng3, psel)


def _outproj_kernel(gla_ref, nsa_ref, x_ref, gm_ref, sh_ref, sc_ref, gf_ref, wo_ref, wrt_ref, br_ref,
                    x1_ref, h2_ref, idx_ref, tw_ref):
    y = _dot(gla_ref[...], wo_ref[:GLA_WIDTH, :]) + _dot(nsa_ref[...], wo_ref[GLA_WIDTH:, :])
    x1 = x_ref[...] + gm_ref[...] * y
    x1_ref[...] = x1
    h2 = _rms(x1, gf_ref[...]) * (1.0 + sc_ref[...]) + sh_ref[...]
    h2_ref[...] = h2
    s = _dot_nt(wrt_ref[...], h2.astype(bf16)) + br_ref[...]
    e_io = lax.broadcasted_iota(i32, s.shape, 0)
    vals, idxs = [], []
    for _ in range(TOP_K):
        mx = jnp.max(s, axis=0, keepdims=True)
        ix = jnp.min(jnp.where(s == mx, e_io, N_EXPERTS), axis=0, keepdims=True)
        vals.append(mx)
        idxs.append(ix)
        s = jnp.where(e_io == ix, -jnp.inf, s)
    ex = [jnp.exp(v - vals[0]) for v in vals]
    den = ex[0] + ex[1] + ex[2] + ex[3]
    idx_ref[...] = jnp.concatenate(idxs, axis=0)
    tw_ref[...] = jnp.concatenate([e / den for e in ex], axis=0)


def _outproj(gla_o, nsa_o, x2d, mod, g_ffn, w_out_b, w_router_t, b_router):
    t, tm = x2d.shape[0], mod.tm
    row = lambda i: (i, 0)
    const = lambda s: pl.BlockSpec(s, lambda i: (0, 0))
    return pl.pallas_call(
        _outproj_kernel, grid=(t // tm,),
        in_specs=[pl.BlockSpec((tm, GLA_WIDTH), row), pl.BlockSpec((tm, NSA_WIDTH), row), pl.BlockSpec((tm, D_MODEL), row),
                  mod.spec(2), mod.spec(3), mod.spec(4), const((1, D_MODEL)), const((MIX_WIDTH, D_MODEL)),
                  const((N_EXPERTS, D_MODEL)), const((N_EXPERTS, 1))],
        out_specs=[pl.BlockSpec((tm, D_MODEL), row), pl.BlockSpec((tm, D_MODEL), row),
                   pl.BlockSpec((TOP_K, tm), lambda i: (0, i)), pl.BlockSpec((TOP_K, tm), lambda i: (0, i))],
        out_shape=[SDS((t, D_MODEL), f32), SDS((t, D_MODEL), f32), SDS((TOP_K, t), i32), SDS((TOP_K, t), f32)],
        compiler_params=_cp("arbitrary"), name="outproj_router",
    )(gla_o, nsa_o, x2d, mod.arr, mod.arr, mod.arr, g_ffn.reshape(1, D_MODEL), w_out_b, w_router_t,
      b_router.reshape(N_EXPERTS, 1))


def _moe_kernel(be_ref, nv_ref, tokm_hbm, slotm_hbm, h2_hbm, wg_ref, bg_ref, wu_ref, bu_ref, wd_ref, bd_ref, out_hbm,
                tok_sm, slot_sm, xbuf, ybuf, wbuf, tsem, lsem, gsem, ssem):
    i = pl.program_id(0)
    nv = nv_ref[0]
    tm = xbuf.shape[1]
    n_real = out_hbm.shape[0] - tm

    def tok_copy(b):
        return pltpu.make_async_copy(tokm_hbm.at[b], tok_sm, tsem)

    def slot_copy(b):
        return pltpu.make_async_copy(slotm_hbm.at[b], slot_sm, lsem)

    def issue_gather(par):
        for r in range(tm):
            pltpu.make_async_copy(h2_hbm.at[pl.ds(tok_sm[0, r], 1)], xbuf.at[par, pl.ds(r, 1)], gsem.at[par]).start()

    def issue_scatter(par):
        for r in range(tm):
            pltpu.make_async_copy(ybuf.at[par, pl.ds(r, 1)], out_hbm.at[pl.ds(slot_sm[0, r], 1)], ssem.at[par]).start()

    def wait_gather(par):
        pltpu.make_async_copy(h2_hbm.at[pl.ds(0, tm)], xbuf.at[par], gsem.at[par]).wait()

    def wait_scatter(par):
        pltpu.make_async_copy(ybuf.at[par], out_hbm.at[pl.ds(0, tm)], ssem.at[par]).wait()

    @pl.when(i == 0)
    def _prologue():
        tok_copy(0).start()
        slot_copy(0).start()
        ybuf[1] = jnp.zeros(ybuf.shape[1:], f32)
        zero_fill = pltpu.make_async_copy(ybuf.at[1], out_hbm.at[pl.ds(n_real, tm)], ssem.at[1])
        zero_fill.start()
        tok_copy(0).wait()
        issue_gather(0)
        zero_fill.wait()

        @pl.when(nv > 1)
        def _():
            tok_copy(1).start()

    @pl.when(i < nv)
    def _run():
        for par in range(2):
            @pl.when((i + 1 < nv) & ((i + 1) % 2 == par))
            def _next_gather():
                tok_copy(i + 1).wait()
                issue_gather(par)

                @pl.when(i + 2 < nv)
                def _():
                    tok_copy(i + 2).start()

        @pl.when((i == 0) | (be_ref[i] != be_ref[jnp.maximum(i - 1, 0)]))
        def _new_expert():
            wbuf[0] = wg_ref[...].astype(bf16)
            wbuf[1] = wu_ref[...].astype(bf16)
            wbuf[2] = wd_ref[...].astype(bf16)

        wait_gather(i % 2)
        x = xbuf[i % 2].astype(bf16)
        gate = jnp.minimum(_dot(x, wbuf[0]) + bg_ref[...], SWIGLU_LIMIT)
        up = jnp.clip(_dot(x, wbuf[1]) + bu_ref[...], -SWIGLU_LIMIT, SWIGLU_LIMIT)
        act = ((up + 1.0) * gate * jax.nn.sigmoid(SWIGLU_ALPHA * gate)).astype(bf16)
        y = _dot(act, wbuf[2]) + bd_ref[...]

        @pl.when(i >= 2)
        def _free_ybuf():
            wait_scatter(i % 2)

        ybuf[i % 2] = y
        slot_copy(i).wait()
        for par in range(2):
            @pl.when(i % 2 == par)
            def _scatter():
                issue_scatter(par)

        @pl.when(i + 1 < nv)
        def _next_slots():
            slot_copy(i + 1).start()

        @pl.when(i == nv - 1)
        def _drain():
            wait_scatter(i % 2)

            @pl.when(i >= 1)
            def _():
                wait_scatter((i + 1) % 2)


def _moe_experts(h2_all, tok_meta, slot_meta, block_e, n_valid, n_slots, wg, bg, wu, bu, wd, bd):
    n_blocks, _, tm = tok_meta.shape
    assert D_MODEL == D_EXPERT
    any_spec = pl.BlockSpec(memory_space=pl.ANY)
    wspec = lambda a, b: pl.BlockSpec((None, a, b), lambda i, be, nv: (be[i], 0, 0))
    return pl.pallas_call(
        _moe_kernel,
        grid_spec=pltpu.PrefetchScalarGridSpec(
            num_scalar_prefetch=2, grid=(n_blocks,),
            in_specs=[any_spec, any_spec, any_spec,
                      wspec(D_MODEL, D_EXPERT), wspec(1, D_EXPERT), wspec(D_MODEL, D_EXPERT), wspec(1, D_EXPERT),
                      wspec(D_EXPERT, D_MODEL), wspec(1, D_MODEL)],
            out_specs=any_spec,
            scratch_shapes=[pltpu.SMEM((1, tm), i32), pltpu.SMEM((1, tm), i32), pltpu.VMEM((2, tm, D_MODEL), f32),
                            pltpu.VMEM((2, tm, D_MODEL), f32), pltpu.VMEM((3, D_MODEL, D_EXPERT), bf16),
                            pltpu.SemaphoreType.DMA, pltpu.SemaphoreType.DMA,
                            pltpu.SemaphoreType.DMA((2,)), pltpu.SemaphoreType.DMA((2,))]),
        out_shape=SDS((n_slots, D_MODEL), f32),
        compiler_params=pltpu.CompilerParams(dimension_semantics=("arbitrary",), vmem_limit_bytes=MOE_VMEM_LIMIT_BYTES),
        name="moe_experts",
    )(block_e, n_valid, tok_meta, slot_meta, h2_all, wg, bg.reshape(N_EXPERTS, 1, D_EXPERT), wu,
      bu.reshape(N_EXPERTS, 1, D_EXPERT), wd, bd.reshape(N_EXPERTS, 1, D_MODEL))


def _moe_plan(idx_t, tm, t_pad):
    n_tok = idx_t.shape[1]
    n_assign = n_tok * TOP_K
    flat_e = idx_t.reshape(n_assign)
    order = jnp.argsort(flat_e).astype(i32)
    counts = jnp.sum((flat_e[:, None] == jnp.arange(N_EXPERTS, dtype=i32)[None, :]).astype(i32), axis=0)
    padded = (counts + tm - 1) // tm * tm
    start = jnp.cumsum(counts) - counts
    pad_end = jnp.cumsum(padded)
    pad_start = pad_end - padded
    n_blocks = (n_assign + N_EXPERTS * (tm - 1) + tm - 1) // tm
    blk0 = jnp.arange(n_blocks, dtype=i32) * tm
    block_e = jnp.minimum(jnp.sum((pad_end[None, :] <= blk0[:, None]).astype(i32), axis=1), N_EXPERTS - 1).astype(i32)
    n_valid = (pad_end[-1] // tm).astype(i32).reshape(1)
    r_in = jnp.arange(tm, dtype=i32)[None, :]
    j = blk0[:, None] + r_in - pad_start[block_e][:, None]
    valid = (j < counts[block_e][:, None]) & (blk0[:, None] < pad_end[-1])
    a = order[jnp.clip(start[block_e][:, None] + j, 0, n_assign - 1)]
    tok = a % n_tok
    slot = (a // n_tok) * t_pad + tok
    tok_meta = jnp.where(valid, tok, 0).astype(i32).reshape(n_blocks, 1, tm)
    slot_meta = jnp.where(valid, slot, TOP_K * t_pad + r_in).astype(i32).reshape(n_blocks, 1, tm)
    return tok_meta, slot_meta, block_e, n_valid


def _final_kernel(x1_ref, *refs):
    y_refs, (tw_ref, gate_ref, g_ref, o_ref) = refs[:TOP_K], refs[TOP_K:]
    tw = tw_ref[...]
    f = tw[:, 0:1] * y_refs[0][...]
    for k in range(1, TOP_K):
        f = f + tw[:, k:k + 1] * y_refs[k][...]
    o_ref[...] = _rms(x1_ref[...] + gate_ref[...] * f, g_ref[...])


def _final(x1, y4, tw, mod, g_final, row0, t_pad):
    t, tm = x1.shape[0], mod.tm
    assert row0 % tm == 0 and t_pad % tm == 0
    b0 = row0 // tm
    row = lambda i: (i, 0)
    y_specs = [pl.BlockSpec((tm, D_MODEL), lambda i, k=k: (k * (t_pad // tm) + b0 + i, 0)) for k in range(TOP_K)]
    return pl.pallas_call(
        _final_kernel, grid=(t // tm,),
        in_specs=[pl.BlockSpec((tm, D_MODEL), row)] + y_specs
        + [pl.BlockSpec((tm, TOP_K), lambda i: (i + b0, 0)), mod.spec(5), pl.BlockSpec((1, D_MODEL), lambda i: (0, 0))],
        out_specs=pl.BlockSpec((tm, D_MODEL), row), out_shape=SDS((t, D_MODEL), f32),
        compiler_params=_cp("arbitrary"), name="combine_final_norm",
    )(x1, *([y4] * TOP_K), tw, mod.arr, g_final.reshape(1, D_MODEL))


def _prep_w_in(w_in):
    gq, gk, gv, gr, ga, nq, nkc, nks, nkw, ng = jnp.split(w_in, np.cumsum(IN_SPLITS)[:-1].tolist(), axis=1)
    pad = lambda a: jnp.pad(a, ((0, 0), (0, LANES - a.shape[1])))
    w_p = jnp.concatenate([gq, gk, gv, gr, nq, nkc, nks, nkw, pad(ga), pad(ng)], axis=1).astype(bf16)
    w_kv_t = jnp.concatenate([nkc, nks, nkw], axis=1).T.astype(bf16)
    return w_p, w_kv_t


def _feature_major(a, rows_axis):
    return jnp.moveaxis(a, rows_axis, -1)


def kernel(x_prompt, x_sample, c_prompt, c_sample, cache_cmp, cache_sel, state_win, state_gla, page_table, w_ada, b_ada, g_mix, g_ffn, w_in, w_a2, b_a, g_gla, phi_pe, phi_w, w_out, w_router, b_router, w_gate, b_gate, w_up, b_up, w_down, b_down, g_final):
    bp, sp = x_prompt.shape[:2]
    bs, ts = x_sample.shape[:2]
    tp, tsn = bp * sp, bs * ts
    win_buf = state_win.shape[2]
    l = 0

    mod = _adaln(jnp.concatenate([c_prompt, c_sample], axis=0), w_ada[l], b_ada[l])
    tm_p = math.gcd(sp, 512)
    tm_s = math.gcd(tsn, 512)
    mod_p = _Mod(mod[:bp].reshape(bp, 6, 1, D_MODEL), False, tm_p, sp // tm_p)
    mod_s = _Mod(jnp.repeat(mod[bp:].reshape(bs, 6, D_MODEL).transpose(1, 0, 2), ts, axis=1), True, tm_s, None)

    w_in_p, w_kv_t = _prep_w_in(w_in[l])
    cw = _prep_compress(phi_pe[l], phi_w[l])
    w_out_b = w_out[l].astype(bf16)
    w_router_t = w_router[l].T.astype(bf16)
    xp2, xs2 = x_prompt.reshape(tp, D_MODEL), x_sample.reshape(tsn, D_MODEL)

    (gq, gk, gv, gr, nq, kc, vc, _, _, _, _, ga, ng, kvc_t, kvs_t, kvw_t, ksk, kwk, ksvt, kwvt) = _inproj(
        xp2, mod_p, g_mix[l], w_in_p, w_kv_t, sp)
    gla_o, st_p = _gla(gq, gk, gv, gr, ga, jnp.zeros((bp, GLA_HEADS, GLA_DV, GLA_DK), f32), w_a2[l], b_a[l], g_gla[l], bp, sp)
    kcm, vct = _cmp_prompt(kc.reshape(bp, sp, FEAT), vc.reshape(bp, sp, FEAT), cw)
    nsa_o = _nsa_prompt(nq.reshape(bp, sp, NSA_WIDTH), ng.reshape(bp, sp, LANES), kcm, vct,
                        ksk.reshape(bp, sp, FEAT), ksvt, kwk.reshape(bp, sp, FEAT), kwvt)
    x1_p, h2_p, idx_p, tw_p = _outproj(gla_o.reshape(tp, GLA_WIDTH), nsa_o.reshape(tp, NSA_WIDTH), xp2, mod_p,
                                       g_ffn[l], w_out_b, w_router_t, b_router[l])
    token_major = lambda a: jnp.moveaxis(a, -1, 1)[None]
    new_cmp_p = token_major(kvc_t)
    new_sel_p = token_major(kvs_t)
    new_win_p = token_major(jnp.pad(kvw_t, ((0, 0),) * 4 + ((win_buf, 0),))[..., -win_buf:])
    new_gla_p = jnp.swapaxes(st_p, 2, 3)[None]

    gq, gk, gv, gr, nq, kc, vc, ks, vs, kw, vw, ga, ng = _inproj(xs2, mod_s, g_mix[l], w_in_p)
    gla_os, st_s = _gla(gq, gk, gv, gr, ga, jnp.swapaxes(state_gla[l], 2, 3), w_a2[l], b_a[l], g_gla[l], bs, ts)
    r3 = lambda a: a.reshape(bs, ts, a.shape[-1])
    kvc3, kvs3, kvw3 = (jnp.concatenate([r3(k_), r3(v_)], axis=-1) for k_, v_ in ((kc, vc), (ks, vs), (kw, vw)))
    swin_t = _feature_major(state_win[l], 1)
    nsa_os = _nsa_sample(r3(nq), r3(ng), r3(kc), r3(vc), kvs3, kvw3,
                         _feature_major(cache_cmp[l], 1), _feature_major(cache_sel[l], 1), swin_t, page_table, cw)
    x1_s, h2_s, idx_s, tw_s = _outproj(gla_os.reshape(tsn, GLA_WIDTH), nsa_os.reshape(tsn, NSA_WIDTH), xs2, mod_s,
                                       g_ffn[l], w_out_b, w_router_t, b_router[l])
    kv_row = (2, NSA_KV_HEADS, NSA_HD)
    new_cmp_s = kvc3.reshape((1, bs, ts) + kv_row)
    new_sel_s = kvs3.reshape((1, bs, ts) + kv_row)
    kw_t = _feature_major(kvw3.reshape((bs, ts) + kv_row), 1)
    new_win_s = token_major(jnp.concatenate([swin_t, kw_t], axis=-1)[..., -win_buf:])
    new_gla_s = jnp.swapaxes(st_s, 2, 3)[None]

    n_tok = tp + tsn
    t_pad = -(-n_tok // tm_p) * tm_p
    tok_meta, slot_meta, block_e, n_valid = _moe_plan(jnp.concatenate([idx_p, idx_s], axis=1), MOE_TM, t_pad)
    y4 = _moe_experts(jnp.concatenate([h2_p, h2_s], axis=0), tok_meta, slot_meta, block_e, n_valid,
                      TOP_K * t_pad + MOE_TM, w_gate[l], b_gate[l], w_up[l], b_up[l], w_down[l], b_down[l])
    tw_all = jnp.concatenate([tw_p, tw_s], axis=1).T
    y_p = _final(x1_p, y4, tw_all, mod_p, g_final, 0, t_pad).reshape(bp, sp, D_MODEL)
    y_s = _final(x1_s, y4, tw_all, mod_s, g_final, tp, t_pad).reshape(bs, ts, D_MODEL)
    return (y_p, y_s, new_cmp_p, new_sel_p, new_win_p, new_gla_p, new_cmp_s, new_sel_s, new_win_s, new_gla_s)
```

```python
import functools
import math

import numpy as np
import jax
import jax.numpy as jnp
from jax import lax
from jax.experimental import pallas as pl
from jax.experimental.pallas import tpu as pltpu

f32, bf16, i32 = jnp.float32, jnp.bfloat16, jnp.int32
SDS = jax.ShapeDtypeStruct

D_MODEL = 1024
GLA_HEADS, GLA_DK, GLA_DV, GLA_GATE_RANK, GLA_TAU, GLA_CHUNK = 4, 64, 128, 16, 16.0, 64
NSA_HEADS, NSA_KV_HEADS, NSA_HD = 8, 2, 64
NSA_GROUP = NSA_HEADS // NSA_KV_HEADS
CMP_LEN, CMP_STRIDE, SEL_BLOCK, N_SEL, WINDOW, NSA_QBLOCK = 32, 16, 64, 8, 512, 128
N_EXPERTS, TOP_K, D_EXPERT = 32, 4, 1024
SWIGLU_LIMIT, SWIGLU_ALPHA = 7.0, 1.702
RMS_EPS, NEG_INF, FORCE_SCORE = 1e-6, -1e30, 1e4
PAGE_SIZE = 128

GLA_QK_WIDTH = GLA_HEADS * GLA_DK
GLA_WIDTH = GLA_HEADS * GLA_DV
NSA_WIDTH = NSA_HEADS * NSA_HD
NSA_KV_WIDTH = 2 * NSA_KV_HEADS * NSA_HD
IN_SPLITS = (GLA_QK_WIDTH, GLA_QK_WIDTH, GLA_WIDTH, GLA_WIDTH, GLA_GATE_RANK,
             NSA_WIDTH, NSA_KV_WIDTH, NSA_KV_WIDTH, NSA_KV_WIDTH, 3 * NSA_HEADS)
MIX_WIDTH = GLA_WIDTH + NSA_WIDTH

LANES = 128
SUBLANES = 8
VMEM_LIMIT_BYTES = 48 * 1024 * 1024
MOE_VMEM_LIMIT_BYTES = 56 * 1024 * 1024
D_CHUNKS = D_MODEL // LANES

FEAT = NSA_KV_HEADS * NSA_HD
_IN_COLS = (("gq", 0, 256, f32), ("gk", 256, 256, f32), ("gv", 512, 512, bf16), ("gr", 1024, 512, f32),
            ("nq", 1536, 512, bf16), ("kc", 2048, FEAT, f32), ("vc", 2176, FEAT, f32), ("ks", 2304, FEAT, f32),
            ("vs", 2432, FEAT, f32), ("kw", 2560, FEAT, f32), ("vw", 2688, FEAT, f32),
            ("ga", 2816, LANES, f32), ("ng", 2944, LANES, f32))
_IN_WIDTH = 3072
_KV0 = 2048
MOE_TM = 512
Q_SCALE = NSA_HD ** -0.5 * math.log2(math.e)


def _cp(*sem):
    return pltpu.CompilerParams(dimension_semantics=sem, vmem_limit_bytes=VMEM_LIMIT_BYTES)


def _dot(a, b):
    return jnp.dot(a, b, preferred_element_type=f32)


def _dot_nt(a, b):
    return lax.dot_general(a, b, (((1,), (1,)), ((), ())), preferred_element_type=f32)


def _dot_tn(a, b):
    return lax.dot_general(a, b, (((0,), (0,)), ((), ())), preferred_element_type=f32)


def _rms(x, g):
    return x * lax.rsqrt(jnp.mean(x * x, axis=-1, keepdims=True) + RMS_EPS) * g


def _silu(x):
    return x * jax.nn.sigmoid(x)


def _adaln_kernel(c_ref, w_ref, b_ref, o_ref):
    s = _silu(c_ref[...]).astype(bf16)
    o_ref[...] = _dot(s, w_ref[...].astype(bf16)) + b_ref[...]


def _adaln(c_all, w_ada, b_ada):
    nb, n = c_all.shape[0], w_ada.shape[1]
    tn = 1024
    return pl.pallas_call(
        _adaln_kernel, grid=(n // tn,),
        in_specs=[pl.BlockSpec((nb, D_MODEL), lambda j: (0, 0)),
                  pl.BlockSpec((D_MODEL, tn), lambda j: (0, j)),
                  pl.BlockSpec((1, tn), lambda j: (0, j))],
        out_specs=pl.BlockSpec((nb, tn), lambda j: (0, j)),
        out_shape=SDS((nb, n), f32), compiler_params=_cp("arbitrary"), name="adaln",
    )(c_all, w_ada, b_ada.reshape(1, n))


class _Mod:
    def __init__(self, arr, per_token, tm, tiles_per_seq):
        self.arr, self.per_token, self.tm, self.tps = arr, per_token, tm, tiles_per_seq

    def spec(self, k):
        if self.per_token:
            return pl.BlockSpec((None, self.tm, D_MODEL), lambda i: (k, i, 0))
        tps = self.tps
        return pl.BlockSpec((None, None, 1, D_MODEL), lambda i: (i // tps, k, 0, 0))


def _in_cols(feature_major):
    return tuple(c for c in _IN_COLS if not (feature_major and c[0] in ("ks", "vs", "kw", "vw")))


def _inproj_kernel(x_ref, sh_ref, sc_ref, g_ref, w_ref, *rest, feature_major):
    y = _rms(x_ref[...], g_ref[...])
    h = (y * (1.0 + sc_ref[...]) + sh_ref[...]).astype(bf16)
    outs = rest[1:] if feature_major else rest
    cols = _in_cols(feature_major)
    for o_ref, (_, c0, w, _) in zip(outs, cols):
        o_ref[...] = _dot(h, w_ref[:, c0:c0 + w]).astype(o_ref.dtype)
    if feature_major:
        wt_ref = rest[0]
        kvt_refs, (ksk_ref, kwk_ref, ksvt_ref, kwvt_ref) = outs[len(cols):-4], outs[-4:]
        kvt = _dot_nt(wt_ref[...], h)
        for a, kvt_ref in enumerate(kvt_refs):
            for ch in range(2 * NSA_KV_HEADS):
                r0 = a * NSA_KV_WIDTH + ch * NSA_HD
                kvt_ref[ch // NSA_KV_HEADS, ch % NSA_KV_HEADS] = kvt[r0:r0 + NSA_HD]
        ksvt_ref[...] = kvt[NSA_KV_WIDTH + FEAT:2 * NSA_KV_WIDTH].astype(bf16)
        kwvt_ref[...] = kvt[2 * NSA_KV_WIDTH + FEAT:].astype(bf16)
        ksk_ref[...] = _dot(h, w_ref[:, _KV0 + NSA_KV_WIDTH:_KV0 + NSA_KV_WIDTH + FEAT]).astype(bf16)
        kwk_ref[...] = _dot(h, w_ref[:, _KV0 + 2 * NSA_KV_WIDTH:_KV0 + 2 * NSA_KV_WIDTH + FEAT]).astype(bf16)


def _inproj(x2d, mod, g_mix, w_in_p, w_kv_t=None, seq_len=None):
    t, tm = x2d.shape[0], mod.tm
    row = lambda i: (i, 0)
    in_specs = [pl.BlockSpec((tm, D_MODEL), row), mod.spec(0), mod.spec(1),
                pl.BlockSpec((1, D_MODEL), lambda i: (0, 0)),
                pl.BlockSpec((D_MODEL, _IN_WIDTH), lambda i: (0, 0))]
    cols = _in_cols(w_kv_t is not None)
    out_specs = [pl.BlockSpec((tm, w), row) for (_, _, w, _) in cols]
    out_shape = [SDS((t, w), dt) for (_, _, w, dt) in cols]
    args = [x2d, mod.arr, mod.arr, g_mix.reshape(1, D_MODEL), w_in_p]
    if w_kv_t is not None:
        b_sz, tps = t // seq_len, seq_len // tm
        in_specs.append(pl.BlockSpec((3 * NSA_KV_WIDTH, D_MODEL), lambda i: (0, 0)))
        args.append(w_kv_t)
        fm = lambda w: pl.BlockSpec((None, w, tm), lambda i: (i // tps, 0, i % tps))
        out_specs += [pl.BlockSpec((None, 2, NSA_KV_HEADS, NSA_HD, tm), lambda i: (i // tps, 0, 0, 0, i % tps))] * 3
        out_specs += [pl.BlockSpec((tm, FEAT), row), pl.BlockSpec((tm, FEAT), row), fm(FEAT), fm(FEAT)]
        out_shape += [SDS((b_sz, 2, NSA_KV_HEADS, NSA_HD, seq_len), f32)] * 3
        out_shape += [SDS((t, FEAT), bf16), SDS((t, FEAT), bf16),
                      SDS((b_sz, FEAT, seq_len), bf16), SDS((b_sz, FEAT, seq_len), bf16)]
    return pl.pallas_call(
        functools.partial(_inproj_kernel, feature_major=w_kv_t is not None), grid=(t // tm,),
        in_specs=in_specs, out_specs=out_specs, out_shape=out_shape,
        compiler_params=_cp("arbitrary"), name="inproj",
    )(*args)


def _gla_kernel(q_ref, k_ref, v_ref, r_ref, a_ref, s0_ref, wa2_ref, ba_ref, gg_ref, o_ref, st_ref, *, chunk, n_chunks):
    @pl.when(pl.program_id(1) == 0)
    def _init():
        st_ref[...] = s0_ref[...]

    ri = lax.broadcasted_iota(i32, (chunk, chunk), 0)
    ci = lax.broadcasted_iota(i32, (chunk, chunk), 1)
    causal = ri >= ci
    tril = jnp.where(causal, 1.0, 0.0).astype(bf16)

    def body(c, carry):
        rows = pl.ds(pl.multiple_of(c * chunk, chunk), chunk)
        for sq in range(q_ref.shape[0]):
            a_low = a_ref[sq, rows, :][:, :GLA_GATE_RANK].astype(bf16)
            z = _dot(a_low, wa2_ref[...]) + ba_ref[...]
            log_a = (jnp.minimum(z, 0.0) - jnp.log1p(jnp.exp(-jnp.abs(z)))) * (1.0 / GLA_TAU)
            hi = log_a.astype(bf16)
            lo = (log_a - hi.astype(f32)).astype(bf16)
            cum = _dot(tril, hi) + _dot(tril, lo)
            last = cum[chunk - 1:chunk, :]
            q = q_ref[sq, rows, :] * (GLA_DK ** -0.5)
            k = k_ref[sq, rows, :]
            qd = (q * jnp.exp(cum)).astype(bf16)
            ki = (k * jnp.exp(-cum)).astype(bf16)
            ke = (k * jnp.exp(last - cum)).astype(bf16)
            dec = jnp.exp(last)
            v = v_ref[sq, rows, :]
            r = r_ref[sq, rows, :]
            for h in range(GLA_HEADS):
                sk = slice(h * GLA_DK, (h + 1) * GLA_DK)
                sv = slice(h * GLA_DV, (h + 1) * GLA_DV)
                qh, kih, keh, vh = qd[:, sk], ki[:, sk], ke[:, sk], v[:, sv]
                att = jnp.where(causal, _dot_nt(qh, kih), 0.0)
                st = st_ref[sq, h]
                o = _dot(att.astype(bf16), vh) + _dot_nt(qh, st.astype(bf16))
                st_ref[sq, h] = st * dec[:, sk] + _dot_tn(vh, keh)
                o_ref[sq, rows, sv] = (_rms(o, gg_ref[...]) * _silu(r[:, sv])).astype(o_ref.dtype)
        return carry

    lax.fori_loop(0, n_chunks, body, 0)


def _gla(gq, gk, gv, gr, ga, s0t, w_a2, b_a, g_gla, b_sz, seq_len):
    chunk = math.gcd(seq_len, GLA_CHUNK)
    tb = min(seq_len, 512)
    n_chunks = tb // chunk
    nb = math.gcd(b_sz, 4 if tb >= 512 else 8)
    r3 = lambda a: a.reshape(b_sz, seq_len, a.shape[-1])
    tok = lambda w: pl.BlockSpec((nb, tb, w), lambda b, j: (b, j, 0))
    st_spec = pl.BlockSpec((nb, GLA_HEADS, GLA_DV, GLA_DK), lambda b, j: (b, 0, 0, 0))
    const = lambda s: pl.BlockSpec(s, lambda b, j: (0, 0))
    return pl.pallas_call(
        functools.partial(_gla_kernel, chunk=chunk, n_chunks=n_chunks),
        grid=(b_sz // nb, seq_len // tb),
        in_specs=[tok(GLA_QK_WIDTH), tok(GLA_QK_WIDTH), tok(GLA_WIDTH), tok(GLA_WIDTH), tok(LANES), st_spec,
                  const((GLA_GATE_RANK, GLA_QK_WIDTH)), const((1, GLA_QK_WIDTH)), const((1, GLA_DV))],
        out_specs=[tok(GLA_WIDTH), st_spec],
        out_shape=[SDS((b_sz, seq_len, GLA_WIDTH), bf16), SDS((b_sz, GLA_HEADS, GLA_DV, GLA_DK), f32)],
        compiler_params=_cp("arbitrary", "arbitrary"), name="gla",
    )(r3(gq), r3(gk), r3(gv), r3(gr), r3(ga), s0t, w_a2.astype(bf16), b_a.reshape(1, -1), g_gla.reshape(1, -1))


def _np_psel():
    p = np.zeros((NSA_HEADS, FEAT, NSA_WIDTH), np.float32)
    for h in range(NSA_KV_HEADS):
        for g in range(NSA_GROUP):
            for d in range(NSA_HD):
                p[h * NSA_GROUP + g, h * NSA_HD + d, (h * NSA_GROUP + g) * NSA_HD + d] = 1.0
    return p


def _np_cmp_to_sel_t(n_cmp, n_sel, rows, cols):
    i0 = np.arange(n_cmp)[None, :] * CMP_STRIDE
    j0 = np.arange(n_sel)[:, None] * SEL_BLOCK
    m = np.zeros((rows, cols), np.float32)
    m[:n_sel, :n_cmp] = ((i0 < j0 + SEL_BLOCK) & (i0 + CMP_LEN > j0)).astype(np.float32)
    return m


def _prep_compress(phi_pe, phi_w):
    w = phi_w.reshape(2, CMP_LEN, NSA_HD, NSA_HD)
    wb = jnp.einsum('cldo,hk->clhdko', w, jnp.eye(NSA_KV_HEADS, dtype=f32)).reshape(2, CMP_LEN, FEAT, FEAT)
    pe = jnp.broadcast_to(phi_pe[:, :, None, :], (2, CMP_LEN, NSA_KV_HEADS, NSA_HD)).reshape(2, CMP_LEN, 1, FEAT)
    return pe, wb.astype(bf16)


def _compress_half(x_ref, pe_ref, w_ref, c, rows):
    top = jnp.zeros((rows, FEAT), f32)
    bot = jnp.zeros((rows, FEAT), f32)
    for l in range(0, CMP_STRIDE, 2):
        xa = x_ref[pl.ds(l, rows, stride=CMP_STRIDE), :]
        xb = x_ref[pl.ds(l + 1, rows, stride=CMP_STRIDE), :]
        for acc_is_top, l0 in ((True, l), (False, CMP_STRIDE + l)):
            x2 = jnp.concatenate([xa + pe_ref[c, l0], xb + pe_ref[c, l0 + 1]], axis=1).astype(bf16)
            w2 = w_ref[c, pl.ds(l0, 2)].reshape(2 * FEAT, FEAT)
            if acc_is_top:
                top = top + _dot(x2, w2)
            else:
                bot = bot + _dot(x2, w2)
    return top + pltpu.roll(bot, rows - 1, 0)


def _col_softmax(s, valid, every_column_valid=False):
    s = jnp.where(valid, s, NEG_INF)
    m = jnp.max(s, axis=0, keepdims=True)
    p = jnp.exp2(s - m)
    if not every_column_valid:
        p = jnp.where(valid, p, 0.0)
    l = jnp.sum(p, axis=0, keepdims=True)
    return p.astype(bf16), 1.0 / jnp.maximum(l, 1e-30)


def _select_bias(imp, t_col, n_sel):
    blk = lax.broadcasted_iota(i32, imp.shape, 0)
    cur = lax.shift_right_logical(t_col, int(math.log2(SEL_BLOCK)))
    forced = (blk == 0) | (blk == cur) | (blk == cur - 1)
    future = blk * SEL_BLOCK > t_col
    score = jnp.where(future, NEG_INF, jnp.where(forced, FORCE_SCORE, imp))
    score = jnp.where(blk < n_sel, score, -jnp.inf)
    bias = jnp.full(imp.shape, NEG_INF, f32)
    for _ in range(min(N_SEL, n_sel)):
        mx = jnp.max(score, axis=0, keepdims=True)
        idx = jnp.min(jnp.where(score == mx, blk, 2 ** 30), axis=0, keepdims=True)
        pick = blk == idx
        bias = jnp.where(pick, 0.0, bias)
        score = jnp.where(pick, -jnp.inf, score)
    return bias


def _block_bias(bias_ref, first_block, n_blocks, cols):
    return jnp.concatenate(
        [jnp.broadcast_to(bias_ref[pl.ds(first_block + b, 1), :], (SEL_BLOCK, cols)) for b in range(n_blocks)], axis=0)


def _flash_step(s, vt, m_ref, l_ref, acc_ref):
    m_old = m_ref[...]
    m_new = jnp.maximum(m_old, jnp.max(s, axis=0, keepdims=True))
    alpha = jnp.exp2(m_old - m_new)
    p = jnp.exp2(s - m_new)
    l_ref[...] = alpha * l_ref[...] + jnp.sum(p, axis=0, keepdims=True)
    acc_ref[...] = alpha * acc_ref[...] + _dot(vt, p.astype(bf16))
    m_ref[...] = m_new


def _flash_init(m_ref, l_ref, acc_ref):
    m_ref[...] = jnp.full(m_ref.shape, NEG_INF, f32)
    l_ref[...] = jnp.zeros(l_ref.shape, f32)
    acc_ref[...] = jnp.zeros(acc_ref.shape, f32)


def _cmp_prompt_kernel(k_ref, v_ref, pe_ref, w_ref, kc_ref, vct_ref):
    rows = kc_ref.shape[0]
    kc_ref[...] = _compress_half(k_ref, pe_ref, w_ref, 0, rows).astype(bf16)
    vct_ref[...] = _compress_half(v_ref, pe_ref, w_ref, 1, rows).T.astype(bf16)


def _cmp_prompt(kc3, vc3, cw):
    b_sz, seq_len, _ = kc3.shape
    r = seq_len // CMP_STRIDE
    pe, w = cw
    return pl.pallas_call(
        _cmp_prompt_kernel, grid=(b_sz,),
        in_specs=[pl.BlockSpec((None, seq_len, FEAT), lambda b: (b, 0, 0)), pl.BlockSpec((None, seq_len, FEAT), lambda b: (b, 0, 0)),
                  pl.BlockSpec(pe.shape, lambda b: (0, 0, 0, 0)), pl.BlockSpec(w.shape, lambda b: (0, 0, 0, 0))],
        out_specs=[pl.BlockSpec((None, r, FEAT), lambda b: (b, 0, 0)), pl.BlockSpec((None, FEAT, r), lambda b: (b, 0, 0))],
        out_shape=[SDS((b_sz, r, FEAT), bf16), SDS((b_sz, FEAT, r), bf16)],
        compiler_params=_cp("arbitrary"), name="cmp_prompt",
    )(kc3, vc3, pe, w)


def _nsa_prompt_kernel(q_ref, gate_ref, kc_ref, vct_ref, ksk_ref, ksvt_ref, kwk_ref, kwvt_ref, psel_ref, mt_ref,
                       o_ref, bias_ref, m_ref, l_ref, acc_ref, *, qb, seq_len, tk, tkp, wk):
    i = pl.program_id(1)
    start = i * qb
    nc = NSA_HEADS * qb
    n_sel = seq_len // SEL_BLOCK
    n_cmp = seq_len // CMP_STRIDE - CMP_LEN // CMP_STRIDE + 1
    q = q_ref[...]
    qzt = jnp.concatenate([_dot_nt(psel_ref[hg], q) for hg in range(NSA_HEADS)], axis=1)
    qzt = (qzt * Q_SCALE).astype(bf16)
    t_col = start + lax.broadcasted_iota(i32, (1, nc), 1) % qb

    s = _dot(kc_ref[...], qzt)
    n_io = lax.broadcasted_iota(i32, s.shape, 0)
    valid = (n_io * CMP_STRIDE + (CMP_LEN - 1) <= t_col) & (n_io < n_cmp)
    pc, inv_c = _col_softmax(s, valid)
    o_c = _dot(vct_ref[...], pc) * inv_c
    imp_all = _dot(mt_ref[...], pc) * inv_c
    imp = []
    for h in range(NSA_KV_HEADS):
        tot = imp_all[:, h * NSA_GROUP * qb:(h * NSA_GROUP + 1) * qb]
        for g in range(1, NSA_GROUP):
            tot = tot + imp_all[:, (h * NSA_GROUP + g) * qb:(h * NSA_GROUP + g + 1) * qb]
        imp.append(tot)
    bias = _select_bias(jnp.concatenate(imp, axis=1), t_col[:, :NSA_KV_HEADS * qb], n_sel)
    bias_ref[...] = jnp.concatenate([bias[:, h * qb:(h + 1) * qb] for h in range(NSA_KV_HEADS) for _ in range(NSA_GROUP)], axis=1)

    _flash_init(m_ref, l_ref, acc_ref)

    def scores(jt, width):
        k0 = pl.multiple_of(jt * width, width)
        bias = _block_bias(bias_ref, jt * (width // SEL_BLOCK), width // SEL_BLOCK, nc)
        return k0, _dot(ksk_ref[pl.ds(k0, width), :], qzt) + bias

    def past_tile(jt, carry):
        k0, s = scores(jt, tkp)
        _flash_step(s, ksvt_ref[:, pl.ds(k0, tkp)], m_ref, l_ref, acc_ref)
        return carry

    def causal_tile(jt, carry):
        k0, s = scores(jt, tk)
        pos = k0 + lax.broadcasted_iota(i32, s.shape, 0)
        _flash_step(jnp.where(pos <= t_col, s, NEG_INF), ksvt_ref[:, pl.ds(k0, tk)], m_ref, l_ref, acc_ref)
        return carry

    n_past = start // tkp
    lax.fori_loop(0, n_past, past_tile, 0)
    lax.fori_loop(n_past * (tkp // tk), (start + qb + tk - 1) // tk, causal_tile, 0)
    o_s = acc_ref[...] * (1.0 / jnp.maximum(l_ref[...], 1e-30))

    ks = pl.multiple_of(jnp.clip(start - WINDOW, 0, seq_len - wk), qb)
    s = _dot(kwk_ref[pl.ds(ks, wk), :], qzt)
    dpos = t_col - (ks + lax.broadcasted_iota(i32, s.shape, 0))
    pw, inv_w = _col_softmax(s, lax.bitcast_convert_type(dpos, jnp.uint32) < WINDOW, every_column_valid=True)
    o_w = _dot(kwvt_ref[:, pl.ds(ks, wk)], pw) * inv_w

    gt = jax.nn.sigmoid(gate_ref[...]).T
    out = jnp.zeros((qb, NSA_WIDTH), f32)
    for hg in range(NSA_HEADS):
        cs = slice(hg * qb, (hg + 1) * qb)
        mix = gt[3 * hg:3 * hg + 1] * o_c[:, cs] + gt[3 * hg + 1:3 * hg + 2] * o_s[:, cs] + gt[3 * hg + 2:3 * hg + 3] * o_w[:, cs]
        out = out + _dot(mix.T.astype(bf16), psel_ref[hg])
    o_ref[...] = out.astype(o_ref.dtype)


def _nsa_prompt(nq3, ng3, kc, vct, ksk, ksvt, kwk, kwvt):
    b_sz, seq_len, _ = nq3.shape
    qb = math.gcd(seq_len, NSA_QBLOCK)
    tk = 2 * qb
    tkp = 2 * tk if seq_len % (2 * tk) == 0 else tk
    assert seq_len % tk == 0 and tk % SEL_BLOCK == 0
    wk = min(WINDOW + qb, seq_len)
    r = kc.shape[1]
    n_sel = seq_len // SEL_BLOCK
    n_cmp = seq_len // CMP_STRIDE - CMP_LEN // CMP_STRIDE + 1
    psel = jnp.asarray(_np_psel(), bf16)
    mt = jnp.asarray(_np_cmp_to_sel_t(n_cmp, n_sel, n_sel, r), bf16)
    nc = NSA_HEADS * qb
    per_seq = lambda s: pl.BlockSpec((None,) + s, lambda b, i: (b, 0, 0))
    return pl.pallas_call(
        functools.partial(_nsa_prompt_kernel, qb=qb, seq_len=seq_len, tk=tk, tkp=tkp, wk=wk),
        grid=(b_sz, seq_len // qb),
        in_specs=[pl.BlockSpec((None, qb, NSA_WIDTH), lambda b, i: (b, i, 0)),
                  pl.BlockSpec((None, qb, LANES), lambda b, i: (b, i, 0)),
                  per_seq((r, FEAT)), per_seq((FEAT, r)),
                  per_seq((seq_len, FEAT)), per_seq((FEAT, seq_len)),
                  per_seq((seq_len, FEAT)), per_seq((FEAT, seq_len)),
                  pl.BlockSpec((NSA_HEADS, FEAT, NSA_WIDTH), lambda b, i: (0, 0, 0)),
                  pl.BlockSpec((n_sel, r), lambda b, i: (0, 0))],
        out_specs=pl.BlockSpec((None, qb, NSA_WIDTH), lambda b, i: (b, i, 0)),
        out_shape=SDS((b_sz, seq_len, NSA_WIDTH), bf16),
        scratch_shapes=[pltpu.VMEM((n_sel, nc), f32), pltpu.VMEM((1, nc), f32), pltpu.VMEM((1, nc), f32),
                        pltpu.VMEM((FEAT, nc), f32)],
        compiler_params=_cp("arbitrary", "arbitrary"), name="nsa_prompt",
    )(nq3, ng3, kc, vct, ksk, ksvt, kwk, kwvt, psel, mt)


NCS = LANES


def _page_halves(page_ref):
    k = jnp.concatenate([page_ref[0, h] for h in range(NSA_KV_HEADS)], axis=0)
    v = jnp.concatenate([page_ref[1, h] for h in range(NSA_KV_HEADS)], axis=0)
    return k, v


def _nsa_s1_kernel(pt_ref, *refs, n_pages, pps, ts, win_buf, rows):
    pages = refs[:pps]
    (newk_ref, newv_ref, q_ref, gate_ref, swin_ref, kwn_ref, pe_ref, w_ref, psel_ref, mt_ref, gsum_ref,
     qzt_ref, part_ref, bias_ref, xk_ref, xv_ref) = refs[pps:]
    j = pl.program_id(1)
    past = n_pages * PAGE_SIZE
    n_tail = xk_ref.shape[0] - past

    @pl.when(j == 0)
    def _fill_tail():
        for x_ref, new_ref in ((xk_ref, newk_ref), (xv_ref, newv_ref)):
            x_ref[pl.ds(past, n_tail), :] = jnp.zeros((n_tail, FEAT), f32)
            x_ref[pl.ds(past, ts), :] = new_ref[...]

    for k in range(pps):
        r0 = pl.multiple_of((j * pps + k) * PAGE_SIZE, PAGE_SIZE)
        kt, vt = _page_halves(pages[k])
        xk_ref[pl.ds(r0, PAGE_SIZE), :] = kt.T
        xv_ref[pl.ds(r0, PAGE_SIZE), :] = vt.T

    @pl.when(j == pl.num_programs(1) - 1)
    def _finish():
        n_rows = NSA_HEADS * ts
        qz = jnp.concatenate([_dot_nt(q_ref[...], psel_ref[hg]) for hg in range(NSA_HEADS)]
                             + [jnp.zeros((NCS - n_rows, FEAT), f32)], axis=0)
        qzt = (qz.T * Q_SCALE).astype(bf16)
        qzt_ref[...] = qzt
        t_col = past + lax.broadcasted_iota(i32, (1, NCS), 1) % ts
        total = past + ts
        n_sel = -(-total // SEL_BLOCK)
        n_cmp = n_sel * SEL_BLOCK // CMP_STRIDE - CMP_LEN // CMP_STRIDE + 1

        s = _dot(_compress_half(xk_ref, pe_ref, w_ref, 0, rows).astype(bf16), qzt)
        n_io = lax.broadcasted_iota(i32, s.shape, 0)
        valid = (n_io * CMP_STRIDE + (CMP_LEN - 1) <= t_col) & (n_io < n_cmp)
        pc, inv_c = _col_softmax(s, valid)
        o_c = _dot(_compress_half(xv_ref, pe_ref, w_ref, 1, rows).T.astype(bf16), pc) * inv_c
        imp_all = _dot(mt_ref[...], pc) * inv_c
        hi = imp_all.astype(bf16)
        r1 = imp_all - hi.astype(f32)
        mid = r1.astype(bf16)
        lo = (r1 - mid.astype(f32)).astype(bf16)
        gs = gsum_ref[...]
        imp = _dot(hi, gs) + _dot(mid, gs) + _dot(lo, gs)
        bias_ref[...] = _select_bias(imp, t_col, n_sel)

        wkt, wvt = _page_halves(swin_ref)
        kwn = jnp.concatenate([kwn_ref[...], jnp.zeros((ts, NSA_KV_WIDTH), f32)], axis=0).astype(bf16)
        s = jnp.concatenate([_dot_tn(wkt.astype(bf16), qzt), _dot(kwn[:, :FEAT], qzt)], axis=0)
        w_row = lax.broadcasted_iota(i32, s.shape, 0)
        w_pos = total - (win_buf + ts) + w_row
        dpos = t_col - w_pos
        pw, inv_w = _col_softmax(s, (dpos >= 0) & (dpos < WINDOW) & (w_pos >= 0) & (w_row < win_buf + ts))
        o_w = (_dot(wvt.astype(bf16), pw[:win_buf]) + _dot_tn(kwn[:, FEAT:], pw[win_buf:])) * inv_w

        sig = jax.nn.sigmoid(gate_ref[...])
        oc_r, ow_r = o_c.T, o_w.T
        part_ref[...] = jnp.concatenate(
            [sig[:, 3 * hg:3 * hg + 1] * oc_r[hg * ts:(hg + 1) * ts] + sig[:, 3 * hg + 2:3 * hg + 3] * ow_r[hg * ts:(hg + 1) * ts]
             for hg in range(NSA_HEADS)], axis=0)


def _nsa_s2_kernel(pt_ref, *refs, n_pages, pps, ts):
    pages = refs[:pps]
    news_ref, qzt_ref, bias_ref, part_ref, gate_ref, psel_ref, o_ref, m_ref, l_ref, acc_ref = refs[pps:]
    j = pl.program_id(1)
    past = n_pages * PAGE_SIZE
    qzt = qzt_ref[...]
    bpp = PAGE_SIZE // SEL_BLOCK

    @pl.when(j == 0)
    def _init():
        _flash_init(m_ref, l_ref, acc_ref)

    halves = [_page_halves(pages[k]) for k in range(pps)]
    kt = jnp.concatenate([h[0] for h in halves], axis=1).astype(bf16)
    vt = jnp.concatenate([h[1] for h in halves], axis=1).astype(bf16)
    _flash_step(_dot_tn(kt, qzt) + _block_bias(bias_ref, j * pps * bpp, pps * bpp, NCS), vt, m_ref, l_ref, acc_ref)

    @pl.when(j == pl.num_programs(1) - 1)
    def _finish():
        t_col = past + lax.broadcasted_iota(i32, (1, NCS), 1) % ts
        new = jnp.concatenate([news_ref[...], jnp.zeros((PAGE_SIZE - ts, NSA_KV_WIDTH), f32)], axis=0)
        s = _dot(new[:, :FEAT].astype(bf16), qzt) + _block_bias(bias_ref, n_pages * bpp, bpp, NCS)
        pos = past + lax.broadcasted_iota(i32, s.shape, 0)
        _flash_step(jnp.where(pos <= t_col, s, NEG_INF), new[:, FEAT:].T.astype(bf16), m_ref, l_ref, acc_ref)
        o_s = (acc_ref[...] * (1.0 / jnp.maximum(l_ref[...], 1e-30))).T
        sig = jax.nn.sigmoid(gate_ref[...])
        part = part_ref[...]
        out = jnp.zeros((ts, NSA_WIDTH), f32)
        for hg in range(NSA_HEADS):
            rs = slice(hg * ts, (hg + 1) * ts)
            mix = part[rs] + sig[:, 3 * hg + 1:3 * hg + 2] * o_s[rs]
            out = out + _dot(mix.astype(bf16), psel_ref[hg])
        o_ref[...] = out.astype(o_ref.dtype)


def _nsa_sample(nq3, ng3, kc3, vc3, kvs3, kvw3, cache_c_t, cache_s_t, swin_t, page_table, cw):
    bs, ts, _ = nq3.shape
    n_pages = page_table.shape[1]
    win_buf = swin_t.shape[-1]
    assert ts == SUBLANES and PAGE_SIZE % SEL_BLOCK == 0 and NSA_HEADS * ts <= NCS
    pps = math.gcd(n_pages, 8)
    steps = n_pages // pps
    past = n_pages * PAGE_SIZE
    total = past + ts
    n_sel = -(-total // SEL_BLOCK)
    n_cmp = n_sel * SEL_BLOCK // CMP_STRIDE - CMP_LEN // CMP_STRIDE + 1
    rows = -(-(n_cmp + 1) // SUBLANES) * SUBLANES
    x_rows = rows * CMP_STRIDE
    rb = -(-(n_sel + 1) // SUBLANES) * SUBLANES
    psel = jnp.asarray(_np_psel(), bf16)
    mt = jnp.asarray(_np_cmp_to_sel_t(n_cmp, n_sel, rb, rows), bf16)
    col = np.arange(NCS)
    gsum = ((col[:, None] // (NSA_GROUP * ts) == col[None, :] // (NSA_GROUP * ts)) & (col[:, None] % ts == col[None, :] % ts))
    gsum = jnp.asarray(gsum.astype(np.float32), bf16)
    pt_flat = page_table.reshape(-1).astype(i32)
    pe, w = cw
    page_shape = (2, NSA_KV_HEADS, NSA_HD, PAGE_SIZE)

    def page_spec(k):
        return pl.BlockSpec((None,) + page_shape, lambda b, j, pt, k=k: (pt[b * n_pages + j * pps + k], 0, 0, 0, 0))

    per_seq = lambda s: pl.BlockSpec((None,) + s, lambda b, j, pt: (b,) + (0,) * len(s))
    const = lambda s: pl.BlockSpec(s, lambda b, j, pt: (0,) * len(s))

    qzt, part, bias = pl.pallas_call(
        functools.partial(_nsa_s1_kernel, n_pages=n_pages, pps=pps, ts=ts, win_buf=win_buf, rows=rows),
        grid_spec=pltpu.PrefetchScalarGridSpec(
            num_scalar_prefetch=1, grid=(bs, steps),
            in_specs=[page_spec(k) for k in range(pps)]
            + [per_seq((ts, FEAT)), per_seq((ts, FEAT)), per_seq((ts, NSA_WIDTH)), per_seq((ts, LANES)),
               per_seq((2, NSA_KV_HEADS, NSA_HD, win_buf)), per_seq((ts, NSA_KV_WIDTH)), const(pe.shape), const(w.shape),
               const((NSA_HEADS, FEAT, NSA_WIDTH)), const((rb, rows)), const((NCS, NCS))],
            out_specs=[per_seq((FEAT, NCS)), per_seq((NSA_HEADS * ts, FEAT)), per_seq((rb, NCS))],
            scratch_shapes=[pltpu.VMEM((x_rows, FEAT), f32), pltpu.VMEM((x_rows, FEAT), f32)]),
        out_shape=[SDS((bs, FEAT, NCS), bf16), SDS((bs, NSA_HEADS * ts, FEAT), f32), SDS((bs, rb, NCS), f32)],
        compiler_params=_cp("arbitrary", "arbitrary"), name="nsa_sample_cmp",
    )(pt_flat, *([cache_c_t] * pps), kc3, vc3, nq3, ng3, swin_t, kvw3, pe, w, psel, mt, gsum)

    return pl.pallas_call(
        functools.partial(_nsa_s2_kernel, n_pages=n_pages, pps=pps, ts=ts),
        grid_spec=pltpu.PrefetchScalarGridSpec(
            num_scalar_prefetch=1, grid=(bs, steps),
            in_specs=[page_spec(k) for k in range(pps)]
            + [per_seq((ts, NSA_KV_WIDTH)), per_seq((FEAT, NCS)), per_seq((rb, NCS)), per_seq((NSA_HEADS * ts, FEAT)),
               per_seq((ts, LANES)), const((NSA_HEADS, FEAT, NSA_WIDTH))],
            out_specs=per_seq((ts, NSA_WIDTH)),
            scratch_shapes=[pltpu.VMEM((1, NCS), f32), pltpu.VMEM((1, NCS), f32), pltpu.VMEM((FEAT, NCS), f32)]),
        out_shape=SDS((bs, ts, NSA_WIDTH), bf16),
        compiler_params=_cp("arbitrary", "arbitrary"), name="nsa_sample_sel",
    )(pt_flat, *([cache_s_t] * pps), kvs3, qzt, bias, part, ng3, psel)


def _outproj_kernel(gla_ref, nsa_ref, x_ref, gm_ref, sh_ref, sc_ref, gf_ref, wo_ref, wrt_ref, br_ref,
                    x1_ref, h2_ref, idx_ref, tw_ref):
    y = _dot(gla_ref[...], wo_ref[:GLA_WIDTH, :]) + _dot(nsa_ref[...], wo_ref[GLA_WIDTH:, :])
    x1 = x_ref[...] + gm_ref[...] * y
    x1_ref[...] = x1
    h2 = _rms(x1, gf_ref[...]) * (1.0 + sc_ref[...]) + sh_ref[...]
    h2_ref[...] = h2
    s = _dot_nt(wrt_ref[...], h2.astype(bf16)) + br_ref[...]
    e_io = lax.broadcasted_iota(i32, s.shape, 0)
    vals, idxs = [], []
    for _ in range(TOP_K):
        mx = jnp.max(s, axis=0, keepdims=True)
        ix = jnp.min(jnp.where(s == mx, e_io, N_EXPERTS), axis=0, keepdims=True)
        vals.append(mx)
        idxs.append(ix)
        s = jnp.where(e_io == ix, -jnp.inf, s)
    ex = [jnp.exp(v - vals[0]) for v in vals]
    den = ex[0] + ex[1] + ex[2] + ex[3]
    idx_ref[...] = jnp.concatenate(idxs, axis=0)
    tw_ref[...] = jnp.concatenate([e / den for e in ex], axis=0)


def _outproj(gla_o, nsa_o, x2d, mod, g_ffn, w_out_b, w_router_t, b_router):
    t, tm = x2d.shape[0], mod.tm
    row = lambda i: (i, 0)
    const = lambda s: pl.BlockSpec(s, lambda i: (0, 0))
    return pl.pallas_call(
        _outproj_kernel, grid=(t // tm,),
        in_specs=[pl.BlockSpec((tm, GLA_WIDTH), row), pl.BlockSpec((tm, NSA_WIDTH), row), pl.BlockSpec((tm, D_MODEL), row),
                  mod.spec(2), mod.spec(3), mod.spec(4), const((1, D_MODEL)), const((MIX_WIDTH, D_MODEL)),
                  const((N_EXPERTS, D_MODEL)), const((N_EXPERTS, 1))],
        out_specs=[pl.BlockSpec((tm, D_MODEL), row), pl.BlockSpec((tm, D_MODEL), row),
                   pl.BlockSpec((TOP_K, tm), lambda i: (0, i)), pl.BlockSpec((TOP_K, tm), lambda i: (0, i))],
        out_shape=[SDS((t, D_MODEL), f32), SDS((t, D_MODEL), f32), SDS((TOP_K, t), i32), SDS((TOP_K, t), f32)],
        compiler_params=_cp("arbitrary"), name="outproj_router",
    )(gla_o, nsa_o, x2d, mod.arr, mod.arr, mod.arr, g_ffn.reshape(1, D_MODEL), w_out_b, w_router_t,
      b_router.reshape(N_EXPERTS, 1))


def _moe_kernel(be_ref, nv_ref, tokm_hbm, slotm_hbm, h2_hbm, wg_ref, bg_ref, wu_ref, bu_ref, wd_ref, bd_ref, out_hbm,
                tok_sm, slot_sm, xbuf, ybuf, wbuf, tsem, lsem, gsem, ssem):
    i = pl.program_id(0)
    nv = nv_ref[0]
    tm = xbuf.shape[1]
    pad_block = tokm_hbm.shape[0] - 1
    n_slab = 4
    sw, rg = D_EXPERT // n_slab, tm // n_slab

    def tok_copy(b):
        return pltpu.make_async_copy(tokm_hbm.at[b], tok_sm, tsem)

    def slot_copy(b):
        return pltpu.make_async_copy(slotm_hbm.at[b], slot_sm, lsem)

    def gather_row(par, r):
        pltpu.make_async_copy(h2_hbm.at[pl.ds(tok_sm[0, r], 1)], xbuf.at[par, pl.ds(r, 1)], gsem.at[par]).start()

    def scatter_row(par, r):
        pltpu.make_async_copy(ybuf.at[par, pl.ds(r, 1)], out_hbm.at[pl.ds(slot_sm[0, r], 1)], ssem.at[par]).start()

    def wait_gather(par):
        pltpu.make_async_copy(h2_hbm.at[pl.ds(0, tm)], xbuf.at[par], gsem.at[par]).wait()

    def wait_scatter(par):
        pltpu.make_async_copy(ybuf.at[par], out_hbm.at[pl.ds(0, tm)], ssem.at[par]).wait()

    @pl.when(i == 0)
    def _prologue():
        tok_copy(0).start()
        slot_copy(pad_block).start()
        ybuf[1] = jnp.zeros(ybuf.shape[1:], f32)
        tok_copy(0).wait()
        for r in range(tm):
            gather_row(0, r)
        tok_copy(1).start()

    @pl.when((i < nv) & ((i == 0) | (be_ref[i] != be_ref[jnp.maximum(i - 1, 0)])))
    def _new_expert():
        wbuf[0] = wg_ref[...].astype(bf16)
        wbuf[1] = wu_ref[...].astype(bf16)
        wbuf[2] = wd_ref[...].astype(bf16)

    for par in range(2):
        @pl.when((i < nv) & (i % 2 == par))
        def _run():
            wait_gather(par)
            tok_copy(i + 1).wait()
            slot_copy(jnp.where(i == 0, pad_block, i - 1)).wait()

            @pl.when(i >= 1)
            def _free_ybuf():
                wait_scatter(par)

            x = xbuf[par].astype(bf16)
            y = jnp.zeros((tm, D_MODEL), f32)
            for s in range(n_slab):
                cs = slice(s * sw, (s + 1) * sw)
                gate = jnp.minimum(_dot(x, wbuf[0, :, cs]) + bg_ref[:, cs], SWIGLU_LIMIT)
                up = jnp.clip(_dot(x, wbuf[1, :, cs]) + bu_ref[:, cs], -SWIGLU_LIMIT, SWIGLU_LIMIT)
                act = ((up + 1.0) * gate * jax.nn.sigmoid(SWIGLU_ALPHA * gate)).astype(bf16)
                y = y + _dot(act, wbuf[2, cs, :])
                for r in range(s * rg, (s + 1) * rg):
                    gather_row(1 - par, r)
                    scatter_row(1 - par, r)
            ybuf[par] = y + bd_ref[...]

            @pl.when(i + 2 <= nv)
            def _next_tokens():
                tok_copy(i + 2).start()

            slot_copy(i).start()

    @pl.when(i == nv)
    def _last_scatter():
        slot_copy(nv - 1).wait()
        for par in range(2):
            @pl.when(nv % 2 == par)
            def _():
                wait_gather(par)
                wait_scatter(par)
                for r in range(tm):
                    scatter_row(1 - par, r)
                wait_scatter(1 - par)


def _moe_experts(h2_all, tok_meta, slot_meta, block_e, n_valid, n_slots, wg, bg, wu, bu, wd, bd):
    n_blocks, _, tm = tok_meta.shape
    assert D_MODEL == D_EXPERT
    any_spec = pl.BlockSpec(memory_space=pl.ANY)
    wspec = lambda a, b: pl.BlockSpec((None, a, b), lambda i, be, nv: (be[i], 0, 0))
    return pl.pallas_call(
        _moe_kernel,
        grid_spec=pltpu.PrefetchScalarGridSpec(
            num_scalar_prefetch=2, grid=(n_blocks,),
            in_specs=[any_spec, any_spec, any_spec,
                      wspec(D_MODEL, D_EXPERT), wspec(1, D_EXPERT), wspec(D_MODEL, D_EXPERT), wspec(1, D_EXPERT),
                      wspec(D_EXPERT, D_MODEL), wspec(1, D_MODEL)],
            out_specs=any_spec,
            scratch_shapes=[pltpu.SMEM((1, tm), i32), pltpu.SMEM((1, tm), i32), pltpu.VMEM((2, tm, D_MODEL), f32),
                            pltpu.VMEM((2, tm, D_MODEL), f32), pltpu.VMEM((3, D_MODEL, D_EXPERT), bf16),
                            pltpu.SemaphoreType.DMA, pltpu.SemaphoreType.DMA,
                            pltpu.SemaphoreType.DMA((2,)), pltpu.SemaphoreType.DMA((2,))]),
        out_shape=SDS((n_slots, D_MODEL), f32),
        compiler_params=pltpu.CompilerParams(dimension_semantics=("arbitrary",), vmem_limit_bytes=MOE_VMEM_LIMIT_BYTES),
        name="moe_experts",
    )(block_e, n_valid, tok_meta, slot_meta, h2_all, wg, bg.reshape(N_EXPERTS, 1, D_EXPERT), wu,
      bu.reshape(N_EXPERTS, 1, D_EXPERT), wd, bd.reshape(N_EXPERTS, 1, D_MODEL))


def _moe_plan(idx_t, tm, t_pad):
    n_tok = idx_t.shape[1]
    n_assign = n_tok * TOP_K
    flat_e = idx_t.reshape(n_assign)
    order = jnp.argsort(flat_e).astype(i32)
    counts = jnp.sum((flat_e[:, None] == jnp.arange(N_EXPERTS, dtype=i32)[None, :]).astype(i32), axis=0)
    padded = (counts + tm - 1) // tm * tm
    start = jnp.cumsum(counts) - counts
    pad_end = jnp.cumsum(padded)
    pad_start = pad_end - padded
    n_blocks = (n_assign + N_EXPERTS * (tm - 1) + tm - 1) // tm + 1
    blk0 = jnp.arange(n_blocks, dtype=i32) * tm
    block_e = jnp.minimum(jnp.sum((pad_end[None, :] <= blk0[:, None]).astype(i32), axis=1), N_EXPERTS - 1).astype(i32)
    n_valid = (pad_end[-1] // tm).astype(i32).reshape(1)
    r_in = jnp.arange(tm, dtype=i32)[None, :]
    j = blk0[:, None] + r_in - pad_start[block_e][:, None]
    valid = (j < counts[block_e][:, None]) & (blk0[:, None] < pad_end[-1])
    a = order[jnp.clip(start[block_e][:, None] + j, 0, n_assign - 1)]
    tok = a % n_tok
    slot = (a // n_tok) * t_pad + tok
    tok_meta = jnp.where(valid, tok, 0).astype(i32).reshape(n_blocks, 1, tm)
    slot_meta = jnp.where(valid, slot, TOP_K * t_pad + r_in).astype(i32).reshape(n_blocks, 1, tm)
    return tok_meta, slot_meta, block_e, n_valid


def _final_kernel(x1_ref, *refs):
    y_refs, (tw_ref, gate_ref, g_ref, o_ref) = refs[:TOP_K], refs[TOP_K:]
    tw = tw_ref[...]
    f = tw[:, 0:1] * y_refs[0][...]
    for k in range(1, TOP_K):
        f = f + tw[:, k:k + 1] * y_refs[k][...]
    o_ref[...] = _rms(x1_ref[...] + gate_ref[...] * f, g_ref[...])


def _final(x1, y4, tw, mod, g_final, row0, t_pad):
    t, tm = x1.shape[0], mod.tm
    assert row0 % tm == 0 and t_pad % tm == 0
    b0 = row0 // tm
    row = lambda i: (i, 0)
    y_specs = [pl.BlockSpec((tm, D_MODEL), lambda i, k=k: (k * (t_pad // tm) + b0 + i, 0)) for k in range(TOP_K)]
    return pl.pallas_call(
        _final_kernel, grid=(t // tm,),
        in_specs=[pl.BlockSpec((tm, D_MODEL), row)] + y_specs
        + [pl.BlockSpec((tm, TOP_K), lambda i: (i + b0, 0)), mod.spec(5), pl.BlockSpec((1, D_MODEL), lambda i: (0, 0))],
        out_specs=pl.BlockSpec((tm, D_MODEL), row), out_shape=SDS((t, D_MODEL), f32),
        compiler_params=_cp("arbitrary"), name="combine_final_norm",
    )(x1, *([y4] * TOP_K), tw, mod.arr, g_final.reshape(1, D_MODEL))


def _prep_w_in(w_in):
    gq, gk, gv, gr, ga, nq, nkc, nks, nkw, ng = jnp.split(w_in, np.cumsum(IN_SPLITS)[:-1].tolist(), axis=1)
    pad = lambda a: jnp.pad(a, ((0, 0), (0, LANES - a.shape[1])))
    w_p = jnp.concatenate([gq, gk, gv, gr, nq, nkc, nks, nkw, pad(ga), pad(ng)], axis=1).astype(bf16)
    w_kv_t = jnp.concatenate([nkc, nks, nkw], axis=1).T.astype(bf16)
    return w_p, w_kv_t


def _feature_major(a, rows_axis):
    return jnp.moveaxis(a, rows_axis, -1)


def kernel(x_prompt, x_sample, c_prompt, c_sample, cache_cmp, cache_sel, state_win, state_gla, page_table, w_ada, b_ada, g_mix, g_ffn, w_in, w_a2, b_a, g_gla, phi_pe, phi_w, w_out, w_router, b_router, w_gate, b_gate, w_up, b_up, w_down, b_down, g_final):
    bp, sp = x_prompt.shape[:2]
    bs, ts = x_sample.shape[:2]
    tp, tsn = bp * sp, bs * ts
    win_buf = state_win.shape[2]
    l = 0

    mod = _adaln(jnp.concatenate([c_prompt, c_sample], axis=0), w_ada[l], b_ada[l])
    tm_p = math.gcd(sp, 512)
    tm_s = math.gcd(tsn, 512)
    mod_p = _Mod(mod[:bp].reshape(bp, 6, 1, D_MODEL), False, tm_p, sp // tm_p)
    mod_s = _Mod(jnp.repeat(mod[bp:].reshape(bs, 6, D_MODEL).transpose(1, 0, 2), ts, axis=1), True, tm_s, None)

    w_in_p, w_kv_t = _prep_w_in(w_in[l])
    cw = _prep_compress(phi_pe[l], phi_w[l])
    w_out_b = w_out[l].astype(bf16)
    w_router_t = w_router[l].T.astype(bf16)
    xp2, xs2 = x_prompt.reshape(tp, D_MODEL), x_sample.reshape(tsn, D_MODEL)

    (gq, gk, gv, gr, nq, kc, vc, ga, ng, kvc_t, kvs_t, kvw_t, ksk, kwk, ksvt, kwvt) = _inproj(
        xp2, mod_p, g_mix[l], w_in_p, w_kv_t, sp)
    gla_o, st_p = _gla(gq, gk, gv, gr, ga, jnp.zeros((bp, GLA_HEADS, GLA_DV, GLA_DK), f32), w_a2[l], b_a[l], g_gla[l], bp, sp)
    kcm, vct = _cmp_prompt(kc.reshape(bp, sp, FEAT), vc.reshape(bp, sp, FEAT), cw)
    nsa_o = _nsa_prompt(nq.reshape(bp, sp, NSA_WIDTH), ng.reshape(bp, sp, LANES), kcm, vct,
                        ksk.reshape(bp, sp, FEAT), ksvt, kwk.reshape(bp, sp, FEAT), kwvt)
    x1_p, h2_p, idx_p, tw_p = _outproj(gla_o.reshape(tp, GLA_WIDTH), nsa_o.reshape(tp, NSA_WIDTH), xp2, mod_p,
                                       g_ffn[l], w_out_b, w_router_t, b_router[l])
    token_major = lambda a: jnp.moveaxis(a, -1, 1)[None]
    new_cmp_p = token_major(kvc_t)
    new_sel_p = token_major(kvs_t)
    new_win_p = token_major(jnp.pad(kvw_t, ((0, 0),) * 4 + ((win_buf, 0),))[..., -win_buf:])
    new_gla_p = jnp.swapaxes(st_p, 2, 3)[None]

    gq, gk, gv, gr, nq, kc, vc, ks, vs, kw, vw, ga, ng = _inproj(xs2, mod_s, g_mix[l], w_in_p)
    gla_os, st_s = _gla(gq, gk, gv, gr, ga, jnp.swapaxes(state_gla[l], 2, 3), w_a2[l], b_a[l], g_gla[l], bs, ts)
    r3 = lambda a: a.reshape(bs, ts, a.shape[-1])
    kvc3, kvs3, kvw3 = (jnp.concatenate([r3(k_), r3(v_)], axis=-1) for k_, v_ in ((kc, vc), (ks, vs), (kw, vw)))
    swin_t = _feature_major(state_win[l], 1)
    nsa_os = _nsa_sample(r3(nq), r3(ng), r3(kc), r3(vc), kvs3, kvw3,
                         _feature_major(cache_cmp[l], 1), _feature_major(cache_sel[l], 1), swin_t, page_table, cw)
    x1_s, h2_s, idx_s, tw_s = _outproj(gla_os.reshape(tsn, GLA_WIDTH), nsa_os.reshape(tsn, NSA_WIDTH), xs2, mod_s,
                                       g_ffn[l], w_out_b, w_router_t, b_router[l])
    kv_row = (2, NSA_KV_HEADS, NSA_HD)
    new_cmp_s = kvc3.reshape((1, bs, ts) + kv_row)
    new_sel_s = kvs3.reshape((1, bs, ts) + kv_row)
    kw_t = _feature_major(kvw3.reshape((bs, ts) + kv_row), 1)
    new_win_s = token_major(jnp.concatenate([swin_t, kw_t], axis=-1)[..., -win_buf:])
    new_gla_s = jnp.swapaxes(st_s, 2, 3)[None]

    n_tok = tp + tsn
    t_pad = -(-n_tok // tm_p) * tm_p
    tok_meta, slot_meta, block_e, n_valid = _moe_plan(jnp.concatenate([idx_p, idx_s], axis=1), MOE_TM, t_pad)
    y4 = _moe_experts(jnp.concatenate([h2_p, h2_s], axis=0), tok_meta, slot_meta, block_e, n_valid,
                      TOP_K * t_pad + MOE_TM, w_gate[l], b_gate[l], w_up[l], b_up[l], w_down[l], b_down[l])
    tw_all = jnp.concatenate([tw_p, tw_s], axis=1).T
    y_p = _final(x1_p, y4, tw_all, mod_p, g_final, 0, t_pad).reshape(bp, sp, D_MODEL)
    y_s = _final(x1_s, y4, tw_all, mod_s, g_final, tp, t_pad).reshape(bs, ts, D_MODEL)
    return (y_p, y_s, new_cmp_p, new_sel_p, new_win_p, new_gla_p, new_cmp_s, new_sel_s, new_win_s, new_gla_s)
```

```python
import functools
import math

import numpy as np
import jax
import jax.numpy as jnp
from jax import lax
from jax.experimental import pallas as pl
from jax.experimental.pallas import tpu as pltpu

f32, bf16, i32 = jnp.float32, jnp.bfloat16, jnp.int32
SDS = jax.ShapeDtypeStruct

D_MODEL = 1024
GLA_HEADS, GLA_DK, GLA_DV, GLA_GATE_RANK, GLA_TAU, GLA_CHUNK = 4, 64, 128, 16, 16.0, 64
NSA_HEADS, NSA_KV_HEADS, NSA_HD = 8, 2, 64
NSA_GROUP = NSA_HEADS // NSA_KV_HEADS
CMP_LEN, CMP_STRIDE, SEL_BLOCK, N_SEL, WINDOW, NSA_QBLOCK = 32, 16, 64, 8, 512, 128
N_EXPERTS, TOP_K, D_EXPERT = 32, 4, 1024
SWIGLU_LIMIT, SWIGLU_ALPHA = 7.0, 1.702
RMS_EPS, NEG_INF, FORCE_SCORE = 1e-6, -1e30, 1e4
PAGE_SIZE = 128

GLA_QK_WIDTH = GLA_HEADS * GLA_DK
GLA_WIDTH = GLA_HEADS * GLA_DV
NSA_WIDTH = NSA_HEADS * NSA_HD
NSA_KV_WIDTH = 2 * NSA_KV_HEADS * NSA_HD
IN_SPLITS = (GLA_QK_WIDTH, GLA_QK_WIDTH, GLA_WIDTH, GLA_WIDTH, GLA_GATE_RANK,
             NSA_WIDTH, NSA_KV_WIDTH, NSA_KV_WIDTH, NSA_KV_WIDTH, 3 * NSA_HEADS)
MIX_WIDTH = GLA_WIDTH + NSA_WIDTH

LANES = 128
SUBLANES = 8
VMEM_LIMIT_BYTES = 48 * 1024 * 1024
MOE_VMEM_LIMIT_BYTES = 56 * 1024 * 1024
D_CHUNKS = D_MODEL // LANES

FEAT = NSA_KV_HEADS * NSA_HD
_IN_COLS = (("gq", 0, 256, f32), ("gk", 256, 256, f32), ("gv", 512, 512, bf16), ("gr", 1024, 512, f32),
            ("nq", 1536, 512, bf16), ("kc", 2048, FEAT, f32), ("vc", 2176, FEAT, f32), ("ks", 2304, FEAT, f32),
            ("vs", 2432, FEAT, f32), ("kw", 2560, FEAT, f32), ("vw", 2688, FEAT, f32),
            ("ga", 2816, LANES, f32), ("ng", 2944, LANES, f32))
_IN_WIDTH = 3072
_KV0 = 2048
MOE_TM = 512
Q_SCALE = NSA_HD ** -0.5 * math.log2(math.e)


def _cp(*sem):
    return pltpu.CompilerParams(dimension_semantics=sem, vmem_limit_bytes=VMEM_LIMIT_BYTES)


def _dot(a, b):
    return jnp.dot(a, b, preferred_element_type=f32)


def _dot_nt(a, b):
    return lax.dot_general(a, b, (((1,), (1,)), ((), ())), preferred_element_type=f32)


def _dot_tn(a, b):
    return lax.dot_general(a, b, (((0,), (0,)), ((), ())), preferred_element_type=f32)


def _rms(x, g):
    return x * lax.rsqrt(jnp.mean(x * x, axis=-1, keepdims=True) + RMS_EPS) * g


def _silu(x):
    return x * jax.nn.sigmoid(x)


def _adaln_kernel(c_ref, w_ref, b_ref, o_ref):
    s = _silu(c_ref[...]).astype(bf16)
    o_ref[...] = _dot(s, w_ref[...].astype(bf16)) + b_ref[...]


def _adaln(c_all, w_ada, b_ada):
    nb, n = c_all.shape[0], w_ada.shape[1]
    tn = 1024
    return pl.pallas_call(
        _adaln_kernel, grid=(n // tn,),
        in_specs=[pl.BlockSpec((nb, D_MODEL), lambda j: (0, 0)),
                  pl.BlockSpec((D_MODEL, tn), lambda j: (0, j)),
                  pl.BlockSpec((1, tn), lambda j: (0, j))],
        out_specs=pl.BlockSpec((nb, tn), lambda j: (0, j)),
        out_shape=SDS((nb, n), f32), compiler_params=_cp("arbitrary"), name="adaln",
    )(c_all, w_ada, b_ada.reshape(1, n))


class _Mod:
    def __init__(self, arr, per_token, tm, tiles_per_seq):
        self.arr, self.per_token, self.tm, self.tps = arr, per_token, tm, tiles_per_seq

    def spec(self, k):
        if self.per_token:
            return pl.BlockSpec((None, self.tm, D_MODEL), lambda i: (k, i, 0))
        tps = self.tps
        return pl.BlockSpec((None, None, 1, D_MODEL), lambda i: (i // tps, k, 0, 0))


def _in_cols(feature_major):
    return tuple(c for c in _IN_COLS if not (feature_major and c[0] in ("ks", "vs", "kw", "vw")))


def _inproj_kernel(x_ref, sh_ref, sc_ref, g_ref, w_ref, *rest, feature_major):
    y = _rms(x_ref[...], g_ref[...])
    h = (y * (1.0 + sc_ref[...]) + sh_ref[...]).astype(bf16)
    outs = rest[1:] if feature_major else rest
    cols = _in_cols(feature_major)
    for o_ref, (_, c0, w, _) in zip(outs, cols):
        o_ref[...] = _dot(h, w_ref[:, c0:c0 + w]).astype(o_ref.dtype)
    if feature_major:
        wt_ref = rest[0]
        kvt_refs, (ksk_ref, kwk_ref, ksvt_ref, kwvt_ref) = outs[len(cols):-4], outs[-4:]
        kvt = _dot_nt(wt_ref[...], h)
        for a, kvt_ref in enumerate(kvt_refs):
            for ch in range(2 * NSA_KV_HEADS):
                r0 = a * NSA_KV_WIDTH + ch * NSA_HD
                kvt_ref[ch // NSA_KV_HEADS, ch % NSA_KV_HEADS] = kvt[r0:r0 + NSA_HD]
        ksvt_ref[...] = kvt[NSA_KV_WIDTH + FEAT:2 * NSA_KV_WIDTH].astype(bf16)
        kwvt_ref[...] = kvt[2 * NSA_KV_WIDTH + FEAT:].astype(bf16)
        ksk_ref[...] = _dot(h, w_ref[:, _KV0 + NSA_KV_WIDTH:_KV0 + NSA_KV_WIDTH + FEAT]).astype(bf16)
        kwk_ref[...] = _dot(h, w_ref[:, _KV0 + 2 * NSA_KV_WIDTH:_KV0 + 2 * NSA_KV_WIDTH + FEAT]).astype(bf16)


def _inproj(x2d, mod, g_mix, w_in_p, w_kv_t=None, seq_len=None):
    t, tm = x2d.shape[0], mod.tm
    row = lambda i: (i, 0)
    in_specs = [pl.BlockSpec((tm, D_MODEL), row), mod.spec(0), mod.spec(1),
                pl.BlockSpec((1, D_MODEL), lambda i: (0, 0)),
                pl.BlockSpec((D_MODEL, _IN_WIDTH), lambda i: (0, 0))]
    cols = _in_cols(w_kv_t is not None)
    out_specs = [pl.BlockSpec((tm, w), row) for (_, _, w, _) in cols]
    out_shape = [SDS((t, w), dt) for (_, _, w, dt) in cols]
    args = [x2d, mod.arr, mod.arr, g_mix.reshape(1, D_MODEL), w_in_p]
    if w_kv_t is not None:
        b_sz, tps = t // seq_len, seq_len // tm
        in_specs.append(pl.BlockSpec((3 * NSA_KV_WIDTH, D_MODEL), lambda i: (0, 0)))
        args.append(w_kv_t)
        fm = lambda w: pl.BlockSpec((None, w, tm), lambda i: (i // tps, 0, i % tps))
        out_specs += [pl.BlockSpec((None, 2, NSA_KV_HEADS, NSA_HD, tm), lambda i: (i // tps, 0, 0, 0, i % tps))] * 3
        out_specs += [pl.BlockSpec((tm, FEAT), row), pl.BlockSpec((tm, FEAT), row), fm(FEAT), fm(FEAT)]
        out_shape += [SDS((b_sz, 2, NSA_KV_HEADS, NSA_HD, seq_len), f32)] * 3
        out_shape += [SDS((t, FEAT), bf16), SDS((t, FEAT), bf16),
                      SDS((b_sz, FEAT, seq_len), bf16), SDS((b_sz, FEAT, seq_len), bf16)]
    return pl.pallas_call(
        functools.partial(_inproj_kernel, feature_major=w_kv_t is not None), grid=(t // tm,),
        in_specs=in_specs, out_specs=out_specs, out_shape=out_shape,
        compiler_params=_cp("arbitrary"), name="inproj",
    )(*args)


def _gla_kernel(q_ref, k_ref, v_ref, r_ref, a_ref, s0_ref, wa2_ref, ba_ref, gg_ref, o_ref, st_ref, *, chunk, n_chunks):
    @pl.when(pl.program_id(1) == 0)
    def _init():
        st_ref[...] = s0_ref[...]

    ri = lax.broadcasted_iota(i32, (chunk, chunk), 0)
    ci = lax.broadcasted_iota(i32, (chunk, chunk), 1)
    causal = ri >= ci
    tril = jnp.where(causal, 1.0, 0.0).astype(bf16)

    def body(c, carry):
        rows = pl.ds(pl.multiple_of(c * chunk, chunk), chunk)
        for sq in range(q_ref.shape[0]):
            a_low = a_ref[sq, rows, :][:, :GLA_GATE_RANK].astype(bf16)
            z = _dot(a_low, wa2_ref[...]) + ba_ref[...]
            log_a = (jnp.minimum(z, 0.0) - jnp.log1p(jnp.exp(-jnp.abs(z)))) * (1.0 / GLA_TAU)
            hi = log_a.astype(bf16)
            lo = (log_a - hi.astype(f32)).astype(bf16)
            cum = _dot(tril, hi) + _dot(tril, lo)
            last = cum[chunk - 1:chunk, :]
            q = q_ref[sq, rows, :] * (GLA_DK ** -0.5)
            k = k_ref[sq, rows, :]
            qd = (q * jnp.exp(cum)).astype(bf16)
            ki = (k * jnp.exp(-cum)).astype(bf16)
            ke = (k * jnp.exp(last - cum)).astype(bf16)
            dec = jnp.exp(last)
            v = v_ref[sq, rows, :]
            r = r_ref[sq, rows, :]
            for h in range(GLA_HEADS):
                sk = slice(h * GLA_DK, (h + 1) * GLA_DK)
                sv = slice(h * GLA_DV, (h + 1) * GLA_DV)
                qh, kih, keh, vh = qd[:, sk], ki[:, sk], ke[:, sk], v[:, sv]
                att = jnp.where(causal, _dot_nt(qh, kih), 0.0)
                st = st_ref[sq, h]
                o = _dot(att.astype(bf16), vh) + _dot_nt(qh, st.astype(bf16))
                st_ref[sq, h] = st * dec[:, sk] + _dot_tn(vh, keh)
                o_ref[sq, rows, sv] = (_rms(o, gg_ref[...]) * _silu(r[:, sv])).astype(o_ref.dtype)
        return carry

    lax.fori_loop(0, n_chunks, body, 0)


def _gla(gq, gk, gv, gr, ga, s0t, w_a2, b_a, g_gla, b_sz, seq_len):
    chunk = math.gcd(seq_len, GLA_CHUNK)
    tb = min(seq_len, 512)
    n_chunks = tb // chunk
    nb = math.gcd(b_sz, 4 if tb >= 512 else 8)
    r3 = lambda a: a.reshape(b_sz, seq_len, a.shape[-1])
    tok = lambda w: pl.BlockSpec((nb, tb, w), lambda b, j: (b, j, 0))
    st_spec = pl.BlockSpec((nb, GLA_HEADS, GLA_DV, GLA_DK), lambda b, j: (b, 0, 0, 0))
    const = lambda s: pl.BlockSpec(s, lambda b, j: (0, 0))
    return pl.pallas_call(
        functools.partial(_gla_kernel, chunk=chunk, n_chunks=n_chunks),
        grid=(b_sz // nb, seq_len // tb),
        in_specs=[tok(GLA_QK_WIDTH), tok(GLA_QK_WIDTH), tok(GLA_WIDTH), tok(GLA_WIDTH), tok(LANES), st_spec,
                  const((GLA_GATE_RANK, GLA_QK_WIDTH)), const((1, GLA_QK_WIDTH)), const((1, GLA_DV))],
        out_specs=[tok(GLA_WIDTH), st_spec],
        out_shape=[SDS((b_sz, seq_len, GLA_WIDTH), bf16), SDS((b_sz, GLA_HEADS, GLA_DV, GLA_DK), f32)],
        compiler_params=_cp("arbitrary", "arbitrary"), name="gla",
    )(r3(gq), r3(gk), r3(gv), r3(gr), r3(ga), s0t, w_a2.astype(bf16), b_a.reshape(1, -1), g_gla.reshape(1, -1))


def _np_psel():
    p = np.zeros((NSA_HEADS, FEAT, NSA_WIDTH), np.float32)
    for h in range(NSA_KV_HEADS):
        for g in range(NSA_GROUP):
            for d in range(NSA_HD):
                p[h * NSA_GROUP + g, h * NSA_HD + d, (h * NSA_GROUP + g) * NSA_HD + d] = 1.0
    return p


def _np_cmp_to_sel_t(n_cmp, n_sel, rows, cols):
    i0 = np.arange(n_cmp)[None, :] * CMP_STRIDE
    j0 = np.arange(n_sel)[:, None] * SEL_BLOCK
    m = np.zeros((rows, cols), np.float32)
    m[:n_sel, :n_cmp] = ((i0 < j0 + SEL_BLOCK) & (i0 + CMP_LEN > j0)).astype(np.float32)
    return m


def _prep_compress(phi_pe, phi_w):
    w = phi_w.reshape(2, CMP_LEN, NSA_HD, NSA_HD)
    wb = jnp.einsum('cldo,hk->clhdko', w, jnp.eye(NSA_KV_HEADS, dtype=f32)).reshape(2, CMP_LEN, FEAT, FEAT)
    pe = jnp.broadcast_to(phi_pe[:, :, None, :], (2, CMP_LEN, NSA_KV_HEADS, NSA_HD)).reshape(2, CMP_LEN, 1, FEAT)
    return pe, wb.astype(bf16)


def _compress_half(x_ref, pe_ref, w_ref, c, rows):
    top = jnp.zeros((rows, FEAT), f32)
    bot = jnp.zeros((rows, FEAT), f32)
    for l in range(0, CMP_STRIDE, 2):
        xa = x_ref[pl.ds(l, rows, stride=CMP_STRIDE), :]
        xb = x_ref[pl.ds(l + 1, rows, stride=CMP_STRIDE), :]
        for acc_is_top, l0 in ((True, l), (False, CMP_STRIDE + l)):
            x2 = jnp.concatenate([xa + pe_ref[c, l0], xb + pe_ref[c, l0 + 1]], axis=1).astype(bf16)
            w2 = w_ref[c, pl.ds(l0, 2)].reshape(2 * FEAT, FEAT)
            if acc_is_top:
                top = top + _dot(x2, w2)
            else:
                bot = bot + _dot(x2, w2)
    return top + pltpu.roll(bot, rows - 1, 0)


def _col_softmax(s, valid, every_column_valid=False):
    s = jnp.where(valid, s, NEG_INF)
    m = jnp.max(s, axis=0, keepdims=True)
    p = jnp.exp2(s - m)
    if not every_column_valid:
        p = jnp.where(valid, p, 0.0)
    l = jnp.sum(p, axis=0, keepdims=True)
    return p.astype(bf16), 1.0 / jnp.maximum(l, 1e-30)


def _select_bias(imp, t_col, n_sel):
    blk = lax.broadcasted_iota(i32, imp.shape, 0)
    cur = lax.shift_right_logical(t_col, int(math.log2(SEL_BLOCK)))
    forced = (blk == 0) | (blk == cur) | (blk == cur - 1)
    future = blk * SEL_BLOCK > t_col
    score = jnp.where(future, NEG_INF, jnp.where(forced, FORCE_SCORE, imp))
    score = jnp.where(blk < n_sel, score, -jnp.inf)
    bias = jnp.full(imp.shape, NEG_INF, f32)
    for _ in range(min(N_SEL, n_sel)):
        mx = jnp.max(score, axis=0, keepdims=True)
        idx = jnp.min(jnp.where(score == mx, blk, 2 ** 30), axis=0, keepdims=True)
        pick = blk == idx
        bias = jnp.where(pick, 0.0, bias)
        score = jnp.where(pick, -jnp.inf, score)
    return bias


def _block_bias(bias_ref, first_block, n_blocks, cols):
    return jnp.concatenate(
        [jnp.broadcast_to(bias_ref[pl.ds(first_block + b, 1), :], (SEL_BLOCK, cols)) for b in range(n_blocks)], axis=0)


def _flash_step(s, vt, m_ref, l_ref, acc_ref):
    m_old = m_ref[...]
    m_new = jnp.maximum(m_old, jnp.max(s, axis=0, keepdims=True))
    alpha = jnp.exp2(m_old - m_new)
    p = jnp.exp2(s - m_new)
    l_ref[...] = alpha * l_ref[...] + jnp.sum(p, axis=0, keepdims=True)
    acc_ref[...] = alpha * acc_ref[...] + _dot(vt, p.astype(bf16))
    m_ref[...] = m_new


def _flash_init(m_ref, l_ref, acc_ref):
    m_ref[...] = jnp.full(m_ref.shape, NEG_INF, f32)
    l_ref[...] = jnp.zeros(l_ref.shape, f32)
    acc_ref[...] = jnp.zeros(acc_ref.shape, f32)


def _cmp_prompt_kernel(k_ref, v_ref, pe_ref, w_ref, kc_ref, vct_ref):
    rows = kc_ref.shape[0]
    kc_ref[...] = _compress_half(k_ref, pe_ref, w_ref, 0, rows).astype(bf16)
    vct_ref[...] = _compress_half(v_ref, pe_ref, w_ref, 1, rows).T.astype(bf16)


def _cmp_prompt(kc3, vc3, cw):
    b_sz, seq_len, _ = kc3.shape
    r = seq_len // CMP_STRIDE
    pe, w = cw
    return pl.pallas_call(
        _cmp_prompt_kernel, grid=(b_sz,),
        in_specs=[pl.BlockSpec((None, seq_len, FEAT), lambda b: (b, 0, 0)), pl.BlockSpec((None, seq_len, FEAT), lambda b: (b, 0, 0)),
                  pl.BlockSpec(pe.shape, lambda b: (0, 0, 0, 0)), pl.BlockSpec(w.shape, lambda b: (0, 0, 0, 0))],
        out_specs=[pl.BlockSpec((None, r, FEAT), lambda b: (b, 0, 0)), pl.BlockSpec((None, FEAT, r), lambda b: (b, 0, 0))],
        out_shape=[SDS((b_sz, r, FEAT), bf16), SDS((b_sz, FEAT, r), bf16)],
        compiler_params=_cp("arbitrary"), name="cmp_prompt",
    )(kc3, vc3, pe, w)


def _nsa_prompt_kernel(q_ref, gate_ref, kc_ref, vct_ref, ksk_ref, ksvt_ref, kwk_ref, kwvt_ref, psel_ref, mt_ref,
                       o_ref, bias_ref, m_ref, l_ref, acc_ref, *, qb, seq_len, tk, tkp, wk):
    i = pl.program_id(1)
    start = i * qb
    nc = NSA_HEADS * qb
    n_sel = seq_len // SEL_BLOCK
    n_cmp = seq_len // CMP_STRIDE - CMP_LEN // CMP_STRIDE + 1
    q = q_ref[...]
    qzt = jnp.concatenate([_dot_nt(psel_ref[hg], q) for hg in range(NSA_HEADS)], axis=1)
    qzt = (qzt * Q_SCALE).astype(bf16)
    t_col = start + lax.broadcasted_iota(i32, (1, nc), 1) % qb

    s = _dot(kc_ref[...], qzt)
    n_io = lax.broadcasted_iota(i32, s.shape, 0)
    valid = (n_io * CMP_STRIDE + (CMP_LEN - 1) <= t_col) & (n_io < n_cmp)
    pc, inv_c = _col_softmax(s, valid)
    o_c = _dot(vct_ref[...], pc) * inv_c
    imp_all = _dot(mt_ref[...], pc) * inv_c
    imp = []
    for h in range(NSA_KV_HEADS):
        tot = imp_all[:, h * NSA_GROUP * qb:(h * NSA_GROUP + 1) * qb]
        for g in range(1, NSA_GROUP):
            tot = tot + imp_all[:, (h * NSA_GROUP + g) * qb:(h * NSA_GROUP + g + 1) * qb]
        imp.append(tot)
    bias = _select_bias(jnp.concatenate(imp, axis=1), t_col[:, :NSA_KV_HEADS * qb], n_sel)
    bias_ref[...] = jnp.concatenate([bias[:, h * qb:(h + 1) * qb] for h in range(NSA_KV_HEADS) for _ in range(NSA_GROUP)], axis=1)

    _flash_init(m_ref, l_ref, acc_ref)

    def scores(jt, width):
        k0 = pl.multiple_of(jt * width, width)
        bias = _block_bias(bias_ref, jt * (width // SEL_BLOCK), width // SEL_BLOCK, nc)
        return k0, _dot(ksk_ref[pl.ds(k0, width), :], qzt) + bias

    def past_tile(jt, carry):
        k0, s = scores(jt, tkp)
        _flash_step(s, ksvt_ref[:, pl.ds(k0, tkp)], m_ref, l_ref, acc_ref)
        return carry

    def causal_tile(jt, carry):
        k0, s = scores(jt, tk)
        pos = k0 + lax.broadcasted_iota(i32, s.shape, 0)
        _flash_step(jnp.where(pos <= t_col, s, NEG_INF), ksvt_ref[:, pl.ds(k0, tk)], m_ref, l_ref, acc_ref)
        return carry

    n_past = start // tkp
    lax.fori_loop(0, n_past, past_tile, 0)
    lax.fori_loop(n_past * (tkp // tk), (start + qb + tk - 1) // tk, causal_tile, 0)
    o_s = acc_ref[...] * (1.0 / jnp.maximum(l_ref[...], 1e-30))

    ks = pl.multiple_of(jnp.clip(start - WINDOW, 0, seq_len - wk), qb)
    s = _dot(kwk_ref[pl.ds(ks, wk), :], qzt)
    dpos = t_col - (ks + lax.broadcasted_iota(i32, s.shape, 0))
    pw, inv_w = _col_softmax(s, lax.bitcast_convert_type(dpos, jnp.uint32) < WINDOW, every_column_valid=True)
    o_w = _dot(kwvt_ref[:, pl.ds(ks, wk)], pw) * inv_w

    gt = jax.nn.sigmoid(gate_ref[...]).T
    out = jnp.zeros((qb, NSA_WIDTH), f32)
    for hg in range(NSA_HEADS):
        cs = slice(hg * qb, (hg + 1) * qb)
        mix = gt[3 * hg:3 * hg + 1] * o_c[:, cs] + gt[3 * hg + 1:3 * hg + 2] * o_s[:, cs] + gt[3 * hg + 2:3 * hg + 3] * o_w[:, cs]
        out = out + _dot(mix.T.astype(bf16), psel_ref[hg])
    o_ref[...] = out.astype(o_ref.dtype)


def _nsa_prompt(nq3, ng3, kc, vct, ksk, ksvt, kwk, kwvt):
    b_sz, seq_len, _ = nq3.shape
    qb = math.gcd(seq_len, NSA_QBLOCK)
    tk = 2 * qb
    tkp = 2 * tk if seq_len % (2 * tk) == 0 else tk
    assert seq_len % tk == 0 and tk % SEL_BLOCK == 0
    wk = min(WINDOW + qb, seq_len)
    r = kc.shape[1]
    n_sel = seq_len // SEL_BLOCK
    n_cmp = seq_len // CMP_STRIDE - CMP_LEN // CMP_STRIDE + 1
    psel = jnp.asarray(_np_psel(), bf16)
    mt = jnp.asarray(_np_cmp_to_sel_t(n_cmp, n_sel, n_sel, r), bf16)
    nc = NSA_HEADS * qb
    per_seq = lambda s: pl.BlockSpec((None,) + s, lambda b, i: (b, 0, 0))
    return pl.pallas_call(
        functools.partial(_nsa_prompt_kernel, qb=qb, seq_len=seq_len, tk=tk, tkp=tkp, wk=wk),
        grid=(b_sz, seq_len // qb),
        in_specs=[pl.BlockSpec((None, qb, NSA_WIDTH), lambda b, i: (b, i, 0)),
                  pl.BlockSpec((None, qb, LANES), lambda b, i: (b, i, 0)),
                  per_seq((r, FEAT)), per_seq((FEAT, r)),
                  per_seq((seq_len, FEAT)), per_seq((FEAT, seq_len)),
                  per_seq((seq_len, FEAT)), per_seq((FEAT, seq_len)),
                  pl.BlockSpec((NSA_HEADS, FEAT, NSA_WIDTH), lambda b, i: (0, 0, 0)),
                  pl.BlockSpec((n_sel, r), lambda b, i: (0, 0))],
        out_specs=pl.BlockSpec((None, qb, NSA_WIDTH), lambda b, i: (b, i, 0)),
        out_shape=SDS((b_sz, seq_len, NSA_WIDTH), bf16),
        scratch_shapes=[pltpu.VMEM((n_sel, nc), f32), pltpu.VMEM((1, nc), f32), pltpu.VMEM((1, nc), f32),
                        pltpu.VMEM((FEAT, nc), f32)],
        compiler_params=_cp("arbitrary", "arbitrary"), name="nsa_prompt",
    )(nq3, ng3, kc, vct, ksk, ksvt, kwk, kwvt, psel, mt)


NCS = LANES


def _page_halves(page_ref):
    k = jnp.concatenate([page_ref[0, h] for h in range(NSA_KV_HEADS)], axis=0)
    v = jnp.concatenate([page_ref[1, h] for h in range(NSA_KV_HEADS)], axis=0)
    return k, v


def _nsa_s1_kernel(pt_ref, *refs, n_pages, pps, ts, win_buf, rows):
    pages = refs[:pps]
    (newk_ref, newv_ref, q_ref, gate_ref, swin_ref, kwn_ref, pe_ref, w_ref, psel_ref, mt_ref, gsum_ref,
     qzt_ref, part_ref, bias_ref, xk_ref, xv_ref) = refs[pps:]
    j = pl.program_id(1)
    past = n_pages * PAGE_SIZE
    n_tail = xk_ref.shape[0] - past

    @pl.when(j == 0)
    def _fill_tail():
        for x_ref, new_ref in ((xk_ref, newk_ref), (xv_ref, newv_ref)):
            x_ref[pl.ds(past, n_tail), :] = jnp.zeros((n_tail, FEAT), f32)
            x_ref[pl.ds(past, ts), :] = new_ref[...]

    for k in range(pps):
        r0 = pl.multiple_of((j * pps + k) * PAGE_SIZE, PAGE_SIZE)
        kt, vt = _page_halves(pages[k])
        xk_ref[pl.ds(r0, PAGE_SIZE), :] = kt.T
        xv_ref[pl.ds(r0, PAGE_SIZE), :] = vt.T

    @pl.when(j == pl.num_programs(1) - 1)
    def _finish():
        n_rows = NSA_HEADS * ts
        qz = jnp.concatenate([_dot_nt(q_ref[...], psel_ref[hg]) for hg in range(NSA_HEADS)]
                             + [jnp.zeros((NCS - n_rows, FEAT), f32)], axis=0)
        qzt = (qz.T * Q_SCALE).astype(bf16)
        qzt_ref[...] = qzt
        t_col = past + lax.broadcasted_iota(i32, (1, NCS), 1) % ts
        total = past + ts
        n_sel = -(-total // SEL_BLOCK)
        n_cmp = n_sel * SEL_BLOCK // CMP_STRIDE - CMP_LEN // CMP_STRIDE + 1

        s = _dot(_compress_half(xk_ref, pe_ref, w_ref, 0, rows).astype(bf16), qzt)
        n_io = lax.broadcasted_iota(i32, s.shape, 0)
        valid = (n_io * CMP_STRIDE + (CMP_LEN - 1) <= t_col) & (n_io < n_cmp)
        pc, inv_c = _col_softmax(s, valid)
        o_c = _dot(_compress_half(xv_ref, pe_ref, w_ref, 1, rows).T.astype(bf16), pc) * inv_c
        imp_all = _dot(mt_ref[...], pc) * inv_c
        hi = imp_all.astype(bf16)
        r1 = imp_all - hi.astype(f32)
        mid = r1.astype(bf16)
        lo = (r1 - mid.astype(f32)).astype(bf16)
        gs = gsum_ref[...]
        imp = _dot(hi, gs) + _dot(mid, gs) + _dot(lo, gs)
        bias_ref[...] = _select_bias(imp, t_col, n_sel)

        wkt, wvt = _page_halves(swin_ref)
        kwn = jnp.concatenate([kwn_ref[...], jnp.zeros((ts, NSA_KV_WIDTH), f32)], axis=0).astype(bf16)
        s = jnp.concatenate([_dot_tn(wkt.astype(bf16), qzt), _dot(kwn[:, :FEAT], qzt)], axis=0)
        w_row = lax.broadcasted_iota(i32, s.shape, 0)
        w_pos = total - (win_buf + ts) + w_row
        dpos = t_col - w_pos
        pw, inv_w = _col_softmax(s, (dpos >= 0) & (dpos < WINDOW) & (w_pos >= 0) & (w_row < win_buf + ts))
        o_w = (_dot(wvt.astype(bf16), pw[:win_buf]) + _dot_tn(kwn[:, FEAT:], pw[win_buf:])) * inv_w

        sig = jax.nn.sigmoid(gate_ref[...])
        oc_r, ow_r = o_c.T, o_w.T
        part_ref[...] = jnp.concatenate(
            [sig[:, 3 * hg:3 * hg + 1] * oc_r[hg * ts:(hg + 1) * ts] + sig[:, 3 * hg + 2:3 * hg + 3] * ow_r[hg * ts:(hg + 1) * ts]
             for hg in range(NSA_HEADS)], axis=0)


def _nsa_s2_kernel(pt_ref, *refs, n_pages, pps, ts):
    pages = refs[:pps]
    news_ref, qzt_ref, bias_ref, part_ref, gate_ref, psel_ref, o_ref, m_ref, l_ref, acc_ref = refs[pps:]
    j = pl.program_id(1)
    past = n_pages * PAGE_SIZE
    qzt = qzt_ref[...]
    bpp = PAGE_SIZE // SEL_BLOCK

    @pl.when(j == 0)
    def _init():
        _flash_init(m_ref, l_ref, acc_ref)

    halves = [_page_halves(pages[k]) for k in range(pps)]
    kt = jnp.concatenate([h[0] for h in halves], axis=1).astype(bf16)
    vt = jnp.concatenate([h[1] for h in halves], axis=1).astype(bf16)
    _flash_step(_dot_tn(kt, qzt) + _block_bias(bias_ref, j * pps * bpp, pps * bpp, NCS), vt, m_ref, l_ref, acc_ref)

    @pl.when(j == pl.num_programs(1) - 1)
    def _finish():
        t_col = past + lax.broadcasted_iota(i32, (1, NCS), 1) % ts
        new = jnp.concatenate([news_ref[...], jnp.zeros((PAGE_SIZE - ts, NSA_KV_WIDTH), f32)], axis=0)
        s = _dot(new[:, :FEAT].astype(bf16), qzt) + _block_bias(bias_ref, n_pages * bpp, bpp, NCS)
        pos = past + lax.broadcasted_iota(i32, s.shape, 0)
        _flash_step(jnp.where(pos <= t_col, s, NEG_INF), new[:, FEAT:].T.astype(bf16), m_ref, l_ref, acc_ref)
        o_s = (acc_ref[...] * (1.0 / jnp.maximum(l_ref[...], 1e-30))).T
        sig = jax.nn.sigmoid(gate_ref[...])
        part = part_ref[...]
        out = jnp.zeros((ts, NSA_WIDTH), f32)
        for hg in range(NSA_HEADS):
            rs = slice(hg * ts, (hg + 1) * ts)
            mix = part[rs] + sig[:, 3 * hg + 1:3 * hg + 2] * o_s[rs]
            out = out + _dot(mix.astype(bf16), psel_ref[hg])
        o_ref[...] = out.astype(o_ref.dtype)


def _nsa_sample(nq3, ng3, kc3, vc3, kvs3, kvw3, cache_c_t, cache_s_t, swin_t, page_table, cw):
    bs, ts, _ = nq3.shape
    n_pages = page_table.shape[1]
    win_buf = swin_t.shape[-1]
    assert ts == SUBLANES and PAGE_SIZE % SEL_BLOCK == 0 and NSA_HEADS * ts <= NCS
    pps = math.gcd(n_pages, 16)
    steps = n_pages // pps
    past = n_pages * PAGE_SIZE
    total = past + ts
    n_sel = -(-total // SEL_BLOCK)
    n_cmp = n_sel * SEL_BLOCK // CMP_STRIDE - CMP_LEN // CMP_STRIDE + 1
    rows = -(-(n_cmp + 1) // SUBLANES) * SUBLANES
    x_rows = rows * CMP_STRIDE
    rb = -(-(n_sel + 1) // SUBLANES) * SUBLANES
    psel = jnp.asarray(_np_psel(), bf16)
    mt = jnp.asarray(_np_cmp_to_sel_t(n_cmp, n_sel, rb, rows), bf16)
    col = np.arange(NCS)
    gsum = ((col[:, None] // (NSA_GROUP * ts) == col[None, :] // (NSA_GROUP * ts)) & (col[:, None] % ts == col[None, :] % ts))
    gsum = jnp.asarray(gsum.astype(np.float32), bf16)
    pt_flat = page_table.reshape(-1).astype(i32)
    pe, w = cw
    page_shape = (2, NSA_KV_HEADS, NSA_HD, PAGE_SIZE)

    def page_spec(k):
        return pl.BlockSpec((None,) + page_shape, lambda b, j, pt, k=k: (pt[b * n_pages + j * pps + k], 0, 0, 0, 0))

    per_seq = lambda s: pl.BlockSpec((None,) + s, lambda b, j, pt: (b,) + (0,) * len(s))
    const = lambda s: pl.BlockSpec(s, lambda b, j, pt: (0,) * len(s))

    qzt, part, bias = pl.pallas_call(
        functools.partial(_nsa_s1_kernel, n_pages=n_pages, pps=pps, ts=ts, win_buf=win_buf, rows=rows),
        grid_spec=pltpu.PrefetchScalarGridSpec(
            num_scalar_prefetch=1, grid=(bs, steps),
            in_specs=[page_spec(k) for k in range(pps)]
            + [per_seq((ts, FEAT)), per_seq((ts, FEAT)), per_seq((ts, NSA_WIDTH)), per_seq((ts, LANES)),
               per_seq((2, NSA_KV_HEADS, NSA_HD, win_buf)), per_seq((ts, NSA_KV_WIDTH)), const(pe.shape), const(w.shape),
               const((NSA_HEADS, FEAT, NSA_WIDTH)), const((rb, rows)), const((NCS, NCS))],
            out_specs=[per_seq((FEAT, NCS)), per_seq((NSA_HEADS * ts, FEAT)), per_seq((rb, NCS))],
            scratch_shapes=[pltpu.VMEM((x_rows, FEAT), f32), pltpu.VMEM((x_rows, FEAT), f32)]),
        out_shape=[SDS((bs, FEAT, NCS), bf16), SDS((bs, NSA_HEADS * ts, FEAT), f32), SDS((bs, rb, NCS), f32)],
        compiler_params=_cp("arbitrary", "arbitrary"), name="nsa_sample_cmp",
    )(pt_flat, *([cache_c_t] * pps), kc3, vc3, nq3, ng3, swin_t, kvw3, pe, w, psel, mt, gsum)

    return pl.pallas_call(
        functools.partial(_nsa_s2_kernel, n_pages=n_pages, pps=pps, ts=ts),
        grid_spec=pltpu.PrefetchScalarGridSpec(
            num_scalar_prefetch=1, grid=(bs, steps),
            in_specs=[page_spec(k) for k in range(pps)]
            + [per_seq((ts, NSA_KV_WIDTH)), per_seq((FEAT, NCS)), per_seq((rb, NCS)), per_seq((NSA_HEADS * ts, FEAT)),
               per_seq((ts, LANES)), const((NSA_HEADS, FEAT, NSA_WIDTH))],
            out_specs=per_seq((ts, NSA_WIDTH)),
            scratch_shapes=[pltpu.VMEM((1, NCS), f32), pltpu.VMEM((1, NCS), f32), pltpu.VMEM((FEAT, NCS), f32)]),
        out_shape=SDS((bs, ts, NSA_WIDTH), bf16),
        compiler_params=_cp("arbitrary", "arbitrary"), name="nsa_sample_sel",
    )(pt_flat, *([cache_s_t] * pps), kvs3, qzt, bias, part, ng3, psel)


def _outproj_kernel(gla_ref, nsa_ref, x_ref, gm_ref, sh_ref, sc_ref, gf_ref, wo_ref, wrt_ref, br_ref,
                    x1_ref, h2_ref, idx_ref, tw_ref):
    y = _dot(gla_ref[...], wo_ref[:GLA_WIDTH, :]) + _dot(nsa_ref[...], wo_ref[GLA_WIDTH:, :])
    x1 = x_ref[...] + gm_ref[...] * y
    x1_ref[...] = x1
    h2 = _rms(x1, gf_ref[...]) * (1.0 + sc_ref[...]) + sh_ref[...]
    h2_ref[...] = h2
    s = _dot_nt(wrt_ref[...], h2.astype(bf16)) + br_ref[...]
    e_io = lax.broadcasted_iota(i32, s.shape, 0)
    vals, idxs = [], []
    for _ in range(TOP_K):
        mx = jnp.max(s, axis=0, keepdims=True)
        ix = jnp.min(jnp.where(s == mx, e_io, N_EXPERTS), axis=0, keepdims=True)
        vals.append(mx)
        idxs.append(ix)
        s = jnp.where(e_io == ix, -jnp.inf, s)
    ex = [jnp.exp(v - vals[0]) for v in vals]
    den = ex[0] + ex[1] + ex[2] + ex[3]
    idx_ref[...] = jnp.concatenate(idxs, axis=0)
    tw_ref[...] = jnp.concatenate([e / den for e in ex], axis=0)


def _outproj(gla_o, nsa_o, x2d, mod, g_ffn, w_out_b, w_router_t, b_router):
    t, tm = x2d.shape[0], mod.tm
    row = lambda i: (i, 0)
    const = lambda s: pl.BlockSpec(s, lambda i: (0, 0))
    return pl.pallas_call(
        _outproj_kernel, grid=(t // tm,),
        in_specs=[pl.BlockSpec((tm, GLA_WIDTH), row), pl.BlockSpec((tm, NSA_WIDTH), row), pl.BlockSpec((tm, D_MODEL), row),
                  mod.spec(2), mod.spec(3), mod.spec(4), const((1, D_MODEL)), const((MIX_WIDTH, D_MODEL)),
                  const((N_EXPERTS, D_MODEL)), const((N_EXPERTS, 1))],
        out_specs=[pl.BlockSpec((tm, D_MODEL), row), pl.BlockSpec((tm, D_MODEL), row),
                   pl.BlockSpec((TOP_K, tm), lambda i: (0, i)), pl.BlockSpec((TOP_K, tm), lambda i: (0, i))],
        out_shape=[SDS((t, D_MODEL), f32), SDS((t, D_MODEL), f32), SDS((TOP_K, t), i32), SDS((TOP_K, t), f32)],
        compiler_params=_cp("arbitrary"), name="outproj_router",
    )(gla_o, nsa_o, x2d, mod.arr, mod.arr, mod.arr, g_ffn.reshape(1, D_MODEL), w_out_b, w_router_t,
      b_router.reshape(N_EXPERTS, 1))


def _moe_kernel(be_ref, nv_ref, tokm_hbm, slotm_hbm, h2_hbm, wg_ref, bg_ref, wu_ref, bu_ref, wd_ref, bd_ref, out_hbm,
                tok_sm, slot_sm, xbuf, ybuf, wbuf, tsem, lsem, gsem, ssem):
    i = pl.program_id(0)
    nv = nv_ref[0]
    tm = xbuf.shape[1]
    pad_block = tokm_hbm.shape[0] - 1
    n_slab = 4
    sw, rg = D_EXPERT // n_slab, tm // n_slab

    def tok_copy(b):
        return pltpu.make_async_copy(tokm_hbm.at[b], tok_sm, tsem)

    def slot_copy(b):
        return pltpu.make_async_copy(slotm_hbm.at[b], slot_sm, lsem)

    def gather_row(par, r):
        pltpu.make_async_copy(h2_hbm.at[pl.ds(tok_sm[0, r], 1)], xbuf.at[par, pl.ds(r, 1)],
                              gsem.at[par]).start(priority=r % 2)

    def scatter_row(par, r):
        pltpu.make_async_copy(ybuf.at[par, pl.ds(r, 1)], out_hbm.at[pl.ds(slot_sm[0, r], 1)],
                              ssem.at[par]).start(priority=r % 2)

    def wait_gather(par):
        pltpu.make_async_copy(h2_hbm.at[pl.ds(0, tm)], xbuf.at[par], gsem.at[par]).wait()

    def wait_scatter(par):
        pltpu.make_async_copy(ybuf.at[par], out_hbm.at[pl.ds(0, tm)], ssem.at[par]).wait()

    @pl.when(i == 0)
    def _prologue():
        tok_copy(0).start()
        slot_copy(pad_block).start()
        ybuf[1] = jnp.zeros(ybuf.shape[1:], f32)
        tok_copy(0).wait()
        for r in range(tm):
            gather_row(0, r)
        tok_copy(1).start()

    @pl.when((i < nv) & ((i == 0) | (be_ref[i] != be_ref[jnp.maximum(i - 1, 0)])))
    def _new_expert():
        wbuf[0] = wg_ref[...].astype(bf16)
        wbuf[1] = wu_ref[...].astype(bf16)
        wbuf[2] = wd_ref[...].astype(bf16)

    for par in range(2):
        @pl.when((i < nv) & (i % 2 == par))
        def _run():
            wait_gather(par)
            tok_copy(i + 1).wait()
            slot_copy(jnp.where(i == 0, pad_block, i - 1)).wait()

            @pl.when(i >= 1)
            def _free_ybuf():
                wait_scatter(par)

            x = xbuf[par].astype(bf16)
            y = jnp.zeros((tm, D_MODEL), f32)
            for s in range(n_slab):
                cs = slice(s * sw, (s + 1) * sw)
                gate = jnp.minimum(_dot(x, wbuf[0, :, cs]) + bg_ref[:, cs], SWIGLU_LIMIT)
                up = jnp.clip(_dot(x, wbuf[1, :, cs]) + bu_ref[:, cs], -SWIGLU_LIMIT, SWIGLU_LIMIT)
                act = ((up + 1.0) * gate * jax.nn.sigmoid(SWIGLU_ALPHA * gate)).astype(bf16)
                y = y + _dot(act, wbuf[2, cs, :])
                for r in range(s * rg, (s + 1) * rg):
                    gather_row(1 - par, r)
                    scatter_row(1 - par, r)
            ybuf[par] = y + bd_ref[...]

            @pl.when(i + 2 <= nv)
            def _next_tokens():
                tok_copy(i + 2).start()

            slot_copy(i).start()

    @pl.when(i == nv)
    def _last_scatter():
        slot_copy(nv - 1).wait()
        for par in range(2):
            @pl.when(nv % 2 == par)
            def _():
                wait_gather(par)
                wait_scatter(par)
                for r in range(tm):
                    scatter_row(1 - par, r)
                wait_scatter(1 - par)


def _moe_experts(h2_all, tok_meta, slot_meta, block_e, n_valid, n_slots, wg, bg, wu, bu, wd, bd):
    n_blocks, _, tm = tok_meta.shape
    assert D_MODEL == D_EXPERT
    any_spec = pl.BlockSpec(memory_space=pl.ANY)
    wspec = lambda a, b: pl.BlockSpec((None, a, b), lambda i, be, nv: (be[i], 0, 0))
    return pl.pallas_call(
        _moe_kernel,
        grid_spec=pltpu.PrefetchScalarGridSpec(
            num_scalar_prefetch=2, grid=(n_blocks,),
            in_specs=[any_spec, any_spec, any_spec,
                      wspec(D_MODEL, D_EXPERT), wspec(1, D_EXPERT), wspec(D_MODEL, D_EXPERT), wspec(1, D_EXPERT),
                      wspec(D_EXPERT, D_MODEL), wspec(1, D_MODEL)],
            out_specs=any_spec,
            scratch_shapes=[pltpu.SMEM((1, tm), i32), pltpu.SMEM((1, tm), i32), pltpu.VMEM((2, tm, D_MODEL), f32),
                            pltpu.VMEM((2, tm, D_MODEL), f32), pltpu.VMEM((3, D_MODEL, D_EXPERT), bf16),
                            pltpu.SemaphoreType.DMA, pltpu.SemaphoreType.DMA,
                            pltpu.SemaphoreType.DMA((2,)), pltpu.SemaphoreType.DMA((2,))]),
        out_shape=SDS((n_slots, D_MODEL), f32),
        compiler_params=pltpu.CompilerParams(dimension_semantics=("arbitrary",), vmem_limit_bytes=MOE_VMEM_LIMIT_BYTES),
        name="moe_experts",
    )(block_e, n_valid, tok_meta, slot_meta, h2_all, wg, bg.reshape(N_EXPERTS, 1, D_EXPERT), wu,
      bu.reshape(N_EXPERTS, 1, D_EXPERT), wd, bd.reshape(N_EXPERTS, 1, D_MODEL))


def _moe_plan(idx_t, tm, t_pad):
    n_tok = idx_t.shape[1]
    n_assign = n_tok * TOP_K
    flat_e = idx_t.reshape(n_assign)
    order = jnp.argsort(flat_e).astype(i32)
    counts = jnp.sum((flat_e[:, None] == jnp.arange(N_EXPERTS, dtype=i32)[None, :]).astype(i32), axis=0)
    padded = (counts + tm - 1) // tm * tm
    start = jnp.cumsum(counts) - counts
    pad_end = jnp.cumsum(padded)
    pad_start = pad_end - padded
    n_blocks = (n_assign + N_EXPERTS * (tm - 1) + tm - 1) // tm + 1
    blk0 = jnp.arange(n_blocks, dtype=i32) * tm
    block_e = jnp.minimum(jnp.sum((pad_end[None, :] <= blk0[:, None]).astype(i32), axis=1), N_EXPERTS - 1).astype(i32)
    n_valid = (pad_end[-1] // tm).astype(i32).reshape(1)
    r_in = jnp.arange(tm, dtype=i32)[None, :]
    j = blk0[:, None] + r_in - pad_start[block_e][:, None]
    valid = (j < counts[block_e][:, None]) & (blk0[:, None] < pad_end[-1])
    a = order[jnp.clip(start[block_e][:, None] + j, 0, n_assign - 1)]
    tok = a % n_tok
    slot = (a // n_tok) * t_pad + tok
    tok_meta = jnp.where(valid, tok, 0).astype(i32).reshape(n_blocks, 1, tm)
    slot_meta = jnp.where(valid, slot, TOP_K * t_pad + r_in).astype(i32).reshape(n_blocks, 1, tm)
    return tok_meta, slot_meta, block_e, n_valid


def _final_kernel(x1_ref, *refs):
    y_refs, (tw_ref, gate_ref, g_ref, o_ref) = refs[:TOP_K], refs[TOP_K:]
    tw = tw_ref[...]
    f = tw[:, 0:1] * y_refs[0][...]
    for k in range(1, TOP_K):
        f = f + tw[:, k:k + 1] * y_refs[k][...]
    o_ref[...] = _rms(x1_ref[...] + gate_ref[...] * f, g_ref[...])


def _final(x1, y4, tw, mod, g_final, row0, t_pad):
    t, tm = x1.shape[0], mod.tm
    assert row0 % tm == 0 and t_pad % tm == 0
    b0 = row0 // tm
    row = lambda i: (i, 0)
    y_specs = [pl.BlockSpec((tm, D_MODEL), lambda i, k=k: (k * (t_pad // tm) + b0 + i, 0)) for k in range(TOP_K)]
    return pl.pallas_call(
        _final_kernel, grid=(t // tm,),
        in_specs=[pl.BlockSpec((tm, D_MODEL), row)] + y_specs
        + [pl.BlockSpec((tm, TOP_K), lambda i: (i + b0, 0)), mod.spec(5), pl.BlockSpec((1, D_MODEL), lambda i: (0, 0))],
        out_specs=pl.BlockSpec((tm, D_MODEL), row), out_shape=SDS((t, D_MODEL), f32),
        compiler_params=_cp("arbitrary"), name="combine_final_norm",
    )(x1, *([y4] * TOP_K), tw, mod.arr, g_final.reshape(1, D_MODEL))


def _prep_w_in(w_in):
    gq, gk, gv, gr, ga, nq, nkc, nks, nkw, ng = jnp.split(w_in, np.cumsum(IN_SPLITS)[:-1].tolist(), axis=1)
    pad = lambda a: jnp.pad(a, ((0, 0), (0, LANES - a.shape[1])))
    w_p = jnp.concatenate([gq, gk, gv, gr, nq, nkc, nks, nkw, pad(ga), pad(ng)], axis=1).astype(bf16)
    w_kv_t = jnp.concatenate([nkc, nks, nkw], axis=1).T.astype(bf16)
    return w_p, w_kv_t


def _feature_major(a, rows_axis):
    return jnp.moveaxis(a, rows_axis, -1)


def kernel(x_prompt, x_sample, c_prompt, c_sample, cache_cmp, cache_sel, state_win, state_gla, page_table, w_ada, b_ada, g_mix, g_ffn, w_in, w_a2, b_a, g_gla, phi_pe, phi_w, w_out, w_router, b_router, w_gate, b_gate, w_up, b_up, w_down, b_down, g_final):
    bp, sp = x_prompt.shape[:2]
    bs, ts = x_sample.shape[:2]
    tp, tsn = bp * sp, bs * ts
    win_buf = state_win.shape[2]
    l = 0

    mod = _adaln(jnp.concatenate([c_prompt, c_sample], axis=0), w_ada[l], b_ada[l])
    tm_p = math.gcd(sp, 512)
    tm_s = math.gcd(tsn, 512)
    mod_p = _Mod(mod[:bp].reshape(bp, 6, 1, D_MODEL), False, tm_p, sp // tm_p)
    mod_s = _Mod(jnp.repeat(mod[bp:].reshape(bs, 6, D_MODEL).transpose(1, 0, 2), ts, axis=1), True, tm_s, None)

    w_in_p, w_kv_t = _prep_w_in(w_in[l])
    cw = _prep_compress(phi_pe[l], phi_w[l])
    w_out_b = w_out[l].astype(bf16)
    w_router_t = w_router[l].T.astype(bf16)
    xp2, xs2 = x_prompt.reshape(tp, D_MODEL), x_sample.reshape(tsn, D_MODEL)

    (gq, gk, gv, gr, nq, kc, vc, ga, ng, kvc_t, kvs_t, kvw_t, ksk, kwk, ksvt, kwvt) = _inproj(
        xp2, mod_p, g_mix[l], w_in_p, w_kv_t, sp)
    gla_o, st_p = _gla(gq, gk, gv, gr, ga, jnp.zeros((bp, GLA_HEADS, GLA_DV, GLA_DK), f32), w_a2[l], b_a[l], g_gla[l], bp, sp)
    kcm, vct = _cmp_prompt(kc.reshape(bp, sp, FEAT), vc.reshape(bp, sp, FEAT), cw)
    nsa_o = _nsa_prompt(nq.reshape(bp, sp, NSA_WIDTH), ng.reshape(bp, sp, LANES), kcm, vct,
                        ksk.reshape(bp, sp, FEAT), ksvt, kwk.reshape(bp, sp, FEAT), kwvt)
    x1_p, h2_p, idx_p, tw_p = _outproj(gla_o.reshape(tp, GLA_WIDTH), nsa_o.reshape(tp, NSA_WIDTH), xp2, mod_p,
                                       g_ffn[l], w_out_b, w_router_t, b_router[l])
    token_major = lambda a: jnp.moveaxis(a, -1, 1)[None]
    new_cmp_p = token_major(kvc_t)
    new_sel_p = token_major(kvs_t)
    new_win_p = token_major(jnp.pad(kvw_t, ((0, 0),) * 4 + ((win_buf, 0),))[..., -win_buf:])
    new_gla_p = jnp.swapaxes(st_p, 2, 3)[None]

    gq, gk, gv, gr, nq, kc, vc, ks, vs, kw, vw, ga, ng = _inproj(xs2, mod_s, g_mix[l], w_in_p)
    gla_os, st_s = _gla(gq, gk, gv, gr, ga, jnp.swapaxes(state_gla[l], 2, 3), w_a2[l], b_a[l], g_gla[l], bs, ts)
    r3 = lambda a: a.reshape(bs, ts, a.shape[-1])
    kvc3, kvs3, kvw3 = (jnp.concatenate([r3(k_), r3(v_)], axis=-1) for k_, v_ in ((kc, vc), (ks, vs), (kw, vw)))
    swin_t = _feature_major(state_win[l], 1)
    nsa_os = _nsa_sample(r3(nq), r3(ng), r3(kc), r3(vc), kvs3, kvw3,
                         _feature_major(cache_cmp[l], 1), _feature_major(cache_sel[l], 1), swin_t, page_table, cw)
    x1_s, h2_s, idx_s, tw_s = _outproj(gla_os.reshape(tsn, GLA_WIDTH), nsa_os.reshape(tsn, NSA_WIDTH), xs2, mod_s,
                                       g_ffn[l], w_out_b, w_router_t, b_router[l])
    kv_row = (2, NSA_KV_HEADS, NSA_HD)
    new_cmp_s = kvc3.reshape((1, bs, ts) + kv_row)
    new_sel_s = kvs3.reshape((1, bs, ts) + kv_row)
    kw_t = _feature_major(kvw3.reshape((bs, ts) + kv_row), 1)
    new_win_s = token_major(jnp.concatenate([swin_t, kw_t], axis=-1)[..., -win_buf:])
    new_gla_s = jnp.swapaxes(st_s, 2, 3)[None]

    n_tok = tp + tsn
    t_pad = -(-n_tok // tm_p) * tm_p
    tok_meta, slot_meta, block_e, n_valid = _moe_plan(jnp.concatenate([idx_p, idx_s], axis=1), MOE_TM, t_pad)
    y4 = _moe_experts(jnp.concatenate([h2_p, h2_s], axis=0), tok_meta, slot_meta, block_e, n_valid,
                      TOP_K * t_pad + MOE_TM, w_gate[l], b_gate[l], w_up[l], b_up[l], w_down[l], b_down[l])
    tw_all = jnp.concatenate([tw_p, tw_s], axis=1).T
    y_p = _final(x1_p, y4, tw_all, mod_p, g_final, 0, t_pad).reshape(bp, sp, D_MODEL)
    y_s = _final(x1_s, y4, tw_all, mod_s, g_final, tp, t_pad).reshape(bs, ts, D_MODEL)
    return (y_p, y_s, new_cmp_p, new_sel_p, new_win_p, new_gla_p, new_cmp_s, new_sel_s, new_win_s, new_gla_s)
```

```python
import functools
import math

import numpy as np
import jax
import jax.numpy as jnp
from jax import lax
from jax.experimental import pallas as pl
from jax.experimental.pallas import tpu as pltpu

f32, bf16, i32 = jnp.float32, jnp.bfloat16, jnp.int32
SDS = jax.ShapeDtypeStruct

D_MODEL = 1024
GLA_HEADS, GLA_DK, GLA_DV, GLA_GATE_RANK, GLA_TAU, GLA_CHUNK = 4, 64, 128, 16, 16.0, 64
NSA_HEADS, NSA_KV_HEADS, NSA_HD = 8, 2, 64
NSA_GROUP = NSA_HEADS // NSA_KV_HEADS
CMP_LEN, CMP_STRIDE, SEL_BLOCK, N_SEL, WINDOW, NSA_QBLOCK = 32, 16, 64, 8, 512, 128
N_EXPERTS, TOP_K, D_EXPERT = 32, 4, 1024
SWIGLU_LIMIT, SWIGLU_ALPHA = 7.0, 1.702
RMS_EPS, NEG_INF, FORCE_SCORE = 1e-6, -1e30, 1e4
PAGE_SIZE = 128

GLA_QK_WIDTH = GLA_HEADS * GLA_DK
GLA_WIDTH = GLA_HEADS * GLA_DV
NSA_WIDTH = NSA_HEADS * NSA_HD
NSA_KV_WIDTH = 2 * NSA_KV_HEADS * NSA_HD
IN_SPLITS = (GLA_QK_WIDTH, GLA_QK_WIDTH, GLA_WIDTH, GLA_WIDTH, GLA_GATE_RANK,
             NSA_WIDTH, NSA_KV_WIDTH, NSA_KV_WIDTH, NSA_KV_WIDTH, 3 * NSA_HEADS)
MIX_WIDTH = GLA_WIDTH + NSA_WIDTH

LANES = 128
SUBLANES = 8
VMEM_LIMIT_BYTES = 48 * 1024 * 1024
MOE_VMEM_LIMIT_BYTES = 56 * 1024 * 1024
D_CHUNKS = D_MODEL // LANES

FEAT = NSA_KV_HEADS * NSA_HD
_IN_COLS = (("gq", 0, 256, f32), ("gk", 256, 256, f32), ("gv", 512, 512, bf16), ("gr", 1024, 512, f32),
            ("nq", 1536, 512, bf16), ("kc", 2048, FEAT, f32), ("vc", 2176, FEAT, f32), ("ks", 2304, FEAT, f32),
            ("vs", 2432, FEAT, f32), ("kw", 2560, FEAT, f32), ("vw", 2688, FEAT, f32),
            ("ga", 2816, LANES, f32), ("ng", 2944, LANES, f32))
_IN_WIDTH = 3072
_KV0 = 2048
MOE_TM = 512
Q_SCALE = NSA_HD ** -0.5 * math.log2(math.e)


def _cp(*sem):
    return pltpu.CompilerParams(dimension_semantics=sem, vmem_limit_bytes=VMEM_LIMIT_BYTES)


def _dot(a, b):
    return jnp.dot(a, b, preferred_element_type=f32)


def _dot_nt(a, b):
    return lax.dot_general(a, b, (((1,), (1,)), ((), ())), preferred_element_type=f32)


def _dot_tn(a, b):
    return lax.dot_general(a, b, (((0,), (0,)), ((), ())), preferred_element_type=f32)


def _rms(x, g):
    return x * lax.rsqrt(jnp.mean(x * x, axis=-1, keepdims=True) + RMS_EPS) * g


def _silu(x):
    return x * jax.nn.sigmoid(x)


def _adaln_kernel(c_ref, w_ref, b_ref, o_ref):
    s = _silu(c_ref[...]).astype(bf16)
    o_ref[...] = _dot(s, w_ref[...].astype(bf16)) + b_ref[...]


def _adaln(c_all, w_ada, b_ada):
    nb, n = c_all.shape[0], w_ada.shape[1]
    tn = 1024
    return pl.pallas_call(
        _adaln_kernel, grid=(n // tn,),
        in_specs=[pl.BlockSpec((nb, D_MODEL), lambda j: (0, 0)),
                  pl.BlockSpec((D_MODEL, tn), lambda j: (0, j)),
                  pl.BlockSpec((1, tn), lambda j: (0, j))],
        out_specs=pl.BlockSpec((nb, tn), lambda j: (0, j)),
        out_shape=SDS((nb, n), f32), compiler_params=_cp("arbitrary"), name="adaln",
    )(c_all, w_ada, b_ada.reshape(1, n))


class _Mod:
    def __init__(self, arr, per_token, tm, tiles_per_seq):
        self.arr, self.per_token, self.tm, self.tps = arr, per_token, tm, tiles_per_seq

    def spec(self, k):
        if self.per_token:
            return pl.BlockSpec((None, self.tm, D_MODEL), lambda i: (k, i, 0))
        tps = self.tps
        return pl.BlockSpec((None, None, 1, D_MODEL), lambda i: (i // tps, k, 0, 0))


def _in_cols(feature_major):
    return tuple(c for c in _IN_COLS if not (feature_major and c[0] in ("ks", "vs", "kw", "vw")))


def _inproj_kernel(x_ref, sh_ref, sc_ref, g_ref, w_ref, *rest, feature_major):
    y = _rms(x_ref[...], g_ref[...])
    h = (y * (1.0 + sc_ref[...]) + sh_ref[...]).astype(bf16)
    outs = rest[1:] if feature_major else rest
    cols = _in_cols(feature_major)
    for o_ref, (_, c0, w, _) in zip(outs, cols):
        o_ref[...] = _dot(h, w_ref[:, c0:c0 + w]).astype(o_ref.dtype)
    if feature_major:
        wt_ref = rest[0]
        kvt_refs, (ksk_ref, kwk_ref, ksvt_ref, kwvt_ref) = outs[len(cols):-4], outs[-4:]
        kvt = _dot_nt(wt_ref[...], h)
        for a, kvt_ref in enumerate(kvt_refs):
            for ch in range(2 * NSA_KV_HEADS):
                r0 = a * NSA_KV_WIDTH + ch * NSA_HD
                kvt_ref[ch // NSA_KV_HEADS, ch % NSA_KV_HEADS] = kvt[r0:r0 + NSA_HD]
        ksvt_ref[...] = kvt[NSA_KV_WIDTH + FEAT:2 * NSA_KV_WIDTH].astype(bf16)
        kwvt_ref[...] = kvt[2 * NSA_KV_WIDTH + FEAT:].astype(bf16)
        ksk_ref[...] = _dot(h, w_ref[:, _KV0 + NSA_KV_WIDTH:_KV0 + NSA_KV_WIDTH + FEAT]).astype(bf16)
        kwk_ref[...] = _dot(h, w_ref[:, _KV0 + 2 * NSA_KV_WIDTH:_KV0 + 2 * NSA_KV_WIDTH + FEAT]).astype(bf16)


def _inproj(x2d, mod, g_mix, w_in_p, w_kv_t=None, seq_len=None):
    t, tm = x2d.shape[0], mod.tm
    row = lambda i: (i, 0)
    in_specs = [pl.BlockSpec((tm, D_MODEL), row), mod.spec(0), mod.spec(1),
                pl.BlockSpec((1, D_MODEL), lambda i: (0, 0)),
                pl.BlockSpec((D_MODEL, _IN_WIDTH), lambda i: (0, 0))]
    cols = _in_cols(w_kv_t is not None)
    out_specs = [pl.BlockSpec((tm, w), row) for (_, _, w, _) in cols]
    out_shape = [SDS((t, w), dt) for (_, _, w, dt) in cols]
    args = [x2d, mod.arr, mod.arr, g_mix.reshape(1, D_MODEL), w_in_p]
    if w_kv_t is not None:
        b_sz, tps = t // seq_len, seq_len // tm
        in_specs.append(pl.BlockSpec((3 * NSA_KV_WIDTH, D_MODEL), lambda i: (0, 0)))
        args.append(w_kv_t)
        fm = lambda w: pl.BlockSpec((None, w, tm), lambda i: (i // tps, 0, i % tps))
        out_specs += [pl.BlockSpec((None, 2, NSA_KV_HEADS, NSA_HD, tm), lambda i: (i // tps, 0, 0, 0, i % tps))] * 3
        out_specs += [pl.BlockSpec((tm, FEAT), row), pl.BlockSpec((tm, FEAT), row), fm(FEAT), fm(FEAT)]
        out_shape += [SDS((b_sz, 2, NSA_KV_HEADS, NSA_HD, seq_len), f32)] * 3
        out_shape += [SDS((t, FEAT), bf16), SDS((t, FEAT), bf16),
                      SDS((b_sz, FEAT, seq_len), bf16), SDS((b_sz, FEAT, seq_len), bf16)]
    return pl.pallas_call(
        functools.partial(_inproj_kernel, feature_major=w_kv_t is not None), grid=(t // tm,),
        in_specs=in_specs, out_specs=out_specs, out_shape=out_shape,
        compiler_params=_cp("arbitrary"), name="inproj",
    )(*args)


def _gla_kernel(q_ref, k_ref, v_ref, r_ref, a_ref, s0_ref, wa2_ref, ba_ref, gg_ref, o_ref, st_ref, *, chunk, n_chunks):
    @pl.when(pl.program_id(1) == 0)
    def _init():
        st_ref[...] = s0_ref[...]

    ri = lax.broadcasted_iota(i32, (chunk, chunk), 0)
    ci = lax.broadcasted_iota(i32, (chunk, chunk), 1)
    causal = ri >= ci
    tril = jnp.where(causal, 1.0, 0.0).astype(bf16)

    def body(c, carry):
        rows = pl.ds(pl.multiple_of(c * chunk, chunk), chunk)
        for sq in range(q_ref.shape[0]):
            a_low = a_ref[sq, rows, :][:, :GLA_GATE_RANK].astype(bf16)
            z = _dot(a_low, wa2_ref[...]) + ba_ref[...]
            log_a = (jnp.minimum(z, 0.0) - jnp.log1p(jnp.exp(-jnp.abs(z)))) * (1.0 / GLA_TAU)
            hi = log_a.astype(bf16)
            lo = (log_a - hi.astype(f32)).astype(bf16)
            cum = _dot(tril, hi) + _dot(tril, lo)
            last = cum[chunk - 1:chunk, :]
            q = q_ref[sq, rows, :] * (GLA_DK ** -0.5)
            k = k_ref[sq, rows, :]
            qd = (q * jnp.exp(cum)).astype(bf16)
            ki = (k * jnp.exp(-cum)).astype(bf16)
            ke = (k * jnp.exp(last - cum)).astype(bf16)
            dec = jnp.exp(last)
            v = v_ref[sq, rows, :]
            r = r_ref[sq, rows, :]
            for h in range(GLA_HEADS):
                sk = slice(h * GLA_DK, (h + 1) * GLA_DK)
                sv = slice(h * GLA_DV, (h + 1) * GLA_DV)
                qh, kih, keh, vh = qd[:, sk], ki[:, sk], ke[:, sk], v[:, sv]
                att = jnp.where(causal, _dot_nt(qh, kih), 0.0)
                st = st_ref[sq, h]
                o = _dot(att.astype(bf16), vh) + _dot_nt(qh, st.astype(bf16))
                st_ref[sq, h] = st * dec[:, sk] + _dot_tn(vh, keh)
                o_ref[sq, rows, sv] = (_rms(o, gg_ref[...]) * _silu(r[:, sv])).astype(o_ref.dtype)
        return carry

    lax.fori_loop(0, n_chunks, body, 0)


def _gla(gq, gk, gv, gr, ga, s0t, w_a2, b_a, g_gla, b_sz, seq_len):
    chunk = math.gcd(seq_len, GLA_CHUNK)
    tb = min(seq_len, 512)
    n_chunks = tb // chunk
    nb = math.gcd(b_sz, 4 if tb >= 512 else 8)
    r3 = lambda a: a.reshape(b_sz, seq_len, a.shape[-1])
    tok = lambda w: pl.BlockSpec((nb, tb, w), lambda b, j: (b, j, 0))
    st_spec = pl.BlockSpec((nb, GLA_HEADS, GLA_DV, GLA_DK), lambda b, j: (b, 0, 0, 0))
    const = lambda s: pl.BlockSpec(s, lambda b, j: (0, 0))
    return pl.pallas_call(
        functools.partial(_gla_kernel, chunk=chunk, n_chunks=n_chunks),
        grid=(b_sz // nb, seq_len // tb),
        in_specs=[tok(GLA_QK_WIDTH), tok(GLA_QK_WIDTH), tok(GLA_WIDTH), tok(GLA_WIDTH), tok(LANES), st_spec,
                  const((GLA_GATE_RANK, GLA_QK_WIDTH)), const((1, GLA_QK_WIDTH)), const((1, GLA_DV))],
        out_specs=[tok(GLA_WIDTH), st_spec],
        out_shape=[SDS((b_sz, seq_len, GLA_WIDTH), bf16), SDS((b_sz, GLA_HEADS, GLA_DV, GLA_DK), f32)],
        compiler_params=_cp("arbitrary", "arbitrary"), name="gla",
    )(r3(gq), r3(gk), r3(gv), r3(gr), r3(ga), s0t, w_a2.astype(bf16), b_a.reshape(1, -1), g_gla.reshape(1, -1))


def _np_psel():
    p = np.zeros((NSA_HEADS, FEAT, NSA_WIDTH), np.float32)
    for h in range(NSA_KV_HEADS):
        for g in range(NSA_GROUP):
            for d in range(NSA_HD):
                p[h * NSA_GROUP + g, h * NSA_HD + d, (h * NSA_GROUP + g) * NSA_HD + d] = 1.0
    return p


def _np_cmp_to_sel_t(n_cmp, n_sel, rows, cols):
    i0 = np.arange(n_cmp)[None, :] * CMP_STRIDE
    j0 = np.arange(n_sel)[:, None] * SEL_BLOCK
    m = np.zeros((rows, cols), np.float32)
    m[:n_sel, :n_cmp] = ((i0 < j0 + SEL_BLOCK) & (i0 + CMP_LEN > j0)).astype(np.float32)
    return m


def _prep_compress(phi_pe, phi_w):
    w = phi_w.reshape(2, CMP_LEN, NSA_HD, NSA_HD)
    wb = jnp.einsum('cldo,hk->clhdko', w, jnp.eye(NSA_KV_HEADS, dtype=f32)).reshape(2, CMP_LEN, FEAT, FEAT)
    pe = jnp.broadcast_to(phi_pe[:, :, None, :], (2, CMP_LEN, NSA_KV_HEADS, NSA_HD)).reshape(2, CMP_LEN, FEAT)
    pe_w = jnp.einsum('clf,clfo->clo', pe, wb, precision=lax.Precision.HIGHEST)
    bias = jnp.concatenate([pe_w[:, :CMP_STRIDE].sum(axis=1), pe_w[:, CMP_STRIDE:].sum(axis=1)], axis=-1)
    pairs = wb.reshape(2, 2, CMP_STRIDE // 2, 2 * FEAT, FEAT)
    w_pairs = jnp.concatenate([pairs[:, 0], pairs[:, 1]], axis=-1)
    return bias.reshape(2, 1, 2 * FEAT), w_pairs.astype(bf16)


def _compress_half(x_ref, bias_ref, w_ref, c, rows):
    acc = jnp.zeros((rows, 2 * FEAT), f32)
    for p in range(CMP_STRIDE // 2):
        xa = x_ref[pl.ds(2 * p, rows, stride=CMP_STRIDE), :]
        xb = x_ref[pl.ds(2 * p + 1, rows, stride=CMP_STRIDE), :]
        acc = acc + _dot(jnp.concatenate([xa, xb], axis=1).astype(bf16), w_ref[c, p])
    acc = acc + bias_ref[c]
    return acc[:, :FEAT] + pltpu.roll(acc[:, FEAT:], rows - 1, 0)


def _col_softmax(s, valid, every_column_valid=False):
    s = jnp.where(valid, s, NEG_INF)
    m = jnp.max(s, axis=0, keepdims=True)
    p = jnp.exp2(s - m)
    if not every_column_valid:
        p = jnp.where(valid, p, 0.0)
    l = jnp.sum(p, axis=0, keepdims=True)
    return p.astype(bf16), 1.0 / jnp.maximum(l, 1e-30)


def _select_bias(imp, t_col, n_sel):
    blk = lax.broadcasted_iota(i32, imp.shape, 0)
    cur = lax.shift_right_logical(t_col, int(math.log2(SEL_BLOCK)))
    forced = (blk == 0) | (blk == cur) | (blk == cur - 1)
    future = blk * SEL_BLOCK > t_col
    score = jnp.where(future, NEG_INF, jnp.where(forced, FORCE_SCORE, imp))
    score = jnp.where(blk < n_sel, score, -jnp.inf)
    bias = jnp.full(imp.shape, NEG_INF, f32)
    for _ in range(min(N_SEL, n_sel)):
        mx = jnp.max(score, axis=0, keepdims=True)
        idx = jnp.min(jnp.where(score == mx, blk, 2 ** 30), axis=0, keepdims=True)
        pick = blk == idx
        bias = jnp.where(pick, 0.0, bias)
        score = jnp.where(pick, -jnp.inf, score)
    return bias


def _block_bias(bias_ref, first_block, n_blocks, cols):
    return jnp.concatenate(
        [jnp.broadcast_to(bias_ref[pl.ds(first_block + b, 1), :], (SEL_BLOCK, cols)) for b in range(n_blocks)], axis=0)


def _flash_step(s, vt, m_ref, l_ref, acc_ref):
    m_old = m_ref[...]
    m_new = jnp.maximum(m_old, jnp.max(s, axis=0, keepdims=True))
    alpha = jnp.exp2(m_old - m_new)
    p = jnp.exp2(s - m_new)
    l_ref[...] = alpha * l_ref[...] + jnp.sum(p, axis=0, keepdims=True)
    acc_ref[...] = alpha * acc_ref[...] + _dot(vt, p.astype(bf16))
    m_ref[...] = m_new


def _flash_init(m_ref, l_ref, acc_ref):
    m_ref[...] = jnp.full(m_ref.shape, NEG_INF, f32)
    l_ref[...] = jnp.zeros(l_ref.shape, f32)
    acc_ref[...] = jnp.zeros(acc_ref.shape, f32)


def _cmp_prompt_kernel(k_ref, v_ref, pe_ref, w_ref, kc_ref, vct_ref):
    rows = kc_ref.shape[0]
    kc_ref[...] = _compress_half(k_ref, pe_ref, w_ref, 0, rows).astype(bf16)
    vct_ref[...] = _compress_half(v_ref, pe_ref, w_ref, 1, rows).T.astype(bf16)


def _cmp_prompt(kc3, vc3, cw):
    b_sz, seq_len, _ = kc3.shape
    r = seq_len // CMP_STRIDE
    pe, w = cw
    return pl.pallas_call(
        _cmp_prompt_kernel, grid=(b_sz,),
        in_specs=[pl.BlockSpec((None, seq_len, FEAT), lambda b: (b, 0, 0)), pl.BlockSpec((None, seq_len, FEAT), lambda b: (b, 0, 0)),
                  pl.BlockSpec(pe.shape, lambda b: (0,) * pe.ndim), pl.BlockSpec(w.shape, lambda b: (0,) * w.ndim)],
        out_specs=[pl.BlockSpec((None, r, FEAT), lambda b: (b, 0, 0)), pl.BlockSpec((None, FEAT, r), lambda b: (b, 0, 0))],
        out_shape=[SDS((b_sz, r, FEAT), bf16), SDS((b_sz, FEAT, r), bf16)],
        compiler_params=_cp("arbitrary"), name="cmp_prompt",
    )(kc3, vc3, pe, w)


def _nsa_prompt_kernel(q_ref, gate_ref, kc_ref, vct_ref, ksk_ref, ksvt_ref, kwk_ref, kwvt_ref, psel_ref, mt_ref,
                       o_ref, bias_ref, m_ref, l_ref, acc_ref, *, qb, seq_len, tk, tkp, wk):
    i = pl.program_id(1)
    start = i * qb
    nc = NSA_HEADS * qb
    n_sel = seq_len // SEL_BLOCK
    n_cmp = seq_len // CMP_STRIDE - CMP_LEN // CMP_STRIDE + 1
    q = q_ref[...]
    qzt = jnp.concatenate([_dot_nt(psel_ref[hg], q) for hg in range(NSA_HEADS)], axis=1)
    qzt = (qzt * Q_SCALE).astype(bf16)
    t_col = start + lax.broadcasted_iota(i32, (1, nc), 1) % qb

    s = _dot(kc_ref[...], qzt)
    n_io = lax.broadcasted_iota(i32, s.shape, 0)
    valid = (n_io * CMP_STRIDE + (CMP_LEN - 1) <= t_col) & (n_io < n_cmp)
    pc, inv_c = _col_softmax(s, valid)
    o_c = _dot(vct_ref[...], pc) * inv_c
    imp_all = _dot(mt_ref[...], pc) * inv_c
    imp = []
    for h in range(NSA_KV_HEADS):
        tot = imp_all[:, h * NSA_GROUP * qb:(h * NSA_GROUP + 1) * qb]
        for g in range(1, NSA_GROUP):
            tot = tot + imp_all[:, (h * NSA_GROUP + g) * qb:(h * NSA_GROUP + g + 1) * qb]
        imp.append(tot)
    bias = _select_bias(jnp.concatenate(imp, axis=1), t_col[:, :NSA_KV_HEADS * qb], n_sel)
    bias_ref[...] = jnp.concatenate([bias[:, h * qb:(h + 1) * qb] for h in range(NSA_KV_HEADS) for _ in range(NSA_GROUP)], axis=1)

    _flash_init(m_ref, l_ref, acc_ref)

    def scores(jt, width):
        k0 = pl.multiple_of(jt * width, width)
        bias = _block_bias(bias_ref, jt * (width // SEL_BLOCK), width // SEL_BLOCK, nc)
        return k0, _dot(ksk_ref[pl.ds(k0, width), :], qzt) + bias

    def past_tile(jt, carry):
        k0, s = scores(jt, tkp)
        _flash_step(s, ksvt_ref[:, pl.ds(k0, tkp)], m_ref, l_ref, acc_ref)
        return carry

    def causal_tile(jt, carry):
        k0, s = scores(jt, tk)
        pos = k0 + lax.broadcasted_iota(i32, s.shape, 0)
        _flash_step(jnp.where(pos <= t_col, s, NEG_INF), ksvt_ref[:, pl.ds(k0, tk)], m_ref, l_ref, acc_ref)
        return carry

    n_past = start // tkp
    lax.fori_loop(0, n_past, past_tile, 0)
    lax.fori_loop(n_past * (tkp // tk), (start + qb + tk - 1) // tk, causal_tile, 0)
    o_s = acc_ref[...] * (1.0 / jnp.maximum(l_ref[...], 1e-30))

    ks = pl.multiple_of(jnp.clip(start - WINDOW, 0, seq_len - wk), qb)
    s = _dot(kwk_ref[pl.ds(ks, wk), :], qzt)
    dpos = t_col - (ks + lax.broadcasted_iota(i32, s.shape, 0))
    pw, inv_w = _col_softmax(s, lax.bitcast_convert_type(dpos, jnp.uint32) < WINDOW, every_column_valid=True)
    o_w = _dot(kwvt_ref[:, pl.ds(ks, wk)], pw) * inv_w

    gt = jax.nn.sigmoid(gate_ref[...]).T
    out = jnp.zeros((qb, NSA_WIDTH), f32)
    for hg in range(NSA_HEADS):
        cs = slice(hg * qb, (hg + 1) * qb)
        mix = gt[3 * hg:3 * hg + 1] * o_c[:, cs] + gt[3 * hg + 1:3 * hg + 2] * o_s[:, cs] + gt[3 * hg + 2:3 * hg + 3] * o_w[:, cs]
        out = out + _dot(mix.T.astype(bf16), psel_ref[hg])
    o_ref[...] = out.astype(o_ref.dtype)


def _nsa_prompt(nq3, ng3, kc, vct, ksk, ksvt, kwk, kwvt):
    b_sz, seq_len, _ = nq3.shape
    qb = math.gcd(seq_len, NSA_QBLOCK)
    tk = 2 * qb
    tkp = 2 * tk if seq_len % (2 * tk) == 0 else tk
    assert seq_len % tk == 0 and tk % SEL_BLOCK == 0
    wk = min(WINDOW + qb, seq_len)
    r = kc.shape[1]
    n_sel = seq_len // SEL_BLOCK
    n_cmp = seq_len // CMP_STRIDE - CMP_LEN // CMP_STRIDE + 1
    psel = jnp.asarray(_np_psel(), bf16)
    mt = jnp.asarray(_np_cmp_to_sel_t(n_cmp, n_sel, n_sel, r), bf16)
    nc = NSA_HEADS * qb
    per_seq = lambda s: pl.BlockSpec((None,) + s, lambda b, i: (b, 0, 0))
    return pl.pallas_call(
        functools.partial(_nsa_prompt_kernel, qb=qb, seq_len=seq_len, tk=tk, tkp=tkp, wk=wk),
        grid=(b_sz, seq_len // qb),
        in_specs=[pl.BlockSpec((None, qb, NSA_WIDTH), lambda b, i: (b, i, 0)),
                  pl.BlockSpec((None, qb, LANES), lambda b, i: (b, i, 0)),
                  per_seq((r, FEAT)), per_seq((FEAT, r)),
                  per_seq((seq_len, FEAT)), per_seq((FEAT, seq_len)),
                  per_seq((seq_len, FEAT)), per_seq((FEAT, seq_len)),
                  pl.BlockSpec((NSA_HEADS, FEAT, NSA_WIDTH), lambda b, i: (0, 0, 0)),
                  pl.BlockSpec((n_sel, r), lambda b, i: (0, 0))],
        out_specs=pl.BlockSpec((None, qb, NSA_WIDTH), lambda b, i: (b, i, 0)),
        out_shape=SDS((b_sz, seq_len, NSA_WIDTH), bf16),
        scratch_shapes=[pltpu.VMEM((n_sel, nc), f32), pltpu.VMEM((1, nc), f32), pltpu.VMEM((1, nc), f32),
                        pltpu.VMEM((FEAT, nc), f32)],
        compiler_params=_cp("arbitrary", "arbitrary"), name="nsa_prompt",
    )(nq3, ng3, kc, vct, ksk, ksvt, kwk, kwvt, psel, mt)


NCS = LANES


def _page_halves(page_ref):
    k = jnp.concatenate([page_ref[0, h] for h in range(NSA_KV_HEADS)], axis=0)
    v = jnp.concatenate([page_ref[1, h] for h in range(NSA_KV_HEADS)], axis=0)
    return k, v


def _nsa_s1_kernel(pt_ref, *refs, n_pages, pps, ts, win_buf, rows):
    pages = refs[:pps]
    (newk_ref, newv_ref, q_ref, gate_ref, swin_ref, kwn_ref, pe_ref, w_ref, psel_ref, mt_ref, gsum_ref,
     qzt_ref, part_ref, bias_ref, xk_ref, xv_ref) = refs[pps:]
    j = pl.program_id(1)
    past = n_pages * PAGE_SIZE
    n_tail = xk_ref.shape[0] - past

    @pl.when(j == 0)
    def _fill_tail():
        for x_ref, new_ref in ((xk_ref, newk_ref), (xv_ref, newv_ref)):
            x_ref[pl.ds(past, n_tail), :] = jnp.zeros((n_tail, FEAT), f32)
            x_ref[pl.ds(past, ts), :] = new_ref[...]

    for k in range(pps):
        r0 = pl.multiple_of((j * pps + k) * PAGE_SIZE, PAGE_SIZE)
        kt, vt = _page_halves(pages[k])
        xk_ref[pl.ds(r0, PAGE_SIZE), :] = kt.T
        xv_ref[pl.ds(r0, PAGE_SIZE), :] = vt.T

    @pl.when(j == pl.num_programs(1) - 1)
    def _finish():
        n_rows = NSA_HEADS * ts
        qz = jnp.concatenate([_dot_nt(q_ref[...], psel_ref[hg]) for hg in range(NSA_HEADS)]
                             + [jnp.zeros((NCS - n_rows, FEAT), f32)], axis=0)
        qzt = (qz.T * Q_SCALE).astype(bf16)
        qzt_ref[...] = qzt
        t_col = past + lax.broadcasted_iota(i32, (1, NCS), 1) % ts
        total = past + ts
        n_sel = -(-total // SEL_BLOCK)
        n_cmp = n_sel * SEL_BLOCK // CMP_STRIDE - CMP_LEN // CMP_STRIDE + 1

        s = _dot(_compress_half(xk_ref, pe_ref, w_ref, 0, rows).astype(bf16), qzt)
        n_io = lax.broadcasted_iota(i32, s.shape, 0)
        valid = (n_io * CMP_STRIDE + (CMP_LEN - 1) <= t_col) & (n_io < n_cmp)
        pc, inv_c = _col_softmax(s, valid)
        o_c = _dot(_compress_half(xv_ref, pe_ref, w_ref, 1, rows).T.astype(bf16), pc) * inv_c
        imp_all = _dot(mt_ref[...], pc) * inv_c
        hi = imp_all.astype(bf16)
        r1 = imp_all - hi.astype(f32)
        mid = r1.astype(bf16)
        lo = (r1 - mid.astype(f32)).astype(bf16)
        gs = gsum_ref[...]
        imp = _dot(hi, gs) + _dot(mid, gs) + _dot(lo, gs)
        bias_ref[...] = _select_bias(imp, t_col, n_sel)

        wkt, wvt = _page_halves(swin_ref)
        kwn = jnp.concatenate([kwn_ref[...], jnp.zeros((ts, NSA_KV_WIDTH), f32)], axis=0).astype(bf16)
        s = jnp.concatenate([_dot_tn(wkt.astype(bf16), qzt), _dot(kwn[:, :FEAT], qzt)], axis=0)
        w_row = lax.broadcasted_iota(i32, s.shape, 0)
        w_pos = total - (win_buf + ts) + w_row
        dpos = t_col - w_pos
        pw, inv_w = _col_softmax(s, (dpos >= 0) & (dpos < WINDOW) & (w_pos >= 0) & (w_row < win_buf + ts))
        o_w = (_dot(wvt.astype(bf16), pw[:win_buf]) + _dot_tn(kwn[:, FEAT:], pw[win_buf:])) * inv_w

        sig = jax.nn.sigmoid(gate_ref[...])
        oc_r, ow_r = o_c.T, o_w.T
        part_ref[...] = jnp.concatenate(
            [sig[:, 3 * hg:3 * hg + 1] * oc_r[hg * ts:(hg + 1) * ts] + sig[:, 3 * hg + 2:3 * hg + 3] * ow_r[hg * ts:(hg + 1) * ts]
             for hg in range(NSA_HEADS)], axis=0)


def _nsa_s2_kernel(pt_ref, *refs, n_pages, pps, ts):
    pages = refs[:pps]
    news_ref, qzt_ref, bias_ref, part_ref, gate_ref, psel_ref, o_ref, m_ref, l_ref, acc_ref = refs[pps:]
    j = pl.program_id(1)
    past = n_pages * PAGE_SIZE
    qzt = qzt_ref[...]
    bpp = PAGE_SIZE // SEL_BLOCK

    @pl.when(j == 0)
    def _init():
        _flash_init(m_ref, l_ref, acc_ref)

    halves = [_page_halves(pages[k]) for k in range(pps)]
    kt = jnp.concatenate([h[0] for h in halves], axis=1).astype(bf16)
    vt = jnp.concatenate([h[1] for h in halves], axis=1).astype(bf16)
    _flash_step(_dot_tn(kt, qzt) + _block_bias(bias_ref, j * pps * bpp, pps * bpp, NCS), vt, m_ref, l_ref, acc_ref)

    @pl.when(j == pl.num_programs(1) - 1)
    def _finish():
        t_col = past + lax.broadcasted_iota(i32, (1, NCS), 1) % ts
        new = jnp.concatenate([news_ref[...], jnp.zeros((PAGE_SIZE - ts, NSA_KV_WIDTH), f32)], axis=0)
        s = _dot(new[:, :FEAT].astype(bf16), qzt) + _block_bias(bias_ref, n_pages * bpp, bpp, NCS)
        pos = past + lax.broadcasted_iota(i32, s.shape, 0)
        _flash_step(jnp.where(pos <= t_col, s, NEG_INF), new[:, FEAT:].T.astype(bf16), m_ref, l_ref, acc_ref)
        o_s = (acc_ref[...] * (1.0 / jnp.maximum(l_ref[...], 1e-30))).T
        sig = jax.nn.sigmoid(gate_ref[...])
        part = part_ref[...]
        out = jnp.zeros((ts, NSA_WIDTH), f32)
        for hg in range(NSA_HEADS):
            rs = slice(hg * ts, (hg + 1) * ts)
            mix = part[rs] + sig[:, 3 * hg + 1:3 * hg + 2] * o_s[rs]
            out = out + _dot(mix.astype(bf16), psel_ref[hg])
        o_ref[...] = out.astype(o_ref.dtype)


def _nsa_sample(nq3, ng3, kc3, vc3, kvs3, kvw3, cache_c_t, cache_s_t, swin_t, page_table, cw):
    bs, ts, _ = nq3.shape
    n_pages = page_table.shape[1]
    win_buf = swin_t.shape[-1]
    assert ts == SUBLANES and PAGE_SIZE % SEL_BLOCK == 0 and NSA_HEADS * ts <= NCS
    pps = math.gcd(n_pages, 16)
    steps = n_pages // pps
    past = n_pages * PAGE_SIZE
    total = past + ts
    n_sel = -(-total // SEL_BLOCK)
    n_cmp = n_sel * SEL_BLOCK // CMP_STRIDE - CMP_LEN // CMP_STRIDE + 1
    rows = -(-(n_cmp + 1) // SUBLANES) * SUBLANES
    x_rows = rows * CMP_STRIDE
    rb = -(-(n_sel + 1) // SUBLANES) * SUBLANES
    psel = jnp.asarray(_np_psel(), bf16)
    mt = jnp.asarray(_np_cmp_to_sel_t(n_cmp, n_sel, rb, rows), bf16)
    col = np.arange(NCS)
    gsum = ((col[:, None] // (NSA_GROUP * ts) == col[None, :] // (NSA_GROUP * ts)) & (col[:, None] % ts == col[None, :] % ts))
    gsum = jnp.asarray(gsum.astype(np.float32), bf16)
    pt_flat = page_table.reshape(-1).astype(i32)
    pe, w = cw
    page_shape = (2, NSA_KV_HEADS, NSA_HD, PAGE_SIZE)

    def page_spec(k):
        return pl.BlockSpec((None,) + page_shape, lambda b, j, pt, k=k: (pt[b * n_pages + j * pps + k], 0, 0, 0, 0))

    per_seq = lambda s: pl.BlockSpec((None,) + s, lambda b, j, pt: (b,) + (0,) * len(s))
    const = lambda s: pl.BlockSpec(s, lambda b, j, pt: (0,) * len(s))

    qzt, part, bias = pl.pallas_call(
        functools.partial(_nsa_s1_kernel, n_pages=n_pages, pps=pps, ts=ts, win_buf=win_buf, rows=rows),
        grid_spec=pltpu.PrefetchScalarGridSpec(
            num_scalar_prefetch=1, grid=(bs, steps),
            in_specs=[page_spec(k) for k in range(pps)]
            + [per_seq((ts, FEAT)), per_seq((ts, FEAT)), per_seq((ts, NSA_WIDTH)), per_seq((ts, LANES)),
               per_seq((2, NSA_KV_HEADS, NSA_HD, win_buf)), per_seq((ts, NSA_KV_WIDTH)), const(pe.shape), const(w.shape),
               const((NSA_HEADS, FEAT, NSA_WIDTH)), const((rb, rows)), const((NCS, NCS))],
            out_specs=[per_seq((FEAT, NCS)), per_seq((NSA_HEADS * ts, FEAT)), per_seq((rb, NCS))],
            scratch_shapes=[pltpu.VMEM((x_rows, FEAT), f32), pltpu.VMEM((x_rows, FEAT), f32)]),
        out_shape=[SDS((bs, FEAT, NCS), bf16), SDS((bs, NSA_HEADS * ts, FEAT), f32), SDS((bs, rb, NCS), f32)],
        compiler_params=_cp("arbitrary", "arbitrary"), name="nsa_sample_cmp",
    )(pt_flat, *([cache_c_t] * pps), kc3, vc3, nq3, ng3, swin_t, kvw3, pe, w, psel, mt, gsum)

    return pl.pallas_call(
        functools.partial(_nsa_s2_kernel, n_pages=n_pages, pps=pps, ts=ts),
        grid_spec=pltpu.PrefetchScalarGridSpec(
            num_scalar_prefetch=1, grid=(bs, steps),
            in_specs=[page_spec(k) for k in range(pps)]
            + [per_seq((ts, NSA_KV_WIDTH)), per_seq((FEAT, NCS)), per_seq((rb, NCS)), per_seq((NSA_HEADS * ts, FEAT)),
               per_seq((ts, LANES)), const((NSA_HEADS, FEAT, NSA_WIDTH))],
            out_specs=per_seq((ts, NSA_WIDTH)),
            scratch_shapes=[pltpu.VMEM((1, NCS), f32), pltpu.VMEM((1, NCS), f32), pltpu.VMEM((FEAT, NCS), f32)]),
        out_shape=SDS((bs, ts, NSA_WIDTH), bf16),
        compiler_params=_cp("arbitrary", "arbitrary"), name="nsa_sample_sel",
    )(pt_flat, *([cache_s_t] * pps), kvs3, qzt, bias, part, ng3, psel)


def _outproj_kernel(gla_ref, nsa_ref, x_ref, gm_ref, sh_ref, sc_ref, gf_ref, wo_ref, wrt_ref, br_ref,
                    x1_ref, h2_ref, idx_ref, tw_ref):
    y = _dot(gla_ref[...], wo_ref[:GLA_WIDTH, :]) + _dot(nsa_ref[...], wo_ref[GLA_WIDTH:, :])
    x1 = x_ref[...] + gm_ref[...] * y
    x1_ref[...] = x1
    h2 = _rms(x1, gf_ref[...]) * (1.0 + sc_ref[...]) + sh_ref[...]
    h2_ref[...] = h2
    s = _dot_nt(wrt_ref[...], h2.astype(bf16)) + br_ref[...]
    e_io = lax.broadcasted_iota(i32, s.shape, 0)
    vals, idxs = [], []
    for _ in range(TOP_K):
        mx = jnp.max(s, axis=0, keepdims=True)
        ix = jnp.min(jnp.where(s == mx, e_io, N_EXPERTS), axis=0, keepdims=True)
        vals.append(mx)
        idxs.append(ix)
        s = jnp.where(e_io == ix, -jnp.inf, s)
    ex = [jnp.exp(v - vals[0]) for v in vals]
    den = ex[0] + ex[1] + ex[2] + ex[3]
    idx_ref[...] = jnp.concatenate(idxs, axis=0)
    tw_ref[...] = jnp.concatenate([e / den for e in ex], axis=0)


def _outproj(gla_o, nsa_o, x2d, mod, g_ffn, w_out_b, w_router_t, b_router):
    t, tm = x2d.shape[0], mod.tm
    row = lambda i: (i, 0)
    const = lambda s: pl.BlockSpec(s, lambda i: (0, 0))
    return pl.pallas_call(
        _outproj_kernel, grid=(t // tm,),
        in_specs=[pl.BlockSpec((tm, GLA_WIDTH), row), pl.BlockSpec((tm, NSA_WIDTH), row), pl.BlockSpec((tm, D_MODEL), row),
                  mod.spec(2), mod.spec(3), mod.spec(4), const((1, D_MODEL)), const((MIX_WIDTH, D_MODEL)),
                  const((N_EXPERTS, D_MODEL)), const((N_EXPERTS, 1))],
        out_specs=[pl.BlockSpec((tm, D_MODEL), row), pl.BlockSpec((tm, D_MODEL), row),
                   pl.BlockSpec((TOP_K, tm), lambda i: (0, i)), pl.BlockSpec((TOP_K, tm), lambda i: (0, i))],
        out_shape=[SDS((t, D_MODEL), f32), SDS((t, D_MODEL), f32), SDS((TOP_K, t), i32), SDS((TOP_K, t), f32)],
        compiler_params=_cp("arbitrary"), name="outproj_router",
    )(gla_o, nsa_o, x2d, mod.arr, mod.arr, mod.arr, g_ffn.reshape(1, D_MODEL), w_out_b, w_router_t,
      b_router.reshape(N_EXPERTS, 1))


def _moe_kernel(be_ref, nv_ref, tokm_hbm, slotm_hbm, h2_hbm, wg_ref, bg_ref, wu_ref, bu_ref, wd_ref, bd_ref, out_hbm,
                tok_sm, slot_sm, xbuf, ybuf, wbuf, tsem, lsem, gsem, ssem):
    i = pl.program_id(0)
    nv = nv_ref[0]
    tm = xbuf.shape[1]
    pad_block = tokm_hbm.shape[0] - 1
    n_slab = 4
    sw, rg = D_EXPERT // n_slab, tm // n_slab

    def tok_copy(b):
        return pltpu.make_async_copy(tokm_hbm.at[b], tok_sm, tsem)

    def slot_copy(b):
        return pltpu.make_async_copy(slotm_hbm.at[b], slot_sm, lsem)

    def gather_row(par, r):
        pltpu.make_async_copy(h2_hbm.at[pl.ds(tok_sm[0, r], 1)], xbuf.at[par, pl.ds(r, 1)],
                              gsem.at[par]).start(priority=r % 2)

    def scatter_row(par, r):
        pltpu.make_async_copy(ybuf.at[par, pl.ds(r, 1)], out_hbm.at[pl.ds(slot_sm[0, r], 1)],
                              ssem.at[par]).start(priority=r % 2)

    def wait_gather(par):
        pltpu.make_async_copy(h2_hbm.at[pl.ds(0, tm)], xbuf.at[par], gsem.at[par]).wait()

    def wait_scatter(par):
        pltpu.make_async_copy(ybuf.at[par], out_hbm.at[pl.ds(0, tm)], ssem.at[par]).wait()

    @pl.when(i == 0)
    def _prologue():
        tok_copy(0).start()
        slot_copy(pad_block).start()
        ybuf[1] = jnp.zeros(ybuf.shape[1:], f32)
        tok_copy(0).wait()
        for r in range(tm):
            gather_row(0, r)
        tok_copy(1).start()

    @pl.when((i < nv) & ((i == 0) | (be_ref[i] != be_ref[jnp.maximum(i - 1, 0)])))
    def _new_expert():
        wbuf[0] = wg_ref[...].astype(bf16)
        wbuf[1] = wu_ref[...].astype(bf16)
        wbuf[2] = wd_ref[...].astype(bf16)

    for par in range(2):
        @pl.when((i < nv) & (i % 2 == par))
        def _run():
            wait_gather(par)
            tok_copy(i + 1).wait()
            slot_copy(jnp.where(i == 0, pad_block, i - 1)).wait()

            @pl.when(i >= 1)
            def _free_ybuf():
                wait_scatter(par)

            x = xbuf[par].astype(bf16)
            y = jnp.zeros((tm, D_MODEL), f32)
            for s in range(n_slab):
                cs = slice(s * sw, (s + 1) * sw)
                gate = jnp.minimum(_dot(x, wbuf[0, :, cs]) + bg_ref[:, cs], SWIGLU_LIMIT)
                up = jnp.clip(_dot(x, wbuf[1, :, cs]) + bu_ref[:, cs], -SWIGLU_LIMIT, SWIGLU_LIMIT)
                act = ((up + 1.0) * gate * jax.nn.sigmoid(SWIGLU_ALPHA * gate)).astype(bf16)
                y = y + _dot(act, wbuf[2, cs, :])
                for r in range(s * rg, (s + 1) * rg):
                    gather_row(1 - par, r)
                    scatter_row(1 - par, r)
            ybuf[par] = y + bd_ref[...]

            @pl.when(i + 2 <= nv)
            def _next_tokens():
                tok_copy(i + 2).start()

            slot_copy(i).start()

    @pl.when(i == nv)
    def _last_scatter():
        slot_copy(nv - 1).wait()
        for par in range(2):
            @pl.when(nv % 2 == par)
            def _():
                wait_gather(par)
                wait_scatter(par)
                for r in range(tm):
                    scatter_row(1 - par, r)
                wait_scatter(1 - par)


def _moe_experts(h2_all, tok_meta, slot_meta, block_e, n_valid, n_slots, wg, bg, wu, bu, wd, bd):
    n_blocks, _, tm = tok_meta.shape
    assert D_MODEL == D_EXPERT
    any_spec = pl.BlockSpec(memory_space=pl.ANY)
    wspec = lambda a, b: pl.BlockSpec((None, a, b), lambda i, be, nv: (be[i], 0, 0))
    return pl.pallas_call(
        _moe_kernel,
        grid_spec=pltpu.PrefetchScalarGridSpec(
            num_scalar_prefetch=2, grid=(n_blocks,),
            in_specs=[any_spec, any_spec, any_spec,
                      wspec(D_MODEL, D_EXPERT), wspec(1, D_EXPERT), wspec(D_MODEL, D_EXPERT), wspec(1, D_EXPERT),
                      wspec(D_EXPERT, D_MODEL), wspec(1, D_MODEL)],
            out_specs=any_spec,
            scratch_shapes=[pltpu.SMEM((1, tm), i32), pltpu.SMEM((1, tm), i32), pltpu.VMEM((2, tm, D_MODEL), f32),
                            pltpu.VMEM((2, tm, D_MODEL), f32), pltpu.VMEM((3, D_MODEL, D_EXPERT), bf16),
                            pltpu.SemaphoreType.DMA, pltpu.SemaphoreType.DMA,
                            pltpu.SemaphoreType.DMA((2,)), pltpu.SemaphoreType.DMA((2,))]),
        out_shape=SDS((n_slots, D_MODEL), f32),
        compiler_params=pltpu.CompilerParams(dimension_semantics=("arbitrary",), vmem_limit_bytes=MOE_VMEM_LIMIT_BYTES),
        name="moe_experts",
    )(block_e, n_valid, tok_meta, slot_meta, h2_all, wg, bg.reshape(N_EXPERTS, 1, D_EXPERT), wu,
      bu.reshape(N_EXPERTS, 1, D_EXPERT), wd, bd.reshape(N_EXPERTS, 1, D_MODEL))


def _moe_plan(idx_t, tm, t_pad):
    n_tok = idx_t.shape[1]
    n_assign = n_tok * TOP_K
    flat_e = idx_t.reshape(n_assign)
    order = jnp.argsort(flat_e).astype(i32)
    counts = jnp.sum((flat_e[:, None] == jnp.arange(N_EXPERTS, dtype=i32)[None, :]).astype(i32), axis=0)
    padded = (counts + tm - 1) // tm * tm
    start = jnp.cumsum(counts) - counts
    pad_end = jnp.cumsum(padded)
    pad_start = pad_end - padded
    n_blocks = (n_assign + N_EXPERTS * (tm - 1) + tm - 1) // tm + 1
    blk0 = jnp.arange(n_blocks, dtype=i32) * tm
    block_e = jnp.minimum(jnp.sum((pad_end[None, :] <= blk0[:, None]).astype(i32), axis=1), N_EXPERTS - 1).astype(i32)
    n_valid = (pad_end[-1] // tm).astype(i32).reshape(1)
    r_in = jnp.arange(tm, dtype=i32)[None, :]
    j = blk0[:, None] + r_in - pad_start[block_e][:, None]
    valid = (j < counts[block_e][:, None]) & (blk0[:, None] < pad_end[-1])
    a = order[jnp.clip(start[block_e][:, None] + j, 0, n_assign - 1)]
    tok = a % n_tok
    slot = (a // n_tok) * t_pad + tok
    tok_meta = jnp.where(valid, tok, 0).astype(i32).reshape(n_blocks, 1, tm)
    slot_meta = jnp.where(valid, slot, TOP_K * t_pad + r_in).astype(i32).reshape(n_blocks, 1, tm)
    return tok_meta, slot_meta, block_e, n_valid


def _final_kernel(x1_ref, *refs):
    y_refs, (tw_ref, gate_ref, g_ref, o_ref) = refs[:TOP_K], refs[TOP_K:]
    tw = tw_ref[...]
    f = tw[:, 0:1] * y_refs[0][...]
    for k in range(1, TOP_K):
        f = f + tw[:, k:k + 1] * y_refs[k][...]
    o_ref[...] = _rms(x1_ref[...] + gate_ref[...] * f, g_ref[...])


def _final(x1, y4, tw, mod, g_final, row0, t_pad):
    t, tm = x1.shape[0], mod.tm
    assert row0 % tm == 0 and t_pad % tm == 0
    b0 = row0 // tm
    row = lambda i: (i, 0)
    y_specs = [pl.BlockSpec((tm, D_MODEL), lambda i, k=k: (k * (t_pad // tm) + b0 + i, 0)) for k in range(TOP_K)]
    return pl.pallas_call(
        _final_kernel, grid=(t // tm,),
        in_specs=[pl.BlockSpec((tm, D_MODEL), row)] + y_specs
        + [pl.BlockSpec((tm, TOP_K), lambda i: (i + b0, 0)), mod.spec(5), pl.BlockSpec((1, D_MODEL), lambda i: (0, 0))],
        out_specs=pl.BlockSpec((tm, D_MODEL), row), out_shape=SDS((t, D_MODEL), f32),
        compiler_params=_cp("arbitrary"), name="combine_final_norm",
    )(x1, *([y4] * TOP_K), tw, mod.arr, g_final.reshape(1, D_MODEL))


def _prep_w_in(w_in):
    gq, gk, gv, gr, ga, nq, nkc, nks, nkw, ng = jnp.split(w_in, np.cumsum(IN_SPLITS)[:-1].tolist(), axis=1)
    pad = lambda a: jnp.pad(a, ((0, 0), (0, LANES - a.shape[1])))
    w_p = jnp.concatenate([gq, gk, gv, gr, nq, nkc, nks, nkw, pad(ga), pad(ng)], axis=1).astype(bf16)
    w_kv_t = jnp.concatenate([nkc, nks, nkw], axis=1).T.astype(bf16)
    return w_p, w_kv_t


def _feature_major(a, rows_axis):
    return jnp.moveaxis(a, rows_axis, -1)


def kernel(x_prompt, x_sample, c_prompt, c_sample, cache_cmp, cache_sel, state_win, state_gla, page_table, w_ada, b_ada, g_mix, g_ffn, w_in, w_a2, b_a, g_gla, phi_pe, phi_w, w_out, w_router, b_router, w_gate, b_gate, w_up, b_up, w_down, b_down, g_final):
    bp, sp = x_prompt.shape[:2]
    bs, ts = x_sample.shape[:2]
    tp, tsn = bp * sp, bs * ts
    win_buf = state_win.shape[2]
    l = 0

    mod = _adaln(jnp.concatenate([c_prompt, c_sample], axis=0), w_ada[l], b_ada[l])
    tm_p = math.gcd(sp, 512)
    tm_s = math.gcd(tsn, 512)
    mod_p = _Mod(mod[:bp].reshape(bp, 6, 1, D_MODEL), False, tm_p, sp // tm_p)
    mod_s = _Mod(jnp.repeat(mod[bp:].reshape(bs, 6, D_MODEL).transpose(1, 0, 2), ts, axis=1), True, tm_s, None)

    w_in_p, w_kv_t = _prep_w_in(w_in[l])
    cw = _prep_compress(phi_pe[l], phi_w[l])
    w_out_b = w_out[l].astype(bf16)
    w_router_t = w_router[l].T.astype(bf16)
    xp2, xs2 = x_prompt.reshape(tp, D_MODEL), x_sample.reshape(tsn, D_MODEL)

    (gq, gk, gv, gr, nq, kc, vc, ga, ng, kvc_t, kvs_t, kvw_t, ksk, kwk, ksvt, kwvt) = _inproj(
        xp2, mod_p, g_mix[l], w_in_p, w_kv_t, sp)
    gla_o, st_p = _gla(gq, gk, gv, gr, ga, jnp.zeros((bp, GLA_HEADS, GLA_DV, GLA_DK), f32), w_a2[l], b_a[l], g_gla[l], bp, sp)
    kcm, vct = _cmp_prompt(kc.reshape(bp, sp, FEAT), vc.reshape(bp, sp, FEAT), cw)
    nsa_o = _nsa_prompt(nq.reshape(bp, sp, NSA_WIDTH), ng.reshape(bp, sp, LANES), kcm, vct,
                        ksk.reshape(bp, sp, FEAT), ksvt, kwk.reshape(bp, sp, FEAT), kwvt)
    x1_p, h2_p, idx_p, tw_p = _outproj(gla_o.reshape(tp, GLA_WIDTH), nsa_o.reshape(tp, NSA_WIDTH), xp2, mod_p,
                                       g_ffn[l], w_out_b, w_router_t, b_router[l])
    token_major = lambda a: jnp.moveaxis(a, -1, 1)[None]
    new_cmp_p = token_major(kvc_t)
    new_sel_p = token_major(kvs_t)
    new_win_p = token_major(jnp.pad(kvw_t, ((0, 0),) * 4 + ((win_buf, 0),))[..., -win_buf:])
    new_gla_p = jnp.swapaxes(st_p, 2, 3)[None]

    gq, gk, gv, gr, nq, kc, vc, ks, vs, kw, vw, ga, ng = _inproj(xs2, mod_s, g_mix[l], w_in_p)
    gla_os, st_s = _gla(gq, gk, gv, gr, ga, jnp.swapaxes(state_gla[l], 2, 3), w_a2[l], b_a[l], g_gla[l], bs, ts)
    r3 = lambda a: a.reshape(bs, ts, a.shape[-1])
    kvc3, kvs3, kvw3 = (jnp.concatenate([r3(k_), r3(v_)], axis=-1) for k_, v_ in ((kc, vc), (ks, vs), (kw, vw)))
    swin_t = _feature_major(state_win[l], 1)
    nsa_os = _nsa_sample(r3(nq), r3(ng), r3(kc), r3(vc), kvs3, kvw3,
                         _feature_major(cache_cmp[l], 1), _feature_major(cache_sel[l], 1), swin_t, page_table, cw)
    x1_s, h2_s, idx_s, tw_s = _outproj(gla_os.reshape(tsn, GLA_WIDTH), nsa_os.reshape(tsn, NSA_WIDTH), xs2, mod_s,
                                       g_ffn[l], w_out_b, w_router_t, b_router[l])
    kv_row = (2, NSA_KV_HEADS, NSA_HD)
    new_cmp_s = kvc3.reshape((1, bs, ts) + kv_row)
    new_sel_s = kvs3.reshape((1, bs, ts) + kv_row)
    kw_t = _feature_major(kvw3.reshape((bs, ts) + kv_row), 1)
    new_win_s = token_major(jnp.concatenate([swin_t, kw_t], axis=-1)[..., -win_buf:])
    new_gla_s = jnp.swapaxes(st_s, 2, 3)[None]

    n_tok = tp + tsn
    t_pad = -(-n_tok // tm_p) * tm_p
    tok_meta, slot_meta, block_e, n_valid = _moe_plan(jnp.concatenate([idx_p, idx_s], axis=1), MOE_TM, t_pad)
    y4 = _moe_experts(jnp.concatenate([h2_p, h2_s], axis=0), tok_meta, slot_meta, block_e, n_valid,
                      TOP_K * t_pad + MOE_TM, w_gate[l], b_gate[l], w_up[l], b_up[l], w_down[l], b_down[l])
    tw_all = jnp.concatenate([tw_p, tw_s], axis=1).T
    y_p = _final(x1_p, y4, tw_all, mod_p, g_final, 0, t_pad).reshape(bp, sp, D_MODEL)
    y_s = _final(x1_s, y4, tw_all, mod_s, g_final, tp, t_pad).reshape(bs, ts, D_MODEL)
    return (y_p, y_s, new_cmp_p, new_sel_p, new_win_p, new_gla_p, new_cmp_s, new_sel_s, new_win_s, new_gla_s)
```

```python
import functools
import math

import numpy as np
import jax
import jax.numpy as jnp
from jax import lax
from jax.experimental import pallas as pl
from jax.experimental.pallas import tpu as pltpu

f32, bf16, i32 = jnp.float32, jnp.bfloat16, jnp.int32
SDS = jax.ShapeDtypeStruct

D_MODEL = 1024
GLA_HEADS, GLA_DK, GLA_DV, GLA_GATE_RANK, GLA_TAU, GLA_CHUNK = 4, 64, 128, 16, 16.0, 64
NSA_HEADS, NSA_KV_HEADS, NSA_HD = 8, 2, 64
NSA_GROUP = NSA_HEADS // NSA_KV_HEADS
CMP_LEN, CMP_STRIDE, SEL_BLOCK, N_SEL, WINDOW, NSA_QBLOCK = 32, 16, 64, 8, 512, 128
N_EXPERTS, TOP_K, D_EXPERT = 32, 4, 1024
SWIGLU_LIMIT, SWIGLU_ALPHA = 7.0, 1.702
RMS_EPS, NEG_INF, FORCE_SCORE = 1e-6, -1e30, 1e4
PAGE_SIZE = 128

GLA_QK_WIDTH = GLA_HEADS * GLA_DK
GLA_WIDTH = GLA_HEADS * GLA_DV
NSA_WIDTH = NSA_HEADS * NSA_HD
NSA_KV_WIDTH = 2 * NSA_KV_HEADS * NSA_HD
IN_SPLITS = (GLA_QK_WIDTH, GLA_QK_WIDTH, GLA_WIDTH, GLA_WIDTH, GLA_GATE_RANK,
             NSA_WIDTH, NSA_KV_WIDTH, NSA_KV_WIDTH, NSA_KV_WIDTH, 3 * NSA_HEADS)
MIX_WIDTH = GLA_WIDTH + NSA_WIDTH

LANES = 128
SUBLANES = 8
VMEM_LIMIT_BYTES = 48 * 1024 * 1024
MOE_VMEM_LIMIT_BYTES = 56 * 1024 * 1024
D_CHUNKS = D_MODEL // LANES

FEAT = NSA_KV_HEADS * NSA_HD
_IN_COLS = (("gq", 0, 256, f32), ("gk", 256, 256, f32), ("gv", 512, 512, bf16), ("gr", 1024, 512, f32),
            ("nq", 1536, 512, bf16), ("kc", 2048, FEAT, f32), ("vc", 2176, FEAT, f32), ("ks", 2304, FEAT, f32),
            ("vs", 2432, FEAT, f32), ("kw", 2560, FEAT, f32), ("vw", 2688, FEAT, f32),
            ("ga", 2816, LANES, f32), ("ng", 2944, LANES, f32))
_IN_WIDTH = 3072
_KV0 = 2048
MOE_TM = 512
Q_SCALE = NSA_HD ** -0.5 * math.log2(math.e)


def _cp(*sem):
    return pltpu.CompilerParams(dimension_semantics=sem, vmem_limit_bytes=VMEM_LIMIT_BYTES)


def _dot(a, b):
    return jnp.dot(a, b, preferred_element_type=f32)


def _dot_nt(a, b):
    return lax.dot_general(a, b, (((1,), (1,)), ((), ())), preferred_element_type=f32)


def _dot_tn(a, b):
    return lax.dot_general(a, b, (((0,), (0,)), ((), ())), preferred_element_type=f32)


def _rms(x, g):
    return x * lax.rsqrt(jnp.mean(x * x, axis=-1, keepdims=True) + RMS_EPS) * g


def _silu(x):
    return x * jax.nn.sigmoid(x)


def _adaln_kernel(c_ref, w_ref, b_ref, o_ref):
    s = _silu(c_ref[...]).astype(bf16)
    o_ref[...] = _dot(s, w_ref[...].astype(bf16)) + b_ref[...]


def _adaln(c_all, w_ada, b_ada):
    nb, n = c_all.shape[0], w_ada.shape[1]
    tn = 1024
    return pl.pallas_call(
        _adaln_kernel, grid=(n // tn,),
        in_specs=[pl.BlockSpec((nb, D_MODEL), lambda j: (0, 0)),
                  pl.BlockSpec((D_MODEL, tn), lambda j: (0, j)),
                  pl.BlockSpec((1, tn), lambda j: (0, j))],
        out_specs=pl.BlockSpec((nb, tn), lambda j: (0, j)),
        out_shape=SDS((nb, n), f32), compiler_params=_cp("arbitrary"), name="adaln",
    )(c_all, w_ada, b_ada.reshape(1, n))


class _Mod:
    def __init__(self, arr, per_token, tm, tiles_per_seq):
        self.arr, self.per_token, self.tm, self.tps = arr, per_token, tm, tiles_per_seq

    def spec(self, k):
        if self.per_token:
            return pl.BlockSpec((None, self.tm, D_MODEL), lambda i: (k, i, 0))
        tps = self.tps
        return pl.BlockSpec((None, None, 1, D_MODEL), lambda i: (i // tps, k, 0, 0))


def _in_cols(feature_major):
    return tuple(c for c in _IN_COLS if not (feature_major and c[0] in ("ks", "vs", "kw", "vw")))


def _inproj_kernel(x_ref, sh_ref, sc_ref, g_ref, w_ref, *rest, feature_major):
    y = _rms(x_ref[...], g_ref[...])
    h = (y * (1.0 + sc_ref[...]) + sh_ref[...]).astype(bf16)
    outs = rest[1:] if feature_major else rest
    cols = _in_cols(feature_major)
    for o_ref, (_, c0, w, _) in zip(outs, cols):
        o_ref[...] = _dot(h, w_ref[:, c0:c0 + w]).astype(o_ref.dtype)
    if feature_major:
        wt_ref = rest[0]
        kvt_refs, (ksk_ref, kwk_ref, ksvt_ref, kwvt_ref) = outs[len(cols):-4], outs[-4:]
        kvt = _dot_nt(wt_ref[...], h)
        for a, kvt_ref in enumerate(kvt_refs):
            for ch in range(2 * NSA_KV_HEADS):
                r0 = a * NSA_KV_WIDTH + ch * NSA_HD
                kvt_ref[ch // NSA_KV_HEADS, ch % NSA_KV_HEADS] = kvt[r0:r0 + NSA_HD]
        ksvt_ref[...] = kvt[NSA_KV_WIDTH + FEAT:2 * NSA_KV_WIDTH].astype(bf16)
        kwvt_ref[...] = kvt[2 * NSA_KV_WIDTH + FEAT:].astype(bf16)
        ksk_ref[...] = _dot(h, w_ref[:, _KV0 + NSA_KV_WIDTH:_KV0 + NSA_KV_WIDTH + FEAT]).astype(bf16)
        kwk_ref[...] = _dot(h, w_ref[:, _KV0 + 2 * NSA_KV_WIDTH:_KV0 + 2 * NSA_KV_WIDTH + FEAT]).astype(bf16)


def _inproj(x2d, mod, g_mix, w_in_p, w_kv_t=None, seq_len=None):
    t, tm = x2d.shape[0], mod.tm
    row = lambda i: (i, 0)
    in_specs = [pl.BlockSpec((tm, D_MODEL), row), mod.spec(0), mod.spec(1),
                pl.BlockSpec((1, D_MODEL), lambda i: (0, 0)),
                pl.BlockSpec((D_MODEL, _IN_WIDTH), lambda i: (0, 0))]
    cols = _in_cols(w_kv_t is not None)
    out_specs = [pl.BlockSpec((tm, w), row) for (_, _, w, _) in cols]
    out_shape = [SDS((t, w), dt) for (_, _, w, dt) in cols]
    args = [x2d, mod.arr, mod.arr, g_mix.reshape(1, D_MODEL), w_in_p]
    if w_kv_t is not None:
        b_sz, tps = t // seq_len, seq_len // tm
        in_specs.append(pl.BlockSpec((3 * NSA_KV_WIDTH, D_MODEL), lambda i: (0, 0)))
        args.append(w_kv_t)
        fm = lambda w: pl.BlockSpec((None, w, tm), lambda i: (i // tps, 0, i % tps))
        out_specs += [pl.BlockSpec((None, 2, NSA_KV_HEADS, NSA_HD, tm), lambda i: (i // tps, 0, 0, 0, i % tps))] * 3
        out_specs += [pl.BlockSpec((tm, FEAT), row), pl.BlockSpec((tm, FEAT), row), fm(FEAT), fm(FEAT)]
        out_shape += [SDS((b_sz, 2, NSA_KV_HEADS, NSA_HD, seq_len), f32)] * 3
        out_shape += [SDS((t, FEAT), bf16), SDS((t, FEAT), bf16),
                      SDS((b_sz, FEAT, seq_len), bf16), SDS((b_sz, FEAT, seq_len), bf16)]
    return pl.pallas_call(
        functools.partial(_inproj_kernel, feature_major=w_kv_t is not None), grid=(t // tm,),
        in_specs=in_specs, out_specs=out_specs, out_shape=out_shape,
        compiler_params=_cp("arbitrary"), name="inproj",
    )(*args)


def _gla_kernel(q_ref, k_ref, v_ref, r_ref, a_ref, s0_ref, wa2_ref, ba_ref, gg_ref, o_ref, st_ref, *, chunk, n_chunks):
    @pl.when(pl.program_id(1) == 0)
    def _init():
        st_ref[...] = s0_ref[...]

    ri = lax.broadcasted_iota(i32, (chunk, chunk), 0)
    ci = lax.broadcasted_iota(i32, (chunk, chunk), 1)
    causal = ri >= ci
    tril = jnp.where(causal, 1.0, 0.0).astype(bf16)

    def body(c, carry):
        rows = pl.ds(pl.multiple_of(c * chunk, chunk), chunk)
        for sq in range(q_ref.shape[0]):
            a_low = a_ref[sq, rows, :][:, :GLA_GATE_RANK].astype(bf16)
            z = _dot(a_low, wa2_ref[...]) + ba_ref[...]
            log_a = (jnp.minimum(z, 0.0) - jnp.log1p(jnp.exp(-jnp.abs(z)))) * (1.0 / GLA_TAU)
            hi = log_a.astype(bf16)
            lo = (log_a - hi.astype(f32)).astype(bf16)
            cum = _dot(tril, hi) + _dot(tril, lo)
            last = cum[chunk - 1:chunk, :]
            q = q_ref[sq, rows, :] * (GLA_DK ** -0.5)
            k = k_ref[sq, rows, :]
            qd = (q * jnp.exp(cum)).astype(bf16)
            ki = (k * jnp.exp(-cum)).astype(bf16)
            ke = (k * jnp.exp(last - cum)).astype(bf16)
            dec = jnp.exp(last)
            v = v_ref[sq, rows, :]
            r = r_ref[sq, rows, :]
            for h in range(GLA_HEADS):
                sk = slice(h * GLA_DK, (h + 1) * GLA_DK)
                sv = slice(h * GLA_DV, (h + 1) * GLA_DV)
                qh, kih, keh, vh = qd[:, sk], ki[:, sk], ke[:, sk], v[:, sv]
                att = jnp.where(causal, _dot_nt(qh, kih), 0.0)
                st = st_ref[sq, h]
                o = _dot(att.astype(bf16), vh) + _dot_nt(qh, st.astype(bf16))
                st_ref[sq, h] = st * dec[:, sk] + _dot_tn(vh, keh)
                o_ref[sq, rows, sv] = (_rms(o, gg_ref[...]) * _silu(r[:, sv])).astype(o_ref.dtype)
        return carry

    lax.fori_loop(0, n_chunks, body, 0)


def _gla(gq, gk, gv, gr, ga, s0t, w_a2, b_a, g_gla, b_sz, seq_len):
    chunk = math.gcd(seq_len, GLA_CHUNK)
    tb = min(seq_len, 512)
    n_chunks = tb // chunk
    nb = math.gcd(b_sz, 4 if tb >= 512 else 8)
    r3 = lambda a: a.reshape(b_sz, seq_len, a.shape[-1])
    tok = lambda w: pl.BlockSpec((nb, tb, w), lambda b, j: (b, j, 0))
    st_spec = pl.BlockSpec((nb, GLA_HEADS, GLA_DV, GLA_DK), lambda b, j: (b, 0, 0, 0))
    const = lambda s: pl.BlockSpec(s, lambda b, j: (0, 0))
    return pl.pallas_call(
        functools.partial(_gla_kernel, chunk=chunk, n_chunks=n_chunks),
        grid=(b_sz // nb, seq_len // tb),
        in_specs=[tok(GLA_QK_WIDTH), tok(GLA_QK_WIDTH), tok(GLA_WIDTH), tok(GLA_WIDTH), tok(LANES), st_spec,
                  const((GLA_GATE_RANK, GLA_QK_WIDTH)), const((1, GLA_QK_WIDTH)), const((1, GLA_DV))],
        out_specs=[tok(GLA_WIDTH), st_spec],
        out_shape=[SDS((b_sz, seq_len, GLA_WIDTH), bf16), SDS((b_sz, GLA_HEADS, GLA_DV, GLA_DK), f32)],
        compiler_params=_cp("arbitrary", "arbitrary"), name="gla",
    )(r3(gq), r3(gk), r3(gv), r3(gr), r3(ga), s0t, w_a2.astype(bf16), b_a.reshape(1, -1), g_gla.reshape(1, -1))


def _np_psel():
    p = np.zeros((NSA_HEADS, FEAT, NSA_WIDTH), np.float32)
    for h in range(NSA_KV_HEADS):
        for g in range(NSA_GROUP):
            for d in range(NSA_HD):
                p[h * NSA_GROUP + g, h * NSA_HD + d, (h * NSA_GROUP + g) * NSA_HD + d] = 1.0
    return p


def _np_cmp_to_sel_t(n_cmp, n_sel, rows, cols):
    i0 = np.arange(n_cmp)[None, :] * CMP_STRIDE
    j0 = np.arange(n_sel)[:, None] * SEL_BLOCK
    m = np.zeros((rows, cols), np.float32)
    m[:n_sel, :n_cmp] = ((i0 < j0 + SEL_BLOCK) & (i0 + CMP_LEN > j0)).astype(np.float32)
    return m


def _prep_compress(phi_pe, phi_w):
    w = phi_w.reshape(2, CMP_LEN, NSA_HD, NSA_HD)
    wb = jnp.einsum('cldo,hk->clhdko', w, jnp.eye(NSA_KV_HEADS, dtype=f32)).reshape(2, CMP_LEN, FEAT, FEAT)
    pe = jnp.broadcast_to(phi_pe[:, :, None, :], (2, CMP_LEN, NSA_KV_HEADS, NSA_HD)).reshape(2, CMP_LEN, FEAT)
    pe_w = jnp.einsum('clf,clfo->clo', pe, wb, precision=lax.Precision.HIGHEST)
    bias = jnp.concatenate([pe_w[:, :CMP_STRIDE].sum(axis=1), pe_w[:, CMP_STRIDE:].sum(axis=1)], axis=-1)
    pairs = wb.reshape(2, 2, CMP_STRIDE // 2, 2 * FEAT, FEAT)
    w_pairs = jnp.concatenate([pairs[:, 0], pairs[:, 1]], axis=-1)
    return bias.reshape(2, 1, 2 * FEAT), w_pairs.astype(bf16)


def _compress_half(x_ref, bias_ref, w_ref, c, rows):
    acc = jnp.zeros((rows, 2 * FEAT), f32)
    for p in range(CMP_STRIDE // 2):
        xa = x_ref[pl.ds(2 * p, rows, stride=CMP_STRIDE), :]
        xb = x_ref[pl.ds(2 * p + 1, rows, stride=CMP_STRIDE), :]
        acc = acc + _dot(jnp.concatenate([xa, xb], axis=1).astype(bf16), w_ref[c, p])
    acc = acc + bias_ref[c]
    return acc[:, :FEAT] + pltpu.roll(acc[:, FEAT:], rows - 1, 0)


def _col_softmax(s, valid, every_column_valid=False):
    s = jnp.where(valid, s, NEG_INF)
    m = jnp.max(s, axis=0, keepdims=True)
    p = jnp.exp2(s - m)
    if not every_column_valid:
        p = jnp.where(valid, p, 0.0)
    l = jnp.sum(p, axis=0, keepdims=True)
    return p.astype(bf16), 1.0 / jnp.maximum(l, 1e-30)


def _select_bias(imp, t_col, n_sel):
    blk = lax.broadcasted_iota(i32, imp.shape, 0)
    cur = lax.shift_right_logical(t_col, int(math.log2(SEL_BLOCK)))
    forced = (blk == 0) | (blk == cur) | (blk == cur - 1)
    future = blk * SEL_BLOCK > t_col
    score = jnp.where(future, NEG_INF, jnp.where(forced, FORCE_SCORE, imp))
    score = jnp.where(blk < n_sel, score, -jnp.inf)
    bias = jnp.full(imp.shape, NEG_INF, f32)
    for _ in range(min(N_SEL, n_sel)):
        mx = jnp.max(score, axis=0, keepdims=True)
        idx = jnp.min(jnp.where(score == mx, blk, 2 ** 30), axis=0, keepdims=True)
        pick = blk == idx
        bias = jnp.where(pick, 0.0, bias)
        score = jnp.where(pick, -jnp.inf, score)
    return bias


def _block_bias(bias_ref, first_block, n_blocks, cols):
    return jnp.concatenate(
        [jnp.broadcast_to(bias_ref[pl.ds(first_block + b, 1), :], (SEL_BLOCK, cols)) for b in range(n_blocks)], axis=0)


def _flash_step(s, vt, m_ref, l_ref, acc_ref):
    m_old = m_ref[...]
    m_new = jnp.maximum(m_old, jnp.max(s, axis=0, keepdims=True))
    alpha = jnp.exp2(m_old - m_new)
    p = jnp.exp2(s - m_new)
    l_ref[...] = alpha * l_ref[...] + jnp.sum(p, axis=0, keepdims=True)
    acc_ref[...] = alpha * acc_ref[...] + _dot(vt, p.astype(bf16))
    m_ref[...] = m_new


def _flash_init(m_ref, l_ref, acc_ref):
    m_ref[...] = jnp.full(m_ref.shape, NEG_INF, f32)
    l_ref[...] = jnp.zeros(l_ref.shape, f32)
    acc_ref[...] = jnp.zeros(acc_ref.shape, f32)


def _cmp_prompt_kernel(k_ref, v_ref, pe_ref, w_ref, kc_ref, vct_ref):
    rows = kc_ref.shape[0]
    kc_ref[...] = _compress_half(k_ref, pe_ref, w_ref, 0, rows).astype(bf16)
    vct_ref[...] = _compress_half(v_ref, pe_ref, w_ref, 1, rows).T.astype(bf16)


def _cmp_prompt(kc3, vc3, cw):
    b_sz, seq_len, _ = kc3.shape
    r = seq_len // CMP_STRIDE
    pe, w = cw
    return pl.pallas_call(
        _cmp_prompt_kernel, grid=(b_sz,),
        in_specs=[pl.BlockSpec((None, seq_len, FEAT), lambda b: (b, 0, 0)), pl.BlockSpec((None, seq_len, FEAT), lambda b: (b, 0, 0)),
                  pl.BlockSpec(pe.shape, lambda b: (0,) * pe.ndim), pl.BlockSpec(w.shape, lambda b: (0,) * w.ndim)],
        out_specs=[pl.BlockSpec((None, r, FEAT), lambda b: (b, 0, 0)), pl.BlockSpec((None, FEAT, r), lambda b: (b, 0, 0))],
        out_shape=[SDS((b_sz, r, FEAT), bf16), SDS((b_sz, FEAT, r), bf16)],
        compiler_params=_cp("arbitrary"), name="cmp_prompt",
    )(kc3, vc3, pe, w)


def _nsa_prompt_kernel(q_ref, gate_ref, kc_ref, vct_ref, ksk_ref, ksvt_ref, kwk_ref, kwvt_ref, psel_ref, mt_ref,
                       o_ref, bias_ref, m_ref, l_ref, acc_ref, *, qb, seq_len, tk, tkp, wk):
    i = pl.program_id(1)
    start = i * qb
    nc = NSA_HEADS * qb
    n_sel = seq_len // SEL_BLOCK
    n_cmp = seq_len // CMP_STRIDE - CMP_LEN // CMP_STRIDE + 1
    q = q_ref[...]
    qzt = jnp.concatenate([_dot_nt(psel_ref[hg], q) for hg in range(NSA_HEADS)], axis=1)
    qzt = (qzt * Q_SCALE).astype(bf16)
    t_col = start + lax.broadcasted_iota(i32, (1, nc), 1) % qb

    s = _dot(kc_ref[...], qzt)
    n_io = lax.broadcasted_iota(i32, s.shape, 0)
    valid = (n_io * CMP_STRIDE + (CMP_LEN - 1) <= t_col) & (n_io < n_cmp)
    pc, inv_c = _col_softmax(s, valid)
    o_c = _dot(vct_ref[...], pc) * inv_c
    imp_all = _dot(mt_ref[...], pc) * inv_c
    imp = []
    for h in range(NSA_KV_HEADS):
        tot = imp_all[:, h * NSA_GROUP * qb:(h * NSA_GROUP + 1) * qb]
        for g in range(1, NSA_GROUP):
            tot = tot + imp_all[:, (h * NSA_GROUP + g) * qb:(h * NSA_GROUP + g + 1) * qb]
        imp.append(tot)
    bias = _select_bias(jnp.concatenate(imp, axis=1), t_col[:, :NSA_KV_HEADS * qb], n_sel)
    bias_ref[...] = jnp.concatenate([bias[:, h * qb:(h + 1) * qb] for h in range(NSA_KV_HEADS) for _ in range(NSA_GROUP)], axis=1)

    _flash_init(m_ref, l_ref, acc_ref)

    def scores(jt, width):
        k0 = pl.multiple_of(jt * width, width)
        bias = _block_bias(bias_ref, jt * (width // SEL_BLOCK), width // SEL_BLOCK, nc)
        return k0, _dot(ksk_ref[pl.ds(k0, width), :], qzt) + bias

    def past_tile(jt, carry):
        k0, s = scores(jt, tkp)
        _flash_step(s, ksvt_ref[:, pl.ds(k0, tkp)], m_ref, l_ref, acc_ref)
        return carry

    def causal_tile(jt, carry):
        k0, s = scores(jt, tk)
        pos = k0 + lax.broadcasted_iota(i32, s.shape, 0)
        _flash_step(jnp.where(pos <= t_col, s, NEG_INF), ksvt_ref[:, pl.ds(k0, tk)], m_ref, l_ref, acc_ref)
        return carry

    n_past = start // tkp
    lax.fori_loop(0, n_past, past_tile, 0)
    lax.fori_loop(n_past * (tkp // tk), (start + qb + tk - 1) // tk, causal_tile, 0)
    o_s = acc_ref[...] * (1.0 / jnp.maximum(l_ref[...], 1e-30))

    ks = pl.multiple_of(jnp.clip(start - WINDOW, 0, seq_len - wk), qb)
    s = _dot(kwk_ref[pl.ds(ks, wk), :], qzt)
    dpos = t_col - (ks + lax.broadcasted_iota(i32, s.shape, 0))
    pw, inv_w = _col_softmax(s, lax.bitcast_convert_type(dpos, jnp.uint32) < WINDOW, every_column_valid=True)
    o_w = _dot(kwvt_ref[:, pl.ds(ks, wk)], pw) * inv_w

    gt = jax.nn.sigmoid(gate_ref[...]).T
    out = jnp.zeros((qb, NSA_WIDTH), f32)
    for hg in range(NSA_HEADS):
        cs = slice(hg * qb, (hg + 1) * qb)
        mix = gt[3 * hg:3 * hg + 1] * o_c[:, cs] + gt[3 * hg + 1:3 * hg + 2] * o_s[:, cs] + gt[3 * hg + 2:3 * hg + 3] * o_w[:, cs]
        out = out + _dot(mix.T.astype(bf16), psel_ref[hg])
    o_ref[...] = out.astype(o_ref.dtype)


def _nsa_prompt(nq3, ng3, kc, vct, ksk, ksvt, kwk, kwvt):
    b_sz, seq_len, _ = nq3.shape
    qb = math.gcd(seq_len, NSA_QBLOCK)
    tk = 2 * qb
    tkp = 2 * tk if seq_len % (2 * tk) == 0 else tk
    assert seq_len % tk == 0 and tk % SEL_BLOCK == 0
    wk = min(WINDOW + qb, seq_len)
    r = kc.shape[1]
    n_sel = seq_len // SEL_BLOCK
    n_cmp = seq_len // CMP_STRIDE - CMP_LEN // CMP_STRIDE + 1
    psel = jnp.asarray(_np_psel(), bf16)
    mt = jnp.asarray(_np_cmp_to_sel_t(n_cmp, n_sel, n_sel, r), bf16)
    nc = NSA_HEADS * qb
    per_seq = lambda s: pl.BlockSpec((None,) + s, lambda b, i: (b, 0, 0))
    return pl.pallas_call(
        functools.partial(_nsa_prompt_kernel, qb=qb, seq_len=seq_len, tk=tk, tkp=tkp, wk=wk),
        grid=(b_sz, seq_len // qb),
        in_specs=[pl.BlockSpec((None, qb, NSA_WIDTH), lambda b, i: (b, i, 0)),
                  pl.BlockSpec((None, qb, LANES), lambda b, i: (b, i, 0)),
                  per_seq((r, FEAT)), per_seq((FEAT, r)),
                  per_seq((seq_len, FEAT)), per_seq((FEAT, seq_len)),
                  per_seq((seq_len, FEAT)), per_seq((FEAT, seq_len)),
                  pl.BlockSpec((NSA_HEADS, FEAT, NSA_WIDTH), lambda b, i: (0, 0, 0)),
                  pl.BlockSpec((n_sel, r), lambda b, i: (0, 0))],
        out_specs=pl.BlockSpec((None, qb, NSA_WIDTH), lambda b, i: (b, i, 0)),
        out_shape=SDS((b_sz, seq_len, NSA_WIDTH), bf16),
        scratch_shapes=[pltpu.VMEM((n_sel, nc), f32), pltpu.VMEM((1, nc), f32), pltpu.VMEM((1, nc), f32),
                        pltpu.VMEM((FEAT, nc), f32)],
        compiler_params=_cp("arbitrary", "arbitrary"), name="nsa_prompt",
    )(nq3, ng3, kc, vct, ksk, ksvt, kwk, kwvt, psel, mt)


NCS = LANES


def _page_halves(page_ref):
    k = jnp.concatenate([page_ref[0, h] for h in range(NSA_KV_HEADS)], axis=0)
    v = jnp.concatenate([page_ref[1, h] for h in range(NSA_KV_HEADS)], axis=0)
    return k, v


def _nsa_s1_kernel(pt_ref, *refs, n_pages, pps, ts, win_buf, rows):
    pages = refs[:pps]
    (newk_ref, newv_ref, q_ref, gate_ref, swin_ref, kwn_ref, pe_ref, w_ref, psel_ref, mt_ref, gsum_ref,
     qzt_ref, part_ref, bias_ref, xk_ref, xv_ref) = refs[pps:]
    j = pl.program_id(1)
    past = n_pages * PAGE_SIZE
    n_tail = xk_ref.shape[0] - past

    @pl.when(j == 0)
    def _fill_tail():
        for x_ref, new_ref in ((xk_ref, newk_ref), (xv_ref, newv_ref)):
            x_ref[pl.ds(past, n_tail), :] = jnp.zeros((n_tail, FEAT), f32)
            x_ref[pl.ds(past, ts), :] = new_ref[...]

    for k in range(pps):
        r0 = pl.multiple_of((j * pps + k) * PAGE_SIZE, PAGE_SIZE)
        kt, vt = _page_halves(pages[k])
        xk_ref[pl.ds(r0, PAGE_SIZE), :] = kt.T
        xv_ref[pl.ds(r0, PAGE_SIZE), :] = vt.T

    @pl.when(j == pl.num_programs(1) - 1)
    def _finish():
        n_rows = NSA_HEADS * ts
        qz = jnp.concatenate([_dot_nt(q_ref[...], psel_ref[hg]) for hg in range(NSA_HEADS)]
                             + [jnp.zeros((NCS - n_rows, FEAT), f32)], axis=0)
        qzt = (qz.T * Q_SCALE).astype(bf16)
        qzt_ref[...] = qzt
        t_col = past + lax.broadcasted_iota(i32, (1, NCS), 1) % ts
        total = past + ts
        n_sel = -(-total // SEL_BLOCK)
        n_cmp = n_sel * SEL_BLOCK // CMP_STRIDE - CMP_LEN // CMP_STRIDE + 1

        s = _dot(_compress_half(xk_ref, pe_ref, w_ref, 0, rows).astype(bf16), qzt)
        n_io = lax.broadcasted_iota(i32, s.shape, 0)
        valid = (n_io * CMP_STRIDE + (CMP_LEN - 1) <= t_col) & (n_io < n_cmp)
        pc, inv_c = _col_softmax(s, valid)
        o_c = _dot(_compress_half(xv_ref, pe_ref, w_ref, 1, rows).T.astype(bf16), pc) * inv_c
        imp_all = _dot(mt_ref[...], pc) * inv_c
        hi = imp_all.astype(bf16)
        r1 = imp_all - hi.astype(f32)
        mid = r1.astype(bf16)
        lo = (r1 - mid.astype(f32)).astype(bf16)
        gs = gsum_ref[...]
        imp = _dot(hi, gs) + _dot(mid, gs) + _dot(lo, gs)
        bias_ref[...] = _select_bias(imp, t_col, n_sel)

        wkt, wvt = _page_halves(swin_ref)
        kwn = jnp.concatenate([kwn_ref[...], jnp.zeros((ts, NSA_KV_WIDTH), f32)], axis=0).astype(bf16)
        s = jnp.concatenate([_dot_tn(wkt.astype(bf16), qzt), _dot(kwn[:, :FEAT], qzt)], axis=0)
        w_row = lax.broadcasted_iota(i32, s.shape, 0)
        w_pos = total - (win_buf + ts) + w_row
        dpos = t_col - w_pos
        pw, inv_w = _col_softmax(s, (dpos >= 0) & (dpos < WINDOW) & (w_pos >= 0) & (w_row < win_buf + ts))
        o_w = (_dot(wvt.astype(bf16), pw[:win_buf]) + _dot_tn(kwn[:, FEAT:], pw[win_buf:])) * inv_w

        sig = jax.nn.sigmoid(gate_ref[...])
        oc_r, ow_r = o_c.T, o_w.T
        part_ref[...] = jnp.concatenate(
            [sig[:, 3 * hg:3 * hg + 1] * oc_r[hg * ts:(hg + 1) * ts] + sig[:, 3 * hg + 2:3 * hg + 3] * ow_r[hg * ts:(hg + 1) * ts]
             for hg in range(NSA_HEADS)], axis=0)


def _nsa_s2_kernel(pt_ref, *refs, n_pages, pps, ts):
    pages = refs[:pps]
    news_ref, qzt_ref, bias_ref, part_ref, gate_ref, psel_ref, o_ref, m_ref, l_ref, acc_ref = refs[pps:]
    j = pl.program_id(1)
    past = n_pages * PAGE_SIZE
    qzt = qzt_ref[...]
    bpp = PAGE_SIZE // SEL_BLOCK

    @pl.when(j == 0)
    def _init():
        _flash_init(m_ref, l_ref, acc_ref)

    halves = [_page_halves(pages[k]) for k in range(pps)]
    kt = jnp.concatenate([h[0] for h in halves], axis=1).astype(bf16)
    vt = jnp.concatenate([h[1] for h in halves], axis=1).astype(bf16)
    _flash_step(_dot_tn(kt, qzt) + _block_bias(bias_ref, j * pps * bpp, pps * bpp, NCS), vt, m_ref, l_ref, acc_ref)

    @pl.when(j == pl.num_programs(1) - 1)
    def _finish():
        t_col = past + lax.broadcasted_iota(i32, (1, NCS), 1) % ts
        new = jnp.concatenate([news_ref[...], jnp.zeros((PAGE_SIZE - ts, NSA_KV_WIDTH), f32)], axis=0)
        s = _dot(new[:, :FEAT].astype(bf16), qzt) + _block_bias(bias_ref, n_pages * bpp, bpp, NCS)
        pos = past + lax.broadcasted_iota(i32, s.shape, 0)
        _flash_step(jnp.where(pos <= t_col, s, NEG_INF), new[:, FEAT:].T.astype(bf16), m_ref, l_ref, acc_ref)
        o_s = (acc_ref[...] * (1.0 / jnp.maximum(l_ref[...], 1e-30))).T
        sig = jax.nn.sigmoid(gate_ref[...])
        part = part_ref[...]
        out = jnp.zeros((ts, NSA_WIDTH), f32)
        for hg in range(NSA_HEADS):
            rs = slice(hg * ts, (hg + 1) * ts)
            mix = part[rs] + sig[:, 3 * hg + 1:3 * hg + 2] * o_s[rs]
            out = out + _dot(mix.astype(bf16), psel_ref[hg])
        o_ref[...] = out.astype(o_ref.dtype)


def _nsa_sample(nq3, ng3, kc3, vc3, kvs3, kvw3, cache_c_t, cache_s_t, swin_t, page_table, cw):
    bs, ts, _ = nq3.shape
    n_pages = page_table.shape[1]
    win_buf = swin_t.shape[-1]
    assert ts == SUBLANES and PAGE_SIZE % SEL_BLOCK == 0 and NSA_HEADS * ts <= NCS
    pps = math.gcd(n_pages, 32)
    steps = n_pages // pps
    past = n_pages * PAGE_SIZE
    total = past + ts
    n_sel = -(-total // SEL_BLOCK)
    n_cmp = n_sel * SEL_BLOCK // CMP_STRIDE - CMP_LEN // CMP_STRIDE + 1
    rows = -(-(n_cmp + 1) // SUBLANES) * SUBLANES
    x_rows = rows * CMP_STRIDE
    rb = -(-(n_sel + 1) // SUBLANES) * SUBLANES
    psel = jnp.asarray(_np_psel(), bf16)
    mt = jnp.asarray(_np_cmp_to_sel_t(n_cmp, n_sel, rb, rows), bf16)
    col = np.arange(NCS)
    gsum = ((col[:, None] // (NSA_GROUP * ts) == col[None, :] // (NSA_GROUP * ts)) & (col[:, None] % ts == col[None, :] % ts))
    gsum = jnp.asarray(gsum.astype(np.float32), bf16)
    pt_flat = page_table.reshape(-1).astype(i32)
    pe, w = cw
    page_shape = (2, NSA_KV_HEADS, NSA_HD, PAGE_SIZE)

    def page_spec(k):
        return pl.BlockSpec((None,) + page_shape, lambda b, j, pt, k=k: (pt[b * n_pages + j * pps + k], 0, 0, 0, 0))

    per_seq = lambda s: pl.BlockSpec((None,) + s, lambda b, j, pt: (b,) + (0,) * len(s))
    const = lambda s: pl.BlockSpec(s, lambda b, j, pt: (0,) * len(s))

    qzt, part, bias = pl.pallas_call(
        functools.partial(_nsa_s1_kernel, n_pages=n_pages, pps=pps, ts=ts, win_buf=win_buf, rows=rows),
        grid_spec=pltpu.PrefetchScalarGridSpec(
            num_scalar_prefetch=1, grid=(bs, steps),
            in_specs=[page_spec(k) for k in range(pps)]
            + [per_seq((ts, FEAT)), per_seq((ts, FEAT)), per_seq((ts, NSA_WIDTH)), per_seq((ts, LANES)),
               per_seq((2, NSA_KV_HEADS, NSA_HD, win_buf)), per_seq((ts, NSA_KV_WIDTH)), const(pe.shape), const(w.shape),
               const((NSA_HEADS, FEAT, NSA_WIDTH)), const((rb, rows)), const((NCS, NCS))],
            out_specs=[per_seq((FEAT, NCS)), per_seq((NSA_HEADS * ts, FEAT)), per_seq((rb, NCS))],
            scratch_shapes=[pltpu.VMEM((x_rows, FEAT), f32), pltpu.VMEM((x_rows, FEAT), f32)]),
        out_shape=[SDS((bs, FEAT, NCS), bf16), SDS((bs, NSA_HEADS * ts, FEAT), f32), SDS((bs, rb, NCS), f32)],
        compiler_params=_cp("arbitrary", "arbitrary"), name="nsa_sample_cmp",
    )(pt_flat, *([cache_c_t] * pps), kc3, vc3, nq3, ng3, swin_t, kvw3, pe, w, psel, mt, gsum)

    return pl.pallas_call(
        functools.partial(_nsa_s2_kernel, n_pages=n_pages, pps=pps, ts=ts),
        grid_spec=pltpu.PrefetchScalarGridSpec(
            num_scalar_prefetch=1, grid=(bs, steps),
            in_specs=[page_spec(k) for k in range(pps)]
            + [per_seq((ts, NSA_KV_WIDTH)), per_seq((FEAT, NCS)), per_seq((rb, NCS)), per_seq((NSA_HEADS * ts, FEAT)),
               per_seq((ts, LANES)), const((NSA_HEADS, FEAT, NSA_WIDTH))],
            out_specs=per_seq((ts, NSA_WIDTH)),
            scratch_shapes=[pltpu.VMEM((1, NCS), f32), pltpu.VMEM((1, NCS), f32), pltpu.VMEM((FEAT, NCS), f32)]),
        out_shape=SDS((bs, ts, NSA_WIDTH), bf16),
        compiler_params=_cp("arbitrary", "arbitrary"), name="nsa_sample_sel",
    )(pt_flat, *([cache_s_t] * pps), kvs3, qzt, bias, part, ng3, psel)


def _outproj_kernel(gla_ref, nsa_ref, x_ref, gm_ref, sh_ref, sc_ref, gf_ref, wo_ref, wrt_ref, br_ref,
                    x1_ref, h2_ref, idx_ref, tw_ref):
    y = _dot(gla_ref[...], wo_ref[:GLA_WIDTH, :]) + _dot(nsa_ref[...], wo_ref[GLA_WIDTH:, :])
    x1 = x_ref[...] + gm_ref[...] * y
    x1_ref[...] = x1
    h2 = _rms(x1, gf_ref[...]) * (1.0 + sc_ref[...]) + sh_ref[...]
    h2_ref[...] = h2
    s = _dot_nt(wrt_ref[...], h2.astype(bf16)) + br_ref[...]
    e_io = lax.broadcasted_iota(i32, s.shape, 0)
    vals, idxs = [], []
    for _ in range(TOP_K):
        mx = jnp.max(s, axis=0, keepdims=True)
        ix = jnp.min(jnp.where(s == mx, e_io, N_EXPERTS), axis=0, keepdims=True)
        vals.append(mx)
        idxs.append(ix)
        s = jnp.where(e_io == ix, -jnp.inf, s)
    ex = [jnp.exp(v - vals[0]) for v in vals]
    den = ex[0] + ex[1] + ex[2] + ex[3]
    idx_ref[...] = jnp.concatenate(idxs, axis=0)
    tw_ref[...] = jnp.concatenate([e / den for e in ex], axis=0)


def _outproj(gla_o, nsa_o, x2d, mod, g_ffn, w_out_b, w_router_t, b_router):
    t, tm = x2d.shape[0], mod.tm
    row = lambda i: (i, 0)
    const = lambda s: pl.BlockSpec(s, lambda i: (0, 0))
    return pl.pallas_call(
        _outproj_kernel, grid=(t // tm,),
        in_specs=[pl.BlockSpec((tm, GLA_WIDTH), row), pl.BlockSpec((tm, NSA_WIDTH), row), pl.BlockSpec((tm, D_MODEL), row),
                  mod.spec(2), mod.spec(3), mod.spec(4), const((1, D_MODEL)), const((MIX_WIDTH, D_MODEL)),
                  const((N_EXPERTS, D_MODEL)), const((N_EXPERTS, 1))],
        out_specs=[pl.BlockSpec((tm, D_MODEL), row), pl.BlockSpec((tm, D_MODEL), row),
                   pl.BlockSpec((TOP_K, tm), lambda i: (0, i)), pl.BlockSpec((TOP_K, tm), lambda i: (0, i))],
        out_shape=[SDS((t, D_MODEL), f32), SDS((t, D_MODEL), f32), SDS((TOP_K, t), i32), SDS((TOP_K, t), f32)],
        compiler_params=_cp("arbitrary"), name="outproj_router",
    )(gla_o, nsa_o, x2d, mod.arr, mod.arr, mod.arr, g_ffn.reshape(1, D_MODEL), w_out_b, w_router_t,
      b_router.reshape(N_EXPERTS, 1))


def _moe_kernel(be_ref, nv_ref, tokm_hbm, slotm_hbm, h2_hbm, wg_ref, bg_ref, wu_ref, bu_ref, wd_ref, bd_ref, out_hbm,
                tok_sm, slot_sm, xbuf, ybuf, wbuf, tsem, lsem, gsem, ssem):
    i = pl.program_id(0)
    nv = nv_ref[0]
    tm = xbuf.shape[1]
    pad_block = tokm_hbm.shape[0] - 1
    n_slab = 4
    sw, rg = D_EXPERT // n_slab, tm // n_slab

    def tok_copy(b):
        return pltpu.make_async_copy(tokm_hbm.at[b], tok_sm, tsem)

    def slot_copy(b):
        return pltpu.make_async_copy(slotm_hbm.at[b], slot_sm, lsem)

    def gather_row(par, r):
        pltpu.make_async_copy(h2_hbm.at[pl.ds(tok_sm[0, r], 1)], xbuf.at[par, pl.ds(r, 1)],
                              gsem.at[par]).start(priority=r % 2)

    def scatter_row(par, r):
        pltpu.make_async_copy(ybuf.at[par, pl.ds(r, 1)], out_hbm.at[pl.ds(slot_sm[0, r], 1)],
                              ssem.at[par]).start(priority=r % 2)

    def wait_gather(par):
        pltpu.make_async_copy(h2_hbm.at[pl.ds(0, tm)], xbuf.at[par], gsem.at[par]).wait()

    def wait_scatter(par):
        pltpu.make_async_copy(ybuf.at[par], out_hbm.at[pl.ds(0, tm)], ssem.at[par]).wait()

    @pl.when(i == 0)
    def _prologue():
        tok_copy(0).start()
        slot_copy(pad_block).start()
        ybuf[1] = jnp.zeros(ybuf.shape[1:], f32)
        tok_copy(0).wait()
        for r in range(tm):
            gather_row(0, r)
        tok_copy(1).start()

    @pl.when((i < nv) & ((i == 0) | (be_ref[i] != be_ref[jnp.maximum(i - 1, 0)])))
    def _new_expert():
        wbuf[0] = wg_ref[...].astype(bf16)
        wbuf[1] = wu_ref[...].astype(bf16)
        wbuf[2] = wd_ref[...].astype(bf16)

    for par in range(2):
        @pl.when((i < nv) & (i % 2 == par))
        def _run():
            wait_gather(par)
            tok_copy(i + 1).wait()
            slot_copy(jnp.where(i == 0, pad_block, i - 1)).wait()

            @pl.when(i >= 1)
            def _free_ybuf():
                wait_scatter(par)

            x = xbuf[par].astype(bf16)
            y = jnp.zeros((tm, D_MODEL), f32)
            for s in range(n_slab):
                cs = slice(s * sw, (s + 1) * sw)
                gate = jnp.minimum(_dot(x, wbuf[0, :, cs]) + bg_ref[:, cs], SWIGLU_LIMIT)
                up = jnp.clip(_dot(x, wbuf[1, :, cs]) + bu_ref[:, cs], -SWIGLU_LIMIT, SWIGLU_LIMIT)
                act = ((up + 1.0) * gate * jax.nn.sigmoid(SWIGLU_ALPHA * gate)).astype(bf16)
                y = y + _dot(act, wbuf[2, cs, :])
                for r in range(s * rg, (s + 1) * rg):
                    gather_row(1 - par, r)
                    scatter_row(1 - par, r)
            ybuf[par] = y + bd_ref[...]

            @pl.when(i + 2 <= nv)
            def _next_tokens():
                tok_copy(i + 2).start()

            slot_copy(i).start()

    @pl.when(i == nv)
    def _last_scatter():
        slot_copy(nv - 1).wait()
        for par in range(2):
            @pl.when(nv % 2 == par)
            def _():
                wait_gather(par)
                wait_scatter(par)
                for r in range(tm):
                    scatter_row(1 - par, r)
                wait_scatter(1 - par)


def _moe_experts(h2_all, tok_meta, slot_meta, block_e, n_valid, n_slots, wg, bg, wu, bu, wd, bd):
    n_blocks, _, tm = tok_meta.shape
    assert D_MODEL == D_EXPERT
    any_spec = pl.BlockSpec(memory_space=pl.ANY)
    wspec = lambda a, b: pl.BlockSpec((None, a, b), lambda i, be, nv: (be[i], 0, 0))
    return pl.pallas_call(
        _moe_kernel,
        grid_spec=pltpu.PrefetchScalarGridSpec(
            num_scalar_prefetch=2, grid=(n_blocks,),
            in_specs=[any_spec, any_spec, any_spec,
                      wspec(D_MODEL, D_EXPERT), wspec(1, D_EXPERT), wspec(D_MODEL, D_EXPERT), wspec(1, D_EXPERT),
                      wspec(D_EXPERT, D_MODEL), wspec(1, D_MODEL)],
            out_specs=any_spec,
            scratch_shapes=[pltpu.SMEM((1, tm), i32), pltpu.SMEM((1, tm), i32), pltpu.VMEM((2, tm, D_MODEL), f32),
                            pltpu.VMEM((2, tm, D_MODEL), f32), pltpu.VMEM((3, D_MODEL, D_EXPERT), bf16),
                            pltpu.SemaphoreType.DMA, pltpu.SemaphoreType.DMA,
                            pltpu.SemaphoreType.DMA((2,)), pltpu.SemaphoreType.DMA((2,))]),
        out_shape=SDS((n_slots, D_MODEL), f32),
        compiler_params=pltpu.CompilerParams(dimension_semantics=("arbitrary",), vmem_limit_bytes=MOE_VMEM_LIMIT_BYTES),
        name="moe_experts",
    )(block_e, n_valid, tok_meta, slot_meta, h2_all, wg, bg.reshape(N_EXPERTS, 1, D_EXPERT), wu,
      bu.reshape(N_EXPERTS, 1, D_EXPERT), wd, bd.reshape(N_EXPERTS, 1, D_MODEL))


def _moe_plan(idx_t, tm, t_pad):
    n_tok = idx_t.shape[1]
    n_assign = n_tok * TOP_K
    flat_e = idx_t.reshape(n_assign)
    order = jnp.argsort(flat_e).astype(i32)
    counts = jnp.sum((flat_e[:, None] == jnp.arange(N_EXPERTS, dtype=i32)[None, :]).astype(i32), axis=0)
    padded = (counts + tm - 1) // tm * tm
    start = jnp.cumsum(counts) - counts
    pad_end = jnp.cumsum(padded)
    pad_start = pad_end - padded
    n_blocks = (n_assign + N_EXPERTS * (tm - 1) + tm - 1) // tm + 1
    blk0 = jnp.arange(n_blocks, dtype=i32) * tm
    block_e = jnp.minimum(jnp.sum((pad_end[None, :] <= blk0[:, None]).astype(i32), axis=1), N_EXPERTS - 1).astype(i32)
    n_valid = (pad_end[-1] // tm).astype(i32).reshape(1)
    r_in = jnp.arange(tm, dtype=i32)[None, :]
    j = blk0[:, None] + r_in - pad_start[block_e][:, None]
    valid = (j < counts[block_e][:, None]) & (blk0[:, None] < pad_end[-1])
    a = order[jnp.clip(start[block_e][:, None] + j, 0, n_assign - 1)]
    tok = a % n_tok
    slot = (a // n_tok) * t_pad + tok
    tok_meta = jnp.where(valid, tok, 0).astype(i32).reshape(n_blocks, 1, tm)
    slot_meta = jnp.where(valid, slot, TOP_K * t_pad + r_in).astype(i32).reshape(n_blocks, 1, tm)
    return tok_meta, slot_meta, block_e, n_valid


def _final_kernel(x1_ref, *refs):
    y_refs, (tw_ref, gate_ref, g_ref, o_ref) = refs[:TOP_K], refs[TOP_K:]
    tw = tw_ref[...]
    f = tw[:, 0:1] * y_refs[0][...]
    for k in range(1, TOP_K):
        f = f + tw[:, k:k + 1] * y_refs[k][...]
    o_ref[...] = _rms(x1_ref[...] + gate_ref[...] * f, g_ref[...])


def _final(x1, y4, tw, mod, g_final, row0, t_pad):
    t, tm = x1.shape[0], mod.tm
    assert row0 % tm == 0 and t_pad % tm == 0
    b0 = row0 // tm
    row = lambda i: (i, 0)
    y_specs = [pl.BlockSpec((tm, D_MODEL), lambda i, k=k: (k * (t_pad // tm) + b0 + i, 0)) for k in range(TOP_K)]
    return pl.pallas_call(
        _final_kernel, grid=(t // tm,),
        in_specs=[pl.BlockSpec((tm, D_MODEL), row)] + y_specs
        + [pl.BlockSpec((tm, TOP_K), lambda i: (i + b0, 0)), mod.spec(5), pl.BlockSpec((1, D_MODEL), lambda i: (0, 0))],
        out_specs=pl.BlockSpec((tm, D_MODEL), row), out_shape=SDS((t, D_MODEL), f32),
        compiler_params=_cp("arbitrary"), name="combine_final_norm",
    )(x1, *([y4] * TOP_K), tw, mod.arr, g_final.reshape(1, D_MODEL))


def _prep_w_in(w_in):
    gq, gk, gv, gr, ga, nq, nkc, nks, nkw, ng = jnp.split(w_in, np.cumsum(IN_SPLITS)[:-1].tolist(), axis=1)
    pad = lambda a: jnp.pad(a, ((0, 0), (0, LANES - a.shape[1])))
    w_p = jnp.concatenate([gq, gk, gv, gr, nq, nkc, nks, nkw, pad(ga), pad(ng)], axis=1).astype(bf16)
    w_kv_t = jnp.concatenate([nkc, nks, nkw], axis=1).T.astype(bf16)
    return w_p, w_kv_t


def _feature_major(a, rows_axis):
    return jnp.moveaxis(a, rows_axis, -1)


def kernel(x_prompt, x_sample, c_prompt, c_sample, cache_cmp, cache_sel, state_win, state_gla, page_table, w_ada, b_ada, g_mix, g_ffn, w_in, w_a2, b_a, g_gla, phi_pe, phi_w, w_out, w_router, b_router, w_gate, b_gate, w_up, b_up, w_down, b_down, g_final):
    bp, sp = x_prompt.shape[:2]
    bs, ts = x_sample.shape[:2]
    tp, tsn = bp * sp, bs * ts
    win_buf = state_win.shape[2]
    l = 0

    mod = _adaln(jnp.concatenate([c_prompt, c_sample], axis=0), w_ada[l], b_ada[l])
    tm_p = math.gcd(sp, 512)
    tm_s = math.gcd(tsn, 512)
    mod_p = _Mod(mod[:bp].reshape(bp, 6, 1, D_MODEL), False, tm_p, sp // tm_p)
    mod_s = _Mod(jnp.repeat(mod[bp:].reshape(bs, 6, D_MODEL).transpose(1, 0, 2), ts, axis=1), True, tm_s, None)

    w_in_p, w_kv_t = _prep_w_in(w_in[l])
    cw = _prep_compress(phi_pe[l], phi_w[l])
    w_out_b = w_out[l].astype(bf16)
    w_router_t = w_router[l].T.astype(bf16)
    xp2, xs2 = x_prompt.reshape(tp, D_MODEL), x_sample.reshape(tsn, D_MODEL)

    (gq, gk, gv, gr, nq, kc, vc, ga, ng, kvc_t, kvs_t, kvw_t, ksk, kwk, ksvt, kwvt) = _inproj(
        xp2, mod_p, g_mix[l], w_in_p, w_kv_t, sp)
    gla_o, st_p = _gla(gq, gk, gv, gr, ga, jnp.zeros((bp, GLA_HEADS, GLA_DV, GLA_DK), f32), w_a2[l], b_a[l], g_gla[l], bp, sp)
    kcm, vct = _cmp_prompt(kc.reshape(bp, sp, FEAT), vc.reshape(bp, sp, FEAT), cw)
    nsa_o = _nsa_prompt(nq.reshape(bp, sp, NSA_WIDTH), ng.reshape(bp, sp, LANES), kcm, vct,
                        ksk.reshape(bp, sp, FEAT), ksvt, kwk.reshape(bp, sp, FEAT), kwvt)
    x1_p, h2_p, idx_p, tw_p = _outproj(gla_o.reshape(tp, GLA_WIDTH), nsa_o.reshape(tp, NSA_WIDTH), xp2, mod_p,
                                       g_ffn[l], w_out_b, w_router_t, b_router[l])
    token_major = lambda a: jnp.moveaxis(a, -1, 1)[None]
    new_cmp_p = token_major(kvc_t)
    new_sel_p = token_major(kvs_t)
    new_win_p = token_major(jnp.pad(kvw_t, ((0, 0),) * 4 + ((win_buf, 0),))[..., -win_buf:])
    new_gla_p = jnp.swapaxes(st_p, 2, 3)[None]

    gq, gk, gv, gr, nq, kc, vc, ks, vs, kw, vw, ga, ng = _inproj(xs2, mod_s, g_mix[l], w_in_p)
    gla_os, st_s = _gla(gq, gk, gv, gr, ga, jnp.swapaxes(state_gla[l], 2, 3), w_a2[l], b_a[l], g_gla[l], bs, ts)
    r3 = lambda a: a.reshape(bs, ts, a.shape[-1])
    kvc3, kvs3, kvw3 = (jnp.concatenate([r3(k_), r3(v_)], axis=-1) for k_, v_ in ((kc, vc), (ks, vs), (kw, vw)))
    swin_t = _feature_major(state_win[l], 1)
    nsa_os = _nsa_sample(r3(nq), r3(ng), r3(kc), r3(vc), kvs3, kvw3,
                         _feature_major(cache_cmp[l], 1), _feature_major(cache_sel[l], 1), swin_t, page_table, cw)
    x1_s, h2_s, idx_s, tw_s = _outproj(gla_os.reshape(tsn, GLA_WIDTH), nsa_os.reshape(tsn, NSA_WIDTH), xs2, mod_s,
                                       g_ffn[l], w_out_b, w_router_t, b_router[l])
    kv_row = (2, NSA_KV_HEADS, NSA_HD)
    new_cmp_s = kvc3.reshape((1, bs, ts) + kv_row)
    new_sel_s = kvs3.reshape((1, bs, ts) + kv_row)
    kw_t = _feature_major(kvw3.reshape((bs, ts) + kv_row), 1)
    new_win_s = token_major(jnp.concatenate([swin_t, kw_t], axis=-1)[..., -win_buf:])
    new_gla_s = jnp.swapaxes(st_s, 2, 3)[None]

    n_tok = tp + tsn
    t_pad = -(-n_tok // tm_p) * tm_p
    tok_meta, slot_meta, block_e, n_valid = _moe_plan(jnp.concatenate([idx_p, idx_s], axis=1), MOE_TM, t_pad)
    y4 = _moe_experts(jnp.concatenate([h2_p, h2_s], axis=0), tok_meta, slot_meta, block_e, n_valid,
                      TOP_K * t_pad + MOE_TM, w_gate[l], b_gate[l], w_up[l], b_up[l], w_down[l], b_down[l])
    tw_all = jnp.concatenate([tw_p, tw_s], axis=1).T
    y_p = _final(x1_p, y4, tw_all, mod_p, g_final, 0, t_pad).reshape(bp, sp, D_MODEL)
    y_s = _final(x1_s, y4, tw_all, mod_s, g_final, tp, t_pad).reshape(bs, ts, D_MODEL)
    return (y_p, y_s, new_cmp_p, new_sel_p, new_win_p, new_gla_p, new_cmp_s, new_sel_s, new_win_s, new_gla_s)
```
